```python
import math
import jax, jax.numpy as jnp
from jax import lax
import numpy as np

D_MODEL = 1024
BATCH = 4
SEQ = 4096
DEPTH = 2

N_MEM = 256
RMS_EPS = 1e-6
MAX_POS_OFFSET = 1024
MASK_VALUE = -1e30

SSM_GROUPS = 32
SSM_GROUP_CH = 16
SSM_WIDTH = SSM_GROUPS * SSM_GROUP_CH
SSM_STATE = 64

MLA_HEADS = 8
MLA_Q_RANK = 512
MLA_KV_RANK = 256
MLA_NOPE = 64
MLA_ROPE = 32
MLA_V = 64
MLA_WIDTH = MLA_HEADS * MLA_V
ROPE_THETA = 10000.0
Q_BLOCK = 128

HG_HEADS = 4
HG_DK = 128
HG_DV = 128
HG_WIDTH = HG_HEADS * HG_DK
HG_CHUNK = 64

X_HEADS = 4
X_HEAD_DIM = 128
X_WIDTH = X_HEADS * X_HEAD_DIM

D_FF = -(-(8 * D_MODEL) // (3 * 256)) * 256

N_BRANCH = 3
IN_SPLITS = [SSM_WIDTH, MLA_Q_RANK, MLA_KV_RANK, MLA_ROPE,
             HG_HEADS * HG_DK, HG_HEADS * HG_DK, HG_HEADS * HG_DV, HG_HEADS * HG_DV,
             N_BRANCH * D_MODEL]
D_IN = sum(IN_SPLITS)

kernel_name = "hybrid_s5_mla_hgrn2_gated_block"


def rmsnorm(x, g):
    xf = x.astype(jnp.float32)
    y = xf * lax.rsqrt(jnp.mean(xf * xf, axis=-1, keepdims=True) + RMS_EPS)
    return (y * g.astype(jnp.float32)).astype(x.dtype)


def rope_tables(positions):
    half = MLA_ROPE // 2
    inv_freq = ROPE_THETA ** (-jnp.arange(half, dtype=jnp.float32) / half)
    ang = positions.astype(jnp.float32)[..., None] * inv_freq
    return jnp.cos(ang), jnp.sin(ang)


def apply_rope(x, cos, sin):
    xf = x.astype(jnp.float32)
    x1, x2 = jnp.split(xf, 2, axis=-1)
    return jnp.concatenate([x1 * cos - x2 * sin, x2 * cos + x1 * sin], axis=-1).astype(x.dtype)


def s5_mixer(u, lam_re, lam_im, b_re, b_im, c_re, c_im, d_skip, log_step, w_glu):
    bsz, s, _ = u.shape
    uf = u.astype(jnp.float32).reshape(bsz, s, SSM_GROUPS, SSM_GROUP_CH)
    lam = lax.complex(lam_re.astype(jnp.float32), lam_im.astype(jnp.float32))
    step = jnp.exp(log_step.astype(jnp.float32))[:, None]
    lam_bar = jnp.exp(lam * step)
    b_mat = lax.complex(b_re.astype(jnp.float32), b_im.astype(jnp.float32))
    b_bar = ((lam_bar - 1.0) / lam)[..., None] * b_mat
    bu = jnp.einsum('bsgh,gph->bsgp', uf.astype(jnp.complex64), b_bar)
    a = jnp.broadcast_to(lam_bar, bu.shape)

    def combine(left, right):
        a_l, b_l = left
        a_r, b_r = right
        return a_r * a_l, a_r * b_l + b_r

    _, states = lax.associative_scan(combine, (a, bu), axis=1)
    c_mat = lax.complex(c_re.astype(jnp.float32), c_im.astype(jnp.float32))
    y = jnp.real(jnp.einsum('bsgp,ghp->bsgh', states, c_mat)) + d_skip.astype(jnp.float32) * uf
    y = jax.nn.gelu(y.reshape(bsz, s, SSM_WIDTH)).astype(u.dtype)
    z_out, z_gate = jnp.split(y @ w_glu, 2, axis=-1)
    return z_out * jax.nn.sigmoid(z_gate)


def blocked_causal_attention(q, k, v, scale):
    bsz, s, h, dqk = q.shape
    nb = s // Q_BLOCK
    qb = q.reshape(bsz, nb, Q_BLOCK, h, dqk).transpose(1, 0, 2, 3, 4)
    kpos = jnp.arange(s)

    def one_block(args):
        q_blk, start = args
        sc = jnp.einsum('bqhd,bkhd->bhqk', q_blk, k, preferred_element_type=jnp.float32) * scale
        qpos = start + jnp.arange(Q_BLOCK)
        sc = jnp.where(kpos[None, :] <= qpos[:, None], sc, MASK_VALUE)
        p = jax.nn.softmax(sc, axis=-1).astype(v.dtype)
        return jnp.einsum('bhqk,bkhd->bqhd', p, v)

    ob = lax.map(one_block, (qb, jnp.arange(nb) * Q_BLOCK))
    return ob.transpose(1, 0, 2, 3, 4).reshape(bsz, s, h, v.shape[-1])


def mla_mixer(q_lat, kv_lat, k_rope, cos, sin, q_norm, kv_norm, w_uq, w_ukv, w_o):
    bsz, s, _ = q_lat.shape
    q = (rmsnorm(q_lat, q_norm) @ w_uq).reshape(bsz, s, MLA_HEADS, MLA_NOPE + MLA_ROPE)
    q_nope, q_pe = q[..., :MLA_NOPE], q[..., MLA_NOPE:]
    q_pe = apply_rope(q_pe, cos[:, :, None, :], sin[:, :, None, :])
    kv = (rmsnorm(kv_lat, kv_norm) @ w_ukv).reshape(bsz, s, MLA_HEADS, MLA_NOPE + MLA_V)
    k_nope, v = kv[..., :MLA_NOPE], kv[..., MLA_NOPE:]
    k_pe = apply_rope(k_rope, cos, sin)
    k = jnp.concatenate([k_nope, jnp.broadcast_to(k_pe[:, :, None, :], (bsz, s, MLA_HEADS, MLA_ROPE))], axis=-1)
    q = jnp.concatenate([q_nope, q_pe], axis=-1)
    o = blocked_causal_attention(q, k, v, 1.0 / math.sqrt(MLA_NOPE + MLA_ROPE))
    return o.reshape(bsz, s, MLA_WIDTH) @ w_o


def hgrn2_mixer(q, f_logit, i_in, g, lb, g_norm, w_o):
    bsz, s, _ = q.shape
    n_chunks = s // HG_CHUNK

    def to_chunks(t):
        return t.reshape(bsz, n_chunks, HG_CHUNK, HG_HEADS, -1).transpose(1, 0, 3, 2, 4)

    lbf = lb.astype(jnp.float32)
    sig = jax.nn.sigmoid(f_logit.astype(jnp.float32))
    f = lbf + (1.0 - lbf) * sig
    log_f = jnp.log(f)
    k = 1.0 - f
    xs = (to_chunks(jax.nn.silu(q.astype(jnp.float32))), to_chunks(k),
          to_chunks(i_in.astype(jnp.float32)), to_chunks(log_f))
    causal = jnp.tril(jnp.ones((HG_CHUNK, HG_CHUNK), dtype=bool))[:, :, None]

    def chunk_step(state, chunk):
        q_n, k_n, v_n, lf_n = chunk
        b = jnp.cumsum(lf_n, axis=2)
        diff = b[:, :, :, None, :] - b[:, :, None, :, :]
        decay = jnp.where(causal, jnp.exp(jnp.where(causal, diff, 0.0)), 0.0)
        attn = jnp.einsum('bhtd,bhsd,bhtsd->bhts', q_n, k_n, decay)
        o_n = jnp.einsum('bhts,bhse->bhte', attn, v_n) + jnp.einsum('bhtd,bhde->bhte', q_n * jnp.exp(b), state)
        b_last = b[:, :, -1, :]
        k_dec = k_n * jnp.exp(b_last[:, :, None, :] - b)
        state = jnp.exp(b_last)[..., None] * state + jnp.einsum('bhsd,bhse->bhde', k_dec, v_n)
        return state, o_n

    s0 = jnp.zeros((bsz, HG_HEADS, HG_DK, HG_DV), jnp.float32)
    _, o = lax.scan(chunk_step, s0, xs)
    o = o.transpose(1, 0, 3, 2, 4).reshape(bsz, s, HG_HEADS, HG_DV)
    gate = g.astype(jnp.float32).reshape(bsz, s, HG_HEADS, HG_DV)
    o = rmsnorm(o, g_norm) * jax.nn.silu(gate)
    return o.reshape(bsz, s, HG_HEADS * HG_DV).astype(g.dtype) @ w_o


def memory_cross_attention(h, mem_n, w_q, w_kv, w_o):
    bsz, s, _ = h.shape
    m = mem_n.shape[1]
    q = (h @ w_q).reshape(bsz, s, X_HEADS, X_HEAD_DIM)
    k, v = jnp.split(mem_n @ w_kv, 2, axis=-1)
    k = k.reshape(bsz, m, X_HEADS, X_HEAD_DIM)
    v = v.reshape(bsz, m, X_HEADS, X_HEAD_DIM)
    sc = jnp.einsum('bqhd,bkhd->bhqk', q, k, preferred_element_type=jnp.float32) / math.sqrt(X_HEAD_DIM)
    p = jax.nn.softmax(sc, axis=-1).astype(v.dtype)
    o = jnp.einsum('bhqk,bkhd->bqhd', p, v)
    return o.reshape(bsz, s, X_WIDTH) @ w_o


def swiglu_ffn(h, w_gate_up, w_down):
    gt, up = jnp.split(h @ w_gate_up, 2, axis=-1)
    return (jax.nn.silu(gt) * up) @ w_down


def setup_inputs(seed: int = 0) -> dict:
    key = jax.random.key(seed)
    keys = list(jax.random.split(key, 48))
    L = DEPTH

    def nrm(shape, scale):
        return scale * jax.random.normal(keys.pop(), shape, jnp.float32)

    def gain(shape):
        return 1.0 + 0.05 * jax.random.normal(keys.pop(), shape, jnp.float32)

    x = nrm((BATCH, SEQ, D_MODEL), 1.0)
    mem = nrm((BATCH, N_MEM, D_MODEL), 1.0)
    offsets = jax.random.randint(keys.pop(), (BATCH, 1), 0, MAX_POS_OFFSET, dtype=jnp.int32)
    positions = (offsets + jnp.arange(SEQ, dtype=jnp.int32)[None, :]).astype(jnp.int32)
    state_idx = jnp.arange(SSM_STATE, dtype=jnp.float32)
    return {
        "x": x,
        "mem": mem,
        "positions": positions,
        "norm_mix": gain((L, D_MODEL)),
        "w_in": nrm((L, D_MODEL, D_IN), D_MODEL ** -0.5),
        "ssm_lam_re": -0.5 + nrm((L, SSM_GROUPS, SSM_STATE), 0.01),
        "ssm_lam_im": math.pi * state_idx + nrm((L, SSM_GROUPS, SSM_STATE), 0.01),
        "ssm_b_re": nrm((L, SSM_GROUPS, SSM_STATE, SSM_GROUP_CH), (2.0 * SSM_GROUP_CH) ** -0.5),
        "ssm_b_im": nrm((L, SSM_GROUPS, SSM_STATE, SSM_GROUP_CH), (2.0 * SSM_GROUP_CH) ** -0.5),
        "ssm_c_re": nrm((L, SSM_GROUPS, SSM_GROUP_CH, SSM_STATE), (2.0 * SSM_STATE) ** -0.5),
        "ssm_c_im": nrm((L, SSM_GROUPS, SSM_GROUP_CH, SSM_STATE), (2.0 * SSM_STATE) ** -0.5),
        "ssm_d": nrm((L, SSM_GROUPS, SSM_GROUP_CH), 1.0),
        "ssm_log_step": jax.random.uniform(keys.pop(), (L, SSM_GROUPS), jnp.float32, math.log(1e-3), math.log(1e-1)),
        "ssm_w_glu": nrm((L, SSM_WIDTH, 2 * D_MODEL), SSM_WIDTH ** -0.5),
        "mla_q_norm": gain((L, MLA_Q_RANK)),
        "mla_kv_norm": gain((L, MLA_KV_RANK)),
        "mla_w_uq": nrm((L, MLA_Q_RANK, MLA_HEADS * (MLA_NOPE + MLA_ROPE)), MLA_Q_RANK ** -0.5),
        "mla_w_ukv": nrm((L, MLA_KV_RANK, MLA_HEADS * (MLA_NOPE + MLA_V)), MLA_KV_RANK ** -0.5),
        "mla_w_o": nrm((L, MLA_WIDTH, D_MODEL), MLA_WIDTH ** -0.5),
        "hg_lb": nrm((L, HG_HEADS * HG_DK), 1.0),
        "hg_g_norm": gain((L, HG_DV)),
        "hg_w_o": nrm((L, HG_HEADS * HG_DV, D_MODEL), (HG_HEADS * HG_DV) ** -0.5),
        "w_out": nrm((L, D_MODEL, D_MODEL), D_MODEL ** -0.5),
        "norm_cross": gain((L, D_MODEL)),
        "norm_mem": gain((L, D_MODEL)),
        "x_w_q": nrm((L, D_MODEL, X_WIDTH), D_MODEL ** -0.5),
        "x_w_kv": nrm((L, D_MODEL, 2 * X_WIDTH), D_MODEL ** -0.5),
        "x_w_o": nrm((L, X_WIDTH, D_MODEL), X_WIDTH ** -0.5),
        "norm_ffn": gain((L, D_MODEL)),
        "ffn_w_gate_up": nrm((L, D_MODEL, 2 * D_FF), D_MODEL ** -0.5),
        "ffn_w_down": nrm((L, D_FF, D_MODEL), D_FF ** -0.5),
        "norm_final": gain((D_MODEL,)),
    }


def reference(x, mem, positions, norm_mix, w_in, ssm_lam_re, ssm_lam_im, ssm_b_re, ssm_b_im,
              ssm_c_re, ssm_c_im, ssm_d, ssm_log_step, ssm_w_glu, mla_q_norm, mla_kv_norm,
              mla_w_uq, mla_w_ukv, mla_w_o, hg_lb, hg_g_norm, hg_w_o, w_out, norm_cross, norm_mem,
              x_w_q, x_w_kv, x_w_o, norm_ffn, ffn_w_gate_up, ffn_w_down, norm_final):
    bsz, s, _ = x.shape
    cos, sin = rope_tables(positions)
    lb_p = jax.nn.softmax(hg_lb.astype(jnp.float32), axis=0)
    lower_bounds = jnp.cumsum(lb_p, axis=0) - lb_p[0:1]
    split_at = np.cumsum(IN_SPLITS)[:-1].tolist()

    for l in range(DEPTH):
        h = rmsnorm(x, norm_mix[l])
        (u_ssm, q_lat, kv_lat, k_rope, hg_q, hg_f, hg_i, hg_g, gate_logits) = jnp.split(h @ w_in[l], split_at, axis=-1)
        y_ssm = s5_mixer(u_ssm, ssm_lam_re[l], ssm_lam_im[l], ssm_b_re[l], ssm_b_im[l],
                         ssm_c_re[l], ssm_c_im[l], ssm_d[l], ssm_log_step[l], ssm_w_glu[l])
        y_mla = mla_mixer(q_lat, kv_lat, k_rope, cos, sin, mla_q_norm[l], mla_kv_norm[l],
                          mla_w_uq[l], mla_w_ukv[l], mla_w_o[l])
        y_hg = hgrn2_mixer(hg_q, hg_f, hg_i, hg_g, lower_bounds[l], hg_g_norm[l], hg_w_o[l])
        gates = jax.nn.sigmoid(gate_logits.astype(jnp.float32)).reshape(bsz, s, N_BRANCH, D_MODEL)
        merged = (gates[:, :, 0] * y_ssm.astype(jnp.float32)
                  + gates[:, :, 1] * y_mla.astype(jnp.float32)
                  + gates[:, :, 2] * y_hg.astype(jnp.float32)).astype(x.dtype)
        x = x + merged @ w_out[l]
        x = x + memory_cross_attention(rmsnorm(x, norm_cross[l]), rmsnorm(mem, norm_mem[l]),
                                       x_w_q[l], x_w_kv[l], x_w_o[l])
        x = x + swiglu_ffn(rmsnorm(x, norm_ffn[l]), ffn_w_gate_up[l], ffn_w_down[l])
    return rmsnorm(x, norm_final)
```

```python
import functools
import math

import jax
import jax.numpy as jnp
from jax import lax
from jax.experimental import pallas as pl
from jax.experimental.pallas import tpu as pltpu

F32 = jnp.float32
BF16 = jnp.bfloat16

RMS_EPS = 1e-6
LOG2E = 1.4426950408889634

SSM_GROUPS = 32
SSM_GROUP_CH = 16
SSM_STATE = 64
SSM_WIDTH = SSM_GROUPS * SSM_GROUP_CH
SSM_CH = SSM_GROUPS * SSM_STATE
SSM_QUARTERS = 4

MLA_HEADS = 8
MLA_Q_RANK = 512
MLA_KV_RANK = 256
MLA_NOPE = 64
MLA_ROPE = 32
MLA_V = 64
ROPE_THETA = 10000.0
HEAD_PAD = 128

HG_HEADS = 4
HG_D = 128
HG_WIDTH = HG_HEADS * HG_D
HG_CHUNK = 64
HG_SUB = 16

X_HEADS = 4
X_HEAD_DIM = 128
X_WIDTH = X_HEADS * X_HEAD_DIM

N_BRANCH = 3

VMEM_LIMIT = 56 * 1024 * 1024

IN_TILE = 256
S5_TIME = 128
MLA_BLOCK = 512
MID_TILE = 256
FFN_TILE = 512
FFN_CHUNK = 1408


def _dot(a, b):
    return jnp.dot(a, b, preferred_element_type=F32)


def _dot_nt(a, b):
    return lax.dot_general(a, b, (((1,), (1,)), ((), ())), preferred_element_type=F32)


def _dot_tn(a, b):
    return lax.dot_general(a, b, (((0,), (0,)), ((), ())), preferred_element_type=F32)


def _rms(x, g):
    return x * lax.rsqrt(jnp.mean(x * x, axis=-1, keepdims=True) + RMS_EPS) * g


def _sigmoid(x):
    return 1.0 / (1.0 + jnp.exp(-x))


def _const_spec(shape):
    nd = len(shape)
    return pl.BlockSpec(shape, lambda *_: (0,) * nd, pipeline_mode=pl.Buffered(1))


def _params(sem):
    return pltpu.CompilerParams(dimension_semantics=sem, vmem_limit_bytes=VMEM_LIMIT)


C_SSM = 0
C_QLAT = C_SSM + SSM_WIDTH
C_KVLAT = C_QLAT + MLA_Q_RANK
C_KPE = C_KVLAT + MLA_KV_RANK
C_KPE_ROT = C_KPE + HEAD_PAD
C_HQ = C_KPE_ROT + HEAD_PAD
C_HF = C_HQ + HG_WIDTH
C_HI = C_HF + HG_WIDTH
C_HG = C_HI + HG_WIDTH
C_GATE = C_HG + HG_WIDTH


def _in_proj_kernel(x_ref, pos_ref, g_ref, w_ref, invf_ref, qn_ref, wuq_ref, wuqr_ref, kvn_ref,
                    wuk_ref, wuv_ref, lb_ref,
                    u_ref, q_ref, k_ref, v_ref, hq_ref, hf_ref, hi_ref, hg_ref, gate_ref, *, q_scale):
    d_model = x_ref.shape[1]
    h = _rms(x_ref[...], g_ref[...]).astype(BF16)

    def seg(a, b):
        return _dot(h, w_ref[:, a:b])

    u_ref[...] = seg(C_SSM, C_QLAT)

    ang = pos_ref[...].astype(F32) * invf_ref[...]
    cos = jnp.cos(ang)
    sin = jnp.sin(ang)

    qn = _rms(seg(C_QLAT, C_KVLAT), qn_ref[...]).astype(BF16)
    qa = _dot(qn, wuq_ref[...])
    qr = _dot(qn, wuqr_ref[...])
    for hh in range(MLA_HEADS):
        sl = slice(hh * HEAD_PAD, (hh + 1) * HEAD_PAD)
        q_ref[0, hh] = ((qa[:, sl] * cos + qr[:, sl] * sin) * q_scale).astype(BF16)

    kvn = _rms(seg(C_KVLAT, C_KPE), kvn_ref[...]).astype(BF16)
    kn = _dot(kvn, wuk_ref[...])
    vv = _dot(kvn, wuv_ref[...])
    kpe = seg(C_KPE, C_KPE_ROT) * cos + seg(C_KPE_ROT, C_HQ) * sin
    for hh in range(MLA_HEADS):
        sl = slice(hh * HEAD_PAD, (hh + 1) * HEAD_PAD)
        k_ref[0, hh] = (kn[:, sl] + kpe).astype(BF16)
        v_ref[0, hh] = vv[:, sl].astype(BF16)

    hq = seg(C_HQ, C_HF)
    hq_ref[...] = hq * _sigmoid(hq)
    lb = lb_ref[...]
    hf_ref[...] = lb + (1.0 - lb) * _sigmoid(seg(C_HF, C_HI))
    hi_ref[...] = seg(C_HI, C_HG).astype(BF16)
    hg = seg(C_HG, C_GATE)
    hg_ref[...] = hg * _sigmoid(hg)
    gate_ref[...] = _sigmoid(seg(C_GATE, C_GATE + N_BRANCH * d_model))


def _in_proj(x2, pos2, g, w, invf, qn, wuq, wuqr, kvn, wuk, wuv, lb, bsz, seq):
    n_tok, d_model = x2.shape
    t = IN_TILE
    nt = seq // t
    tok = lambda w_: pl.BlockSpec((t, w_), lambda i: (i, 0))
    head = pl.BlockSpec((1, MLA_HEADS, t, HEAD_PAD), lambda i: (i // nt, 0, i % nt, 0))
    head_shape = jax.ShapeDtypeStruct((bsz, MLA_HEADS, seq, HEAD_PAD), BF16)
    q_scale = LOG2E / math.sqrt(MLA_NOPE + MLA_ROPE)
    return pl.pallas_call(
        functools.partial(_in_proj_kernel, q_scale=q_scale),
        grid=(n_tok // t,),
        in_specs=[tok(d_model), tok(1), _const_spec(g.shape), _const_spec(w.shape), _const_spec(invf.shape),
                  _const_spec(qn.shape), _const_spec(wuq.shape), _const_spec(wuqr.shape), _const_spec(kvn.shape),
                  _const_spec(wuk.shape), _const_spec(wuv.shape), _const_spec(lb.shape)],
        out_specs=[tok(SSM_WIDTH), head, head, head, tok(HG_WIDTH), tok(HG_WIDTH), tok(HG_WIDTH), tok(HG_WIDTH),
                   tok(N_BRANCH * d_model)],
        out_shape=[jax.ShapeDtypeStruct((n_tok, SSM_WIDTH), F32), head_shape, head_shape, head_shape,
                   jax.ShapeDtypeStruct((n_tok, HG_WIDTH), F32), jax.ShapeDtypeStruct((n_tok, HG_WIDTH), F32),
                   jax.ShapeDtypeStruct((n_tok, HG_WIDTH), BF16), jax.ShapeDtypeStruct((n_tok, HG_WIDTH), F32),
                   jax.ShapeDtypeStruct((n_tok, N_BRANCH * d_model), F32)],
        compiler_params=_params(("parallel",)),
        name="in_proj",
    )(x2, pos2, g, w, invf, qn, wuq, wuqr, kvn, wuk, wuv, lb)


def _s5_kernel(u_ref, wb_ref, a_ref, bsw_ref, wc_ref, d_ref, wglu_ref, y_ref, se_ref, so_ref, st_ref):
    rows = u_ref.shape[0]
    d_model = y_ref.shape[1]
    qw = SSM_CH // SSM_QUARTERS
    uw = SSM_WIDTH // SSM_QUARTERS
    half = 4

    @pl.when(pl.program_id(0) == 0)
    def _():
        st_ref[...] = jnp.zeros_like(st_ref)

    u = u_ref[...]
    ub = u.astype(BF16)
    low = (lax.broadcasted_iota(jnp.int32, (rows, qw), 0) % 8) < half
    for q in range(SSM_QUARTERS):
        o = _dot(ub[:, q * uw:(q + 1) * uw], wb_ref[q])
        re = o[:, :qw]
        im = o[:, qw:]
        se_ref[:, q * qw:(q + 1) * qw] = jnp.where(low, re, pltpu.roll(im, half, 0))
        so_ref[:, q * qw:(q + 1) * qw] = jnp.where(low, pltpu.roll(re, rows - half, 0), im)

    n_blk = 2
    cw = SSM_CH // n_blk
    for cb in range(n_blk):
        cs = slice(cb * cw, (cb + 1) * cw)
        a = a_ref[:, cs]
        bs = bsw_ref[:, cs]

        def body(v, s, cs=cs, a=a, bs=bs):
            r = pl.multiple_of(v * 8, 8)
            s = a * s + bs * pltpu.roll(s, half, 0) + se_ref[pl.ds(r, 8), cs]
            se_ref[pl.ds(r, 8), cs] = s
            s = a * s + bs * pltpu.roll(s, half, 0) + so_ref[pl.ds(r, 8), cs]
            so_ref[pl.ds(r, 8), cs] = s
            return s

        st_ref[:, cs] = lax.fori_loop(0, rows // 8, body, st_ref[:, cs], unroll=4)

    yw = uw
    low_y = low[:, :yw]
    pieces = []
    for q in range(SSM_QUARTERS):
        oe = _dot(se_ref[:, q * qw:(q + 1) * qw].astype(BF16), wc_ref[q])
        oo = _dot(so_ref[:, q * qw:(q + 1) * qw].astype(BF16), wc_ref[q])
        ye = oe[:, :yw] + pltpu.roll(oe[:, yw:], rows - half, 0)
        yo = pltpu.roll(oo[:, :yw], half, 0) + oo[:, yw:]
        pieces.append(jnp.where(low_y, ye, yo))
    y = jnp.concatenate(pieces, axis=1) + d_ref[...] * u
    y = jax.nn.gelu(y).astype(BF16)
    z = _dot(y, wglu_ref[...])
    y_ref[...] = z[:, :d_model] * _sigmoid(z[:, d_model:])


def _s5(u_tb, wb, a, bsw, wc, d, wglu, d_model):
    n_rows = u_tb.shape[0]
    rows = S5_TIME * 4
    return pl.pallas_call(
        _s5_kernel,
        grid=(n_rows // rows,),
        in_specs=[pl.BlockSpec((rows, SSM_WIDTH), lambda i: (i, 0)), _const_spec(wb.shape), _const_spec(a.shape),
                  _const_spec(bsw.shape), _const_spec(wc.shape), _const_spec(d.shape), _const_spec(wglu.shape)],
        out_specs=pl.BlockSpec((rows, d_model), lambda i: (i, 0)),
        out_shape=jax.ShapeDtypeStruct((n_rows, d_model), F32),
        scratch_shapes=[pltpu.VMEM((rows, SSM_CH), F32), pltpu.VMEM((rows, SSM_CH), F32),
                        pltpu.VMEM((8, SSM_CH), F32)],
        compiler_params=_params(("arbitrary",)),
        name="s5",
    )(u_tb, wb, a, bsw, wc, d, wglu)


def _mla_kernel(qi_ref, kj_ref, q_ref, k_ref, v_ref, o_ref, m_ref, l_ref, acc_ref):
    p = pl.program_id(1)
    qi = qi_ref[p]
    kj = kj_ref[p]
    blk = q_ref.shape[2]

    @pl.when(kj == 0)
    def _():
        m_ref[...] = jnp.full_like(m_ref, -1e30)
        l_ref[...] = jnp.zeros_like(l_ref)
        acc_ref[...] = jnp.zeros_like(acc_ref)

    def step(masked):
        def head(hh, carry):
            s = _dot_nt(q_ref[0, hh], k_ref[0, hh])
            if masked:
                row = lax.broadcasted_iota(jnp.int32, (blk, blk), 0)
                col = lax.broadcasted_iota(jnp.int32, (blk, blk), 1)
                s = jnp.where(col <= row, s, -1e30)
            m_prev = m_ref[hh]
            m_new = jnp.maximum(m_prev, jnp.max(s, axis=-1, keepdims=True))
            alpha = jnp.exp2(m_prev - m_new)
            pr = jnp.exp2(s - m_new)
            l_ref[hh] = alpha * l_ref[hh] + jnp.sum(pr, axis=-1, keepdims=True)
            acc_ref[hh] = alpha * acc_ref[hh] + _dot(pr.astype(BF16), v_ref[0, hh])
            m_ref[hh] = m_new
            return carry

        lax.fori_loop(0, MLA_HEADS, head, 0)

    @pl.when(kj < qi)
    def _():
        step(False)

    @pl.when(kj == qi)
    def _():
        step(True)
        o_ref[0] = (acc_ref[...] / l_ref[...]).astype(BF16)


def _mla(q, k, v):
    bsz, nh, seq, hd = q.shape
    blk = MLA_BLOCK
    nb = seq // blk
    pairs = [(i, j) for i in range(nb) for j in range(i + 1)]
    qi = jnp.asarray([p[0] for p in pairs], jnp.int32)
    kj = jnp.asarray([p[1] for p in pairs], jnp.int32)
    qspec = pl.BlockSpec((1, nh, blk, hd), lambda b, p, qi, kj: (b, 0, qi[p], 0))
    kspec = pl.BlockSpec((1, nh, blk, hd), lambda b, p, qi, kj: (b, 0, kj[p], 0))
    return pl.pallas_call(
        _mla_kernel,
        grid_spec=pltpu.PrefetchScalarGridSpec(
            num_scalar_prefetch=2,
            grid=(bsz, len(pairs)),
            in_specs=[qspec, kspec, kspec],
            out_specs=qspec,
            scratch_shapes=[pltpu.VMEM((nh, blk, 1), F32), pltpu.VMEM((nh, blk, 1), F32),
                            pltpu.VMEM((nh, blk, hd), F32)],
        ),
        out_shape=jax.ShapeDtypeStruct(q.shape, BF16),
        compiler_params=_params(("parallel", "arbitrary")),
        name="mla",
    )(qi, kj, q, k, v)


def _hgrn_kernel(q_ref, f_ref, v_ref, g_ref, gn_ref, o_ref, st_ref, at_ref):
    c = HG_CHUNK
    sub = HG_SUB
    n_sub = c // sub

    @pl.when(pl.program_id(2) == 0)
    def _():
        st_ref[...] = jnp.zeros_like(st_ref)

    q = q_ref[...]
    f = f_ref[...]
    v = v_ref[...]
    k = 1.0 - f
    lf = jnp.log(f)

    row = lax.broadcasted_iota(jnp.int32, (c, c), 0)
    col = lax.broadcasted_iota(jnp.int32, (c, c), 1)
    tri = (col <= row).astype(BF16)
    l1 = lf.astype(BF16)
    r1 = lf - l1.astype(F32)
    l2 = r1.astype(BF16)
    l3 = (r1 - l2.astype(F32)).astype(BF16)
    b = _dot(tri, l1) + _dot(tri, l2) + _dot(tri, l3)

    st = st_ref[...]
    o = _dot_nt((q * jnp.exp(b)).astype(BF16), st.astype(BF16))

    rowd = lax.broadcasted_iota(jnp.int32, (c, HG_D), 0)
    ks_parts = []
    qs_parts = []
    for i in range(1, n_sub):
        r = b[i * sub - 1:i * sub, :]
        ks_parts.append(jnp.where(rowd < i * sub, k * jnp.exp(jnp.where(rowd < i * sub, r - b, 0.0)), 0.0))
        in_blk = (rowd >= i * sub) & (rowd < (i + 1) * sub)
        qs_parts.append(jnp.where(in_blk, q * jnp.exp(jnp.where(in_blk, b - r, 0.0)), 0.0))
    ks_cat = jnp.concatenate(ks_parts, axis=1).astype(BF16)
    qs_cat = jnp.concatenate(qs_parts, axis=1).astype(BF16)
    at_off = _dot_nt(ks_cat, qs_cat)

    lane = lax.broadcasted_iota(jnp.int32, (sub, HG_D), 1)
    srow = lax.broadcasted_iota(jnp.int32, (sub, HG_D), 0)
    for i in range(n_sub):
        rs = slice(i * sub, (i + 1) * sub)
        bi = b[rs]
        ki = k[rs]
        tile = jnp.zeros((sub, HG_D), F32)
        for tt in range(sub):
            t = i * sub + tt
            dec = jnp.exp(jnp.minimum(b[t:t + 1, :] - bi, 0.0))
            a_t = jnp.sum(q[t:t + 1, :] * ki * dec, axis=-1, keepdims=True)
            tile = jnp.where(lane == t, a_t, tile)
        at_ref[rs, :] = jnp.where(lane >= srow + i * sub, tile, 0.0)
    at = (at_ref[:, :c] + at_off).astype(BF16)
    o = o + _dot_tn(at, v)

    bl = b[c - 1:c, :]
    kdec = (k * jnp.exp(bl - b)).astype(BF16)
    st_ref[...] = st * jnp.exp(bl) + _dot_tn(v, kdec)

    on = o * lax.rsqrt(jnp.mean(o * o, axis=-1, keepdims=True) + RMS_EPS) * gn_ref[...]
    o_ref[...] = (on * g_ref[...]).astype(BF16)


def _hgrn(hq, hf, hi, hg, gn, bsz, seq):
    n_tok = hq.shape[0]
    c = HG_CHUNK
    nc = seq // c
    spec = pl.BlockSpec((c, HG_D), lambda b, h, i: (b * nc + i, h))
    return pl.pallas_call(
        _hgrn_kernel,
        grid=(bsz, HG_HEADS, nc),
        in_specs=[spec, spec, spec, spec, pl.BlockSpec((1, HG_D), lambda b, h, i: (0, 0))],
        out_specs=spec,
        out_shape=jax.ShapeDtypeStruct((n_tok, HG_WIDTH), BF16),
        scratch_shapes=[pltpu.VMEM((HG_D, HG_D), F32), pltpu.VMEM((c, HG_D), F32)],
        compiler_params=_params(("parallel", "parallel", "arbitrary")),
        name="hgrn",
    )(hq, hf, hi, hg, gn)


def _memkv_kernel(mem_ref, g_ref, w_ref, kv_ref):
    kv_ref[...] = _dot(_rms(mem_ref[...], g_ref[...]).astype(BF16), w_ref[...]).astype(BF16)


def _memkv(mem2, g, w, n_mem):
    n_rows, d_model = mem2.shape
    return pl.pallas_call(
        _memkv_kernel,
        grid=(n_rows // n_mem,),
        in_specs=[pl.BlockSpec((n_mem, d_model), lambda i: (i, 0)), _const_spec(g.shape), _const_spec(w.shape)],
        out_specs=pl.BlockSpec((n_mem, 2 * X_WIDTH), lambda i: (i, 0)),
        out_shape=jax.ShapeDtypeStruct((n_rows, 2 * X_WIDTH), BF16),
        compiler_params=_params(("parallel",)),
        name="memkv",
    )(mem2, g, w)


def _mid_kernel(x_ref, ys_ref, om_ref, oh_ref, gate_ref, kv_ref, wmo_ref, who_ref, wout_ref, gx_ref, wq_ref,
                wxo_ref, o_ref, *, x_scale):
    d_model = x_ref.shape[1]
    y_mla = _dot(om_ref[0, 0], wmo_ref[0])
    for hh in range(1, MLA_HEADS):
        y_mla = y_mla + _dot(om_ref[0, hh], wmo_ref[hh])
    y_hg = _dot(oh_ref[...], who_ref[...])
    merged = (gate_ref[:, :d_model] * ys_ref[...] + gate_ref[:, d_model:2 * d_model] * y_mla
              + gate_ref[:, 2 * d_model:] * y_hg)
    x = x_ref[...] + _dot(merged.astype(BF16), wout_ref[...])

    q = (_dot(_rms(x, gx_ref[...]).astype(BF16), wq_ref[...]) * x_scale).astype(BF16)
    outs = []
    for hh in range(X_HEADS):
        sl = slice(hh * X_HEAD_DIM, (hh + 1) * X_HEAD_DIM)
        s = _dot_nt(q[:, sl], kv_ref[:, sl])
        pr = jnp.exp2(s - jnp.max(s, axis=-1, keepdims=True))
        den = jnp.sum(pr, axis=-1, keepdims=True)
        vh = kv_ref[:, X_WIDTH + hh * X_HEAD_DIM:X_WIDTH + (hh + 1) * X_HEAD_DIM]
        outs.append((_dot(pr.astype(BF16), vh) / den).astype(BF16))
    o_ref[...] = x + _dot(jnp.concatenate(outs, axis=1), wxo_ref[...])


def _mid(x2, ys, om, oh, gates, kv, wmo, who, wout, gx, wq, wxo, seq, n_mem):
    n_tok, d_model = x2.shape
    t = MID_TILE
    nt = seq // t
    tok = lambda w_: pl.BlockSpec((t, w_), lambda i: (i, 0))
    return pl.pallas_call(
        functools.partial(_mid_kernel, x_scale=LOG2E / math.sqrt(X_HEAD_DIM)),
        grid=(n_tok // t,),
        in_specs=[tok(d_model), tok(d_model),
                  pl.BlockSpec((1, MLA_HEADS, t, HEAD_PAD), lambda i: (i // nt, 0, i % nt, 0)),
                  tok(HG_WIDTH), tok(N_BRANCH * d_model),
                  pl.BlockSpec((n_mem, 2 * X_WIDTH), lambda i: (i // nt, 0)),
                  _const_spec(wmo.shape), _const_spec(who.shape), _const_spec(wout.shape), _const_spec(gx.shape),
                  _const_spec(wq.shape), _const_spec(wxo.shape)],
        out_specs=tok(d_model),
        out_shape=jax.ShapeDtypeStruct((n_tok, d_model), F32),
        compiler_params=_params(("parallel",)),
        name="mid",
    )(x2, ys, om, oh, gates, kv, wmo, who, wout, gx, wq, wxo)


def _ffn_kernel(x_ref, g_ref, wgu_ref, wd_ref, gf_ref, o_ref, *, final_norm):
    d_ff = wd_ref.shape[0]
    x = x_ref[...]
    h = _rms(x, g_ref[...]).astype(BF16)
    y = x
    for c0 in range(0, d_ff, FFN_CHUNK):
        gt = _dot(h, wgu_ref[:, c0:c0 + FFN_CHUNK])
        up = _dot(h, wgu_ref[:, d_ff + c0:d_ff + c0 + FFN_CHUNK])
        act = (gt * _sigmoid(gt) * up).astype(BF16)
        y = y + _dot(act, wd_ref[c0:c0 + FFN_CHUNK, :])
    if final_norm:
        y = _rms(y, gf_ref[...])
    o_ref[...] = y


def _ffn(x2, g, wgu, wd, gf, final_norm):
    n_tok, d_model = x2.shape
    t = FFN_TILE
    tok = pl.BlockSpec((t, d_model), lambda i: (i, 0))
    return pl.pallas_call(
        functools.partial(_ffn_kernel, final_norm=final_norm),
        grid=(n_tok // t,),
        in_specs=[tok, _const_spec(g.shape), _const_spec(wgu.shape), _const_spec(wd.shape), _const_spec(gf.shape)],
        out_specs=tok,
        out_shape=jax.ShapeDtypeStruct((n_tok, d_model), F32),
        compiler_params=_params(("parallel",)),
        name="ffn",
    )(x2, g, wgu, wd, gf)


def _in_proj_weights(w_in, d_model):
    o = 0
    parts = {}
    for name, width in (("ssm", SSM_WIDTH), ("qlat", MLA_Q_RANK), ("kvlat", MLA_KV_RANK), ("kpe", MLA_ROPE),
                        ("hq", HG_WIDTH), ("hf", HG_WIDTH), ("hi", HG_WIDTH), ("hg", HG_WIDTH),
                        ("gate", N_BRANCH * d_model)):
        parts[name] = w_in[:, o:o + width]
        o += width
    half = MLA_ROPE // 2
    kpe = parts["kpe"]
    kpe_rot = jnp.concatenate([-kpe[:, half:], kpe[:, :half]], axis=1)
    pad = lambda a: jnp.pad(a, ((0, 0), (MLA_NOPE, HEAD_PAD - MLA_NOPE - MLA_ROPE)))
    cols = [parts["ssm"], parts["qlat"], parts["kvlat"], pad(kpe), pad(kpe_rot), parts["hq"], parts["hf"],
            parts["hi"], parts["hg"], parts["gate"]]
    return jnp.concatenate(cols, axis=1).astype(BF16)


def _mla_weights(w_uq, w_ukv, w_o):
    half = MLA_ROPE // 2
    dq = MLA_NOPE + MLA_ROPE
    rank = w_uq.shape[0]
    wq = w_uq.reshape(rank, MLA_HEADS, dq)
    pe = wq[:, :, MLA_NOPE:]
    pe_rot = jnp.concatenate([-pe[:, :, half:], pe[:, :, :half]], axis=2)
    zpad = jnp.zeros((rank, MLA_HEADS, HEAD_PAD - dq), F32)
    wuq = jnp.concatenate([wq, zpad], axis=2).reshape(rank, MLA_HEADS * HEAD_PAD)
    wuqr = jnp.concatenate([jnp.zeros_like(wq[:, :, :MLA_NOPE]), pe_rot, zpad], axis=2).reshape(rank, -1)
    kvr = w_ukv.shape[0]
    wkv = w_ukv.reshape(kvr, MLA_HEADS, MLA_NOPE + MLA_V)
    wuk = jnp.pad(wkv[:, :, :MLA_NOPE], ((0, 0), (0, 0), (0, HEAD_PAD - MLA_NOPE))).reshape(kvr, -1)
    wuv = jnp.pad(wkv[:, :, MLA_NOPE:], ((0, 0), (0, 0), (0, HEAD_PAD - MLA_V))).reshape(kvr, -1)
    wmo = jnp.pad(w_o.reshape(MLA_HEADS, MLA_V, -1), ((0, 0), (0, HEAD_PAD - MLA_V), (0, 0)))
    return wuq.astype(BF16), wuqr.astype(BF16), wuk.astype(BF16), wuv.astype(BF16), wmo.astype(BF16)


def _s5_weights(lam_re, lam_im, b_re, b_im, c_re, c_im, log_step):
    step = jnp.exp(log_step)[:, None]
    mag = jnp.exp(lam_re * step)
    lbr = mag * jnp.cos(lam_im * step)
    lbi = mag * jnp.sin(lam_im * step)
    den = lam_re * lam_re + lam_im * lam_im
    cr = ((lbr - 1.0) * lam_re + lbi * lam_im) / den
    ci = (lbi * lam_re - (lbr - 1.0) * lam_im) / den
    bbr = cr[..., None] * b_re - ci[..., None] * b_im
    bbi = cr[..., None] * b_im + ci[..., None] * b_re
    gq = SSM_GROUPS // SSM_QUARTERS
    eye = jnp.eye(gq, dtype=F32)

    def blockdiag_in(m):
        m = m.reshape(SSM_QUARTERS, gq, SSM_STATE, SSM_GROUP_CH)
        return jnp.einsum("ab,qaph->qahbp", eye, m).reshape(SSM_QUARTERS, gq * SSM_GROUP_CH, gq * SSM_STATE)

    def blockdiag_out(m):
        m = m.reshape(SSM_QUARTERS, gq, SSM_GROUP_CH, SSM_STATE)
        return jnp.einsum("ab,qahp->qapbh", eye, m).reshape(SSM_QUARTERS, gq * SSM_STATE, gq * SSM_GROUP_CH)

    wb = jnp.concatenate([blockdiag_in(bbr), blockdiag_in(bbi)], axis=2).astype(BF16)
    wc = jnp.concatenate([blockdiag_out(c_re), -blockdiag_out(c_im)], axis=2).astype(BF16)
    lr = lbr.reshape(1, SSM_CH)
    li = lbi.reshape(1, SSM_CH)
    a = jnp.broadcast_to(lr, (8, SSM_CH))
    bsw = jnp.concatenate([jnp.broadcast_to(-li, (4, SSM_CH)), jnp.broadcast_to(li, (4, SSM_CH))], axis=0)
    return wb, a, bsw, wc


def kernel(x, mem, positions, norm_mix, w_in, ssm_lam_re, ssm_lam_im, ssm_b_re, ssm_b_im, ssm_c_re, ssm_c_im, ssm_d, ssm_log_step, ssm_w_glu, mla_q_norm, mla_kv_norm, mla_w_uq, mla_w_ukv, mla_w_o, hg_lb, hg_g_norm, hg_w_o, w_out, norm_cross, norm_mem, x_w_q, x_w_kv, x_w_o, norm_ffn, ffn_w_gate_up, ffn_w_down, norm_final):
    bsz, seq, d_model = x.shape
    depth = w_in.shape[0]
    n_mem = mem.shape[1]
    assert bsz == 4, "the s5 scan packs 4 batches x (re, im) on the 8 sublanes"
    n_tok = bsz * seq

    half = MLA_ROPE // 2
    inv_freq = ROPE_THETA ** (-jnp.arange(half, dtype=F32) / half)
    invf = jnp.zeros((1, HEAD_PAD), F32).at[0, MLA_NOPE:MLA_NOPE + half].set(inv_freq)
    invf = invf.at[0, MLA_NOPE + half:MLA_NOPE + MLA_ROPE].set(inv_freq)
    lb_p = jax.nn.softmax(hg_lb.astype(F32), axis=0)
    lower_bounds = jnp.cumsum(lb_p, axis=0) - lb_p[0:1]

    x2 = x.reshape(n_tok, d_model)
    mem2 = mem.reshape(bsz * n_mem, d_model)
    pos2 = positions.reshape(n_tok, 1)
    row = lambda a: a.reshape(1, -1)

    for l in range(depth):
        w_l = _in_proj_weights(w_in[l], d_model)
        wuq, wuqr, wuk, wuv, wmo = _mla_weights(mla_w_uq[l], mla_w_ukv[l], mla_w_o[l])
        u, q, k, v, hq, hf, hi, hg, gates = _in_proj(
            x2, pos2, row(norm_mix[l]), w_l, invf, row(mla_q_norm[l]), wuq, wuqr, row(mla_kv_norm[l]), wuk, wuv,
            row(lower_bounds[l]), bsz, seq)

        wb, a, bsw, wc = _s5_weights(ssm_lam_re[l], ssm_lam_im[l], ssm_b_re[l], ssm_b_im[l], ssm_c_re[l],
                                     ssm_c_im[l], ssm_log_step[l])
        u_tb = u.reshape(bsz, seq, SSM_WIDTH).transpose(1, 0, 2).reshape(n_tok, SSM_WIDTH)
        ys_tb = _s5(u_tb, wb, a, bsw, wc, row(ssm_d[l]), ssm_w_glu[l].astype(BF16), d_model)
        ys = ys_tb.reshape(seq, bsz, d_model).transpose(1, 0, 2).reshape(n_tok, d_model)

        om = _mla(q, k, v)
        oh = _hgrn(hq, hf, hi, hg, row(hg_g_norm[l]), bsz, seq)
        kv = _memkv(mem2, row(norm_mem[l]), x_w_kv[l].astype(BF16), n_mem)
        x2 = _mid(x2, ys, om, oh, gates, kv, wmo, hg_w_o[l].astype(BF16), w_out[l].astype(BF16),
                  row(norm_cross[l]), x_w_q[l].astype(BF16), x_w_o[l].astype(BF16), seq, n_mem)
        x2 = _ffn(x2, row(norm_ffn[l]), ffn_w_gate_up[l].astype(BF16), ffn_w_down[l].astype(BF16),
                  row(norm_final), final_norm=(l == depth - 1))
    return x2.reshape(bsz, seq, d_model)
```

```python
import functools
import math

import jax
import jax.numpy as jnp
from jax import lax
from jax.experimental import pallas as pl
from jax.experimental.pallas import tpu as pltpu

F32 = jnp.float32
BF16 = jnp.bfloat16

RMS_EPS = 1e-6
LOG2E = 1.4426950408889634

SSM_GROUPS = 32
SSM_GROUP_CH = 16
SSM_STATE = 64
SSM_WIDTH = SSM_GROUPS * SSM_GROUP_CH
SSM_CH = SSM_GROUPS * SSM_STATE
SSM_QUARTERS = 4

MLA_HEADS = 8
MLA_Q_RANK = 512
MLA_KV_RANK = 256
MLA_NOPE = 64
MLA_ROPE = 32
MLA_V = 64
ROPE_THETA = 10000.0
HEAD_PAD = 128

HG_HEADS = 4
HG_D = 128
HG_WIDTH = HG_HEADS * HG_D
HG_CHUNK = 64
HG_SUB = 8
HG_STEP_CHUNKS = 2

X_HEADS = 4
X_HEAD_DIM = 128
X_WIDTH = X_HEADS * X_HEAD_DIM

N_BRANCH = 3

VMEM_LIMIT = 56 * 1024 * 1024

IN_TILE = 256
S5_TIME = 128
MLA_BLOCK = 512
MID_TILE = 256
FFN_TILE = 512
FFN_CHUNK = 1408


def _dot(a, b):
    return jnp.dot(a, b, preferred_element_type=F32)


def _dot_nt(a, b):
    return lax.dot_general(a, b, (((1,), (1,)), ((), ())), preferred_element_type=F32)


def _dot_tn(a, b):
    return lax.dot_general(a, b, (((0,), (0,)), ((), ())), preferred_element_type=F32)


def _rms(x, g):
    return x * lax.rsqrt(jnp.mean(x * x, axis=-1, keepdims=True) + RMS_EPS) * g


def _sigmoid(x):
    return 1.0 / (1.0 + jnp.exp(-x))


def _const_spec(shape):
    nd = len(shape)
    return pl.BlockSpec(shape, lambda *_: (0,) * nd, pipeline_mode=pl.Buffered(1))


def _params(sem):
    return pltpu.CompilerParams(dimension_semantics=sem, vmem_limit_bytes=VMEM_LIMIT)


C_SSM = 0
C_QLAT = C_SSM + SSM_WIDTH
C_KVLAT = C_QLAT + MLA_Q_RANK
C_KPE = C_KVLAT + MLA_KV_RANK
C_KPE_ROT = C_KPE + HEAD_PAD
C_HQ = C_KPE_ROT + HEAD_PAD
C_HF = C_HQ + HG_WIDTH
C_HI = C_HF + HG_WIDTH
C_HG = C_HI + HG_WIDTH
C_GATE = C_HG + HG_WIDTH


def _in_proj_kernel(x_ref, pos_ref, g_ref, w_ref, invf_ref, qn_ref, wuq_ref, wuqr_ref, kvn_ref,
                    wuk_ref, wuv_ref, lb_ref,
                    u_ref, q_ref, k_ref, v_ref, hq_ref, hf_ref, hi_ref, hg_ref, gate_ref, *, q_scale):
    d_model = x_ref.shape[1]
    h = _rms(x_ref[...], g_ref[...]).astype(BF16)

    def seg(a, b):
        return _dot(h, w_ref[:, a:b])

    u_ref[...] = seg(C_SSM, C_QLAT)

    ang = pos_ref[...].astype(F32) * invf_ref[...]
    cos = jnp.cos(ang)
    sin = jnp.sin(ang)

    qn = _rms(seg(C_QLAT, C_KVLAT), qn_ref[...]).astype(BF16)
    qa = _dot(qn, wuq_ref[...])
    qr = _dot(qn, wuqr_ref[...])
    for hh in range(MLA_HEADS):
        sl = slice(hh * HEAD_PAD, (hh + 1) * HEAD_PAD)
        q_ref[0, hh] = ((qa[:, sl] * cos + qr[:, sl] * sin) * q_scale).astype(BF16)

    kvn = _rms(seg(C_KVLAT, C_KPE), kvn_ref[...]).astype(BF16)
    kn = _dot(kvn, wuk_ref[...])
    vv = _dot(kvn, wuv_ref[...])
    kpe = seg(C_KPE, C_KPE_ROT) * cos + seg(C_KPE_ROT, C_HQ) * sin
    ones_lane = (lax.broadcasted_iota(jnp.int32, (1, HEAD_PAD), 1) == MLA_V).astype(F32)
    for hh in range(MLA_HEADS):
        sl = slice(hh * HEAD_PAD, (hh + 1) * HEAD_PAD)
        k_ref[0, hh] = (kn[:, sl] + kpe).astype(BF16)
        v_ref[0, hh] = (vv[:, sl] + ones_lane).T.astype(BF16)

    hq = seg(C_HQ, C_HF)
    hq_ref[...] = hq * _sigmoid(hq)
    lb = lb_ref[...]
    hf_ref[...] = lb + (1.0 - lb) * _sigmoid(seg(C_HF, C_HI))
    hi_ref[...] = seg(C_HI, C_HG).astype(BF16)
    hg = seg(C_HG, C_GATE)
    hg_ref[...] = hg * _sigmoid(hg)
    gate_ref[...] = _sigmoid(seg(C_GATE, C_GATE + N_BRANCH * d_model))


def _in_proj(x2, pos2, g, w, invf, qn, wuq, wuqr, kvn, wuk, wuv, lb, bsz, seq):
    n_tok, d_model = x2.shape
    t = IN_TILE
    nt = seq // t
    tok = lambda w_: pl.BlockSpec((t, w_), lambda i: (i, 0))
    head = pl.BlockSpec((1, MLA_HEADS, t, HEAD_PAD), lambda i: (i // nt, 0, i % nt, 0))
    head_shape = jax.ShapeDtypeStruct((bsz, MLA_HEADS, seq, HEAD_PAD), BF16)
    head_t = pl.BlockSpec((1, MLA_HEADS, HEAD_PAD, t), lambda i: (i // nt, 0, 0, i % nt))
    head_t_shape = jax.ShapeDtypeStruct((bsz, MLA_HEADS, HEAD_PAD, seq), BF16)
    q_scale = LOG2E / math.sqrt(MLA_NOPE + MLA_ROPE)
    return pl.pallas_call(
        functools.partial(_in_proj_kernel, q_scale=q_scale),
        grid=(n_tok // t,),
        in_specs=[tok(d_model), tok(1), _const_spec(g.shape), _const_spec(w.shape), _const_spec(invf.shape),
                  _const_spec(qn.shape), _const_spec(wuq.shape), _const_spec(wuqr.shape), _const_spec(kvn.shape),
                  _const_spec(wuk.shape), _const_spec(wuv.shape), _const_spec(lb.shape)],
        out_specs=[tok(SSM_WIDTH), head, head, head_t, tok(HG_WIDTH), tok(HG_WIDTH), tok(HG_WIDTH), tok(HG_WIDTH),
                   tok(N_BRANCH * d_model)],
        out_shape=[jax.ShapeDtypeStruct((n_tok, SSM_WIDTH), F32), head_shape, head_shape, head_t_shape,
                   jax.ShapeDtypeStruct((n_tok, HG_WIDTH), F32), jax.ShapeDtypeStruct((n_tok, HG_WIDTH), F32),
                   jax.ShapeDtypeStruct((n_tok, HG_WIDTH), BF16), jax.ShapeDtypeStruct((n_tok, HG_WIDTH), F32),
                   jax.ShapeDtypeStruct((n_tok, N_BRANCH * d_model), F32)],
        compiler_params=_params(("parallel",)),
        name="in_proj",
    )(x2, pos2, g, w, invf, qn, wuq, wuqr, kvn, wuk, wuv, lb)


def _s5_kernel(u_ref, wb_ref, a_ref, bsw_ref, wc_ref, d_ref, wglu_ref, y_ref, se_ref, so_ref, st_ref):
    rows = u_ref.shape[0]
    d_model = y_ref.shape[1]
    qw = SSM_CH // SSM_QUARTERS
    uw = SSM_WIDTH // SSM_QUARTERS
    half = 4

    @pl.when(pl.program_id(0) == 0)
    def _():
        st_ref[...] = jnp.zeros_like(st_ref)

    u = u_ref[...]
    ub = u.astype(BF16)
    low = (lax.broadcasted_iota(jnp.int32, (rows, qw), 0) % 8) < half
    for q in range(SSM_QUARTERS):
        o = _dot(ub[:, q * uw:(q + 1) * uw], wb_ref[q])
        re = o[:, :qw]
        im = o[:, qw:]
        se_ref[:, q * qw:(q + 1) * qw] = jnp.where(low, re, pltpu.roll(im, half, 0))
        so_ref[:, q * qw:(q + 1) * qw] = jnp.where(low, pltpu.roll(re, rows - half, 0), im)

    n_blk = 2
    cw = SSM_CH // n_blk
    for cb in range(n_blk):
        cs = slice(cb * cw, (cb + 1) * cw)
        a = a_ref[:, cs]
        bs = bsw_ref[:, cs]

        def body(v, s, cs=cs, a=a, bs=bs):
            r = pl.multiple_of(v * 8, 8)
            s = a * s + bs * pltpu.roll(s, half, 0) + se_ref[pl.ds(r, 8), cs]
            se_ref[pl.ds(r, 8), cs] = s
            s = a * s + bs * pltpu.roll(s, half, 0) + so_ref[pl.ds(r, 8), cs]
            so_ref[pl.ds(r, 8), cs] = s
            return s

        st_ref[:, cs] = lax.fori_loop(0, rows // 8, body, st_ref[:, cs], unroll=4)

    yw = uw
    low_y = low[:, :yw]
    pieces = []
    for q in range(SSM_QUARTERS):
        oe = _dot(se_ref[:, q * qw:(q + 1) * qw].astype(BF16), wc_ref[q])
        oo = _dot(so_ref[:, q * qw:(q + 1) * qw].astype(BF16), wc_ref[q])
        ye = oe[:, :yw] + pltpu.roll(oe[:, yw:], rows - half, 0)
        yo = pltpu.roll(oo[:, :yw], half, 0) + oo[:, yw:]
        pieces.append(jnp.where(low_y, ye, yo))
    y = jnp.concatenate(pieces, axis=1) + d_ref[...] * u
    y = jax.nn.gelu(y).astype(BF16)
    z = _dot(y, wglu_ref[...])
    y_ref[...] = z[:, :d_model] * _sigmoid(z[:, d_model:])


def _s5(u_tb, wb, a, bsw, wc, d, wglu, d_model):
    n_rows = u_tb.shape[0]
    rows = S5_TIME * 4
    return pl.pallas_call(
        _s5_kernel,
        grid=(n_rows // rows,),
        in_specs=[pl.BlockSpec((rows, SSM_WIDTH), lambda i: (i, 0)), _const_spec(wb.shape), _const_spec(a.shape),
                  _const_spec(bsw.shape), _const_spec(wc.shape), _const_spec(d.shape), _const_spec(wglu.shape)],
        out_specs=pl.BlockSpec((rows, d_model), lambda i: (i, 0)),
        out_shape=jax.ShapeDtypeStruct((n_rows, d_model), F32),
        scratch_shapes=[pltpu.VMEM((rows, SSM_CH), F32), pltpu.VMEM((rows, SSM_CH), F32),
                        pltpu.VMEM((8, SSM_CH), F32)],
        compiler_params=_params(("arbitrary",)),
        name="s5",
    )(u_tb, wb, a, bsw, wc, d, wglu)


def _mla_kernel(qi_ref, kj_ref, q_ref, k_ref, vt_ref, o_ref, m_ref, acc_ref):
    p = pl.program_id(1)
    qi = qi_ref[p]
    kj = kj_ref[p]
    blk = q_ref.shape[2]

    @pl.when(kj == 0)
    def _():
        m_ref[...] = jnp.full_like(m_ref, -1e30)
        acc_ref[...] = jnp.zeros_like(acc_ref)

    def step(masked):
        for hh in range(MLA_HEADS):
            s = _dot_nt(k_ref[0, hh], q_ref[0, hh])
            if masked:
                key = lax.broadcasted_iota(jnp.int32, (blk, blk), 0)
                qry = lax.broadcasted_iota(jnp.int32, (blk, blk), 1)
                s = jnp.where(key <= qry, s, -1e30)
            m_prev = m_ref[hh]
            m_new = jnp.maximum(m_prev, jnp.max(s, axis=0, keepdims=True))
            alpha = jnp.exp2(m_prev - m_new)
            pr = jnp.exp2(s - m_new).astype(BF16)
            acc_ref[hh] = alpha * acc_ref[hh] + _dot(vt_ref[0, hh], pr)
            m_ref[hh] = m_new

    @pl.when(kj < qi)
    def _():
        step(False)

    @pl.when(kj == qi)
    def _():
        step(True)
        for hh in range(MLA_HEADS):
            acc = acc_ref[hh]
            o_ref[0, hh] = (acc[:MLA_V, :] / acc[MLA_V:MLA_V + 1, :]).astype(BF16)


def _mla(q, k, vt):
    bsz, nh, seq, hd = q.shape
    blk = MLA_BLOCK
    nb = seq // blk
    pairs = [(i, j) for i in range(nb) for j in range(i + 1)]
    qi = jnp.asarray([p[0] for p in pairs], jnp.int32)
    kj = jnp.asarray([p[1] for p in pairs], jnp.int32)
    qspec = pl.BlockSpec((1, nh, blk, hd), lambda b, p, qi, kj: (b, 0, qi[p], 0))
    kspec = pl.BlockSpec((1, nh, blk, hd), lambda b, p, qi, kj: (b, 0, kj[p], 0))
    vspec = pl.BlockSpec((1, nh, hd, blk), lambda b, p, qi, kj: (b, 0, 0, kj[p]))
    ospec = pl.BlockSpec((1, nh, MLA_V, blk), lambda b, p, qi, kj: (b, 0, 0, qi[p]))
    return pl.pallas_call(
        _mla_kernel,
        grid_spec=pltpu.PrefetchScalarGridSpec(
            num_scalar_prefetch=2,
            grid=(bsz, len(pairs)),
            in_specs=[qspec, kspec, vspec],
            out_specs=ospec,
            scratch_shapes=[pltpu.VMEM((nh, 1, blk), F32), pltpu.VMEM((nh, hd, blk), F32)],
        ),
        out_shape=jax.ShapeDtypeStruct((bsz, nh, MLA_V, seq), BF16),
        compiler_params=_params(("parallel", "arbitrary")),
        name="mla",
    )(qi, kj, q, k, vt)


def _hgrn_head(q, f, v, g, gn, st, ks_ref):
    c = HG_CHUNK
    sub = HG_SUB
    n_sub = c // sub
    k = 1.0 - f
    lf = jnp.log(f) * LOG2E

    row = lax.broadcasted_iota(jnp.int32, (c, c), 0)
    col = lax.broadcasted_iota(jnp.int32, (c, c), 1)
    tri = (col <= row).astype(BF16)
    l1 = lf.astype(BF16)
    r1 = lf - l1.astype(F32)
    l2 = r1.astype(BF16)
    l3 = (r1 - l2.astype(F32)).astype(BF16)
    b = _dot(tri, l1) + _dot(tri, l2) + _dot(tri, l3)

    o = _dot_nt((q * jnp.exp2(b)).astype(BF16), st.astype(BF16))

    lane = lax.broadcasted_iota(jnp.int32, (sub, HG_D), 1)
    trow = lax.broadcasted_iota(jnp.int32, (sub, HG_D), 0)
    bk = b - jnp.log(k) * LOG2E
    rows = []
    for i in range(n_sub):
        rs = slice(i * sub, (i + 1) * sub)
        bi = b[rs]
        qi = q[rs]
        tile = jnp.zeros((sub, HG_D), F32)
        for ss in range(sub):
            s = i * sub + ss
            kdec_s = jnp.exp2(jnp.minimum(bi - bk[s:s + 1, :], 0.0))
            a_s = jnp.sum(qi * kdec_s, axis=-1, keepdims=True)
            tile = jnp.where(lane == s, a_s, tile)
        a_row = jnp.where(trow + i * sub >= lane, tile, 0.0)[:, :c]
        if i > 0:
            r = b[i * sub - 1:i * sub, :]
            ks_ref[i - 1, :i * sub, :] = k[:i * sub] * jnp.exp2(r - b[:i * sub])
            qs = (qi * jnp.exp2(bi - r)).astype(BF16)
            a_row = a_row + _dot_nt(qs, ks_ref[i - 1].astype(BF16))
        rows.append(a_row)
    a = jnp.concatenate(rows, axis=0).astype(BF16)
    o = o + _dot(a, v)

    bl = b[c - 1:c, :]
    kdec = (k * jnp.exp2(bl - b)).astype(BF16)
    st_new = st * jnp.exp2(bl) + _dot_tn(v, kdec)

    on = o * lax.rsqrt(jnp.mean(o * o, axis=-1, keepdims=True) + RMS_EPS) * gn
    return (on * g).astype(BF16), st_new


def _hgrn_kernel(q_ref, f_ref, v_ref, g_ref, gn_ref, o_ref, st_ref, ks_ref):
    @pl.when(pl.program_id(1) == 0)
    def _():
        st_ref[...] = jnp.zeros_like(st_ref)
        ks_ref[...] = jnp.zeros_like(ks_ref)

    gn = gn_ref[...]
    for hh in range(HG_HEADS):
        sl = slice(hh * HG_D, (hh + 1) * HG_D)
        st = st_ref[hh]
        for ci in range(HG_STEP_CHUNKS):
            rs = slice(ci * HG_CHUNK, (ci + 1) * HG_CHUNK)
            o, st = _hgrn_head(q_ref[rs, sl], f_ref[rs, sl], v_ref[rs, sl], g_ref[rs, sl], gn, st,
                               ks_ref.at[hh * HG_STEP_CHUNKS + ci])
            o_ref[rs, sl] = o
        st_ref[hh] = st


def _hgrn(hq, hf, hi, hg, gn, bsz, seq):
    n_tok = hq.shape[0]
    c = HG_CHUNK
    rows = c * HG_STEP_CHUNKS
    nc = seq // rows
    spec = pl.BlockSpec((rows, HG_WIDTH), lambda b, i: (b * nc + i, 0))
    return pl.pallas_call(
        _hgrn_kernel,
        grid=(bsz, nc),
        in_specs=[spec, spec, spec, spec, pl.BlockSpec((1, HG_D), lambda b, i: (0, 0))],
        out_specs=spec,
        out_shape=jax.ShapeDtypeStruct((n_tok, HG_WIDTH), BF16),
        scratch_shapes=[pltpu.VMEM((HG_HEADS, HG_D, HG_D), F32),
                        pltpu.VMEM((HG_HEADS * HG_STEP_CHUNKS, c // HG_SUB - 1, c, HG_D), F32)],
        compiler_params=_params(("parallel", "arbitrary")),
        name="hgrn",
    )(hq, hf, hi, hg, gn)


def _memkv_kernel(mem_ref, g_ref, w_ref, kv_ref):
    kv_ref[...] = _dot(_rms(mem_ref[...], g_ref[...]).astype(BF16), w_ref[...]).astype(BF16)


def _memkv(mem2, g, w, n_mem):
    n_rows, d_model = mem2.shape
    return pl.pallas_call(
        _memkv_kernel,
        grid=(n_rows // n_mem,),
        in_specs=[pl.BlockSpec((n_mem, d_model), lambda i: (i, 0)), _const_spec(g.shape), _const_spec(w.shape)],
        out_specs=pl.BlockSpec((n_mem, 2 * X_WIDTH), lambda i: (i, 0)),
        out_shape=jax.ShapeDtypeStruct((n_rows, 2 * X_WIDTH), BF16),
        compiler_params=_params(("parallel",)),
        name="memkv",
    )(mem2, g, w)


def _mid_kernel(x_ref, ys_ref, om_ref, oh_ref, gate_ref, kv_ref, wmo_ref, who_ref, wout_ref, gx_ref, wq_ref,
                wxo_ref, o_ref, *, x_scale):
    d_model = x_ref.shape[1]
    om_t = om_ref[0].reshape(MLA_HEADS * MLA_V, x_ref.shape[0])
    y_mla = _dot_tn(om_t, wmo_ref[...])
    y_hg = _dot(oh_ref[...], who_ref[...])
    merged = (gate_ref[:, :d_model] * ys_ref[...] + gate_ref[:, d_model:2 * d_model] * y_mla
              + gate_ref[:, 2 * d_model:] * y_hg)
    x = x_ref[...] + _dot(merged.astype(BF16), wout_ref[...])

    q = (_dot(_rms(x, gx_ref[...]).astype(BF16), wq_ref[...]) * x_scale).astype(BF16)
    outs = []
    for hh in range(X_HEADS):
        sl = slice(hh * X_HEAD_DIM, (hh + 1) * X_HEAD_DIM)
        s = _dot_nt(q[:, sl], kv_ref[:, sl])
        pr = jnp.exp2(s - jnp.max(s, axis=-1, keepdims=True))
        den = jnp.sum(pr, axis=-1, keepdims=True)
        vh = kv_ref[:, X_WIDTH + hh * X_HEAD_DIM:X_WIDTH + (hh + 1) * X_HEAD_DIM]
        outs.append((_dot(pr.astype(BF16), vh) / den).astype(BF16))
    o_ref[...] = x + _dot(jnp.concatenate(outs, axis=1), wxo_ref[...])


def _mid(x2, ys, om, oh, gates, kv, wmo, who, wout, gx, wq, wxo, seq, n_mem):
    n_tok, d_model = x2.shape
    t = MID_TILE
    nt = seq // t
    tok = lambda w_: pl.BlockSpec((t, w_), lambda i: (i, 0))
    return pl.pallas_call(
        functools.partial(_mid_kernel, x_scale=LOG2E / math.sqrt(X_HEAD_DIM)),
        grid=(n_tok // t,),
        in_specs=[tok(d_model), tok(d_model),
                  pl.BlockSpec((1, MLA_HEADS, MLA_V, t), lambda i: (i // nt, 0, 0, i % nt)),
                  tok(HG_WIDTH), tok(N_BRANCH * d_model),
                  pl.BlockSpec((n_mem, 2 * X_WIDTH), lambda i: (i // nt, 0)),
                  _const_spec(wmo.shape), _const_spec(who.shape), _const_spec(wout.shape), _const_spec(gx.shape),
                  _const_spec(wq.shape), _const_spec(wxo.shape)],
        out_specs=tok(d_model),
        out_shape=jax.ShapeDtypeStruct((n_tok, d_model), F32),
        compiler_params=_params(("parallel",)),
        name="mid",
    )(x2, ys, om, oh, gates, kv, wmo, who, wout, gx, wq, wxo)


def _ffn_kernel(x_ref, g_ref, wgu_ref, wd_ref, gf_ref, o_ref, *, final_norm):
    d_ff = wd_ref.shape[0]
    x = x_ref[...]
    h = _rms(x, g_ref[...]).astype(BF16)
    y = x
    for c0 in range(0, d_ff, FFN_CHUNK):
        gt = _dot(h, wgu_ref[:, c0:c0 + FFN_CHUNK])
        up = _dot(h, wgu_ref[:, d_ff + c0:d_ff + c0 + FFN_CHUNK])
        act = (gt * _sigmoid(gt) * up).astype(BF16)
        y = y + _dot(act, wd_ref[c0:c0 + FFN_CHUNK, :])
    if final_norm:
        y = _rms(y, gf_ref[...])
    o_ref[...] = y


def _ffn(x2, g, wgu, wd, gf, final_norm):
    n_tok, d_model = x2.shape
    t = FFN_TILE
    tok = pl.BlockSpec((t, d_model), lambda i: (i, 0))
    return pl.pallas_call(
        functools.partial(_ffn_kernel, final_norm=final_norm),
        grid=(n_tok // t,),
        in_specs=[tok, _const_spec(g.shape), _const_spec(wgu.shape), _const_spec(wd.shape), _const_spec(gf.shape)],
        out_specs=tok,
        out_shape=jax.ShapeDtypeStruct((n_tok, d_model), F32),
        compiler_params=_params(("parallel",)),
        name="ffn",
    )(x2, g, wgu, wd, gf)


def _in_proj_weights(w_in, d_model):
    o = 0
    parts = {}
    for name, width in (("ssm", SSM_WIDTH), ("qlat", MLA_Q_RANK), ("kvlat", MLA_KV_RANK), ("kpe", MLA_ROPE),
                        ("hq", HG_WIDTH), ("hf", HG_WIDTH), ("hi", HG_WIDTH), ("hg", HG_WIDTH),
                        ("gate", N_BRANCH * d_model)):
        parts[name] = w_in[:, o:o + width]
        o += width
    half = MLA_ROPE // 2
    kpe = parts["kpe"]
    kpe_rot = jnp.concatenate([-kpe[:, half:], kpe[:, :half]], axis=1)
    pad = lambda a: jnp.pad(a, ((0, 0), (MLA_NOPE, HEAD_PAD - MLA_NOPE - MLA_ROPE)))
    cols = [parts["ssm"], parts["qlat"], parts["kvlat"], pad(kpe), pad(kpe_rot), parts["hq"], parts["hf"],
            parts["hi"], parts["hg"], parts["gate"]]
    return jnp.concatenate(cols, axis=1).astype(BF16)


def _mla_weights(w_uq, w_ukv, w_o):
    half = MLA_ROPE // 2
    dq = MLA_NOPE + MLA_ROPE
    rank = w_uq.shape[0]
    wq = w_uq.reshape(rank, MLA_HEADS, dq)
    pe = wq[:, :, MLA_NOPE:]
    pe_rot = jnp.concatenate([-pe[:, :, half:], pe[:, :, :half]], axis=2)
    zpad = jnp.zeros((rank, MLA_HEADS, HEAD_PAD - dq), F32)
    wuq = jnp.concatenate([wq, zpad], axis=2).reshape(rank, MLA_HEADS * HEAD_PAD)
    wuqr = jnp.concatenate([jnp.zeros_like(wq[:, :, :MLA_NOPE]), pe_rot, zpad], axis=2).reshape(rank, -1)
    kvr = w_ukv.shape[0]
    wkv = w_ukv.reshape(kvr, MLA_HEADS, MLA_NOPE + MLA_V)
    wuk = jnp.pad(wkv[:, :, :MLA_NOPE], ((0, 0), (0, 0), (0, HEAD_PAD - MLA_NOPE))).reshape(kvr, -1)
    wuv = jnp.pad(wkv[:, :, MLA_NOPE:], ((0, 0), (0, 0), (0, HEAD_PAD - MLA_V))).reshape(kvr, -1)
    return wuq.astype(BF16), wuqr.astype(BF16), wuk.astype(BF16), wuv.astype(BF16), w_o.astype(BF16)


def _s5_weights(lam_re, lam_im, b_re, b_im, c_re, c_im, log_step):
    step = jnp.exp(log_step)[:, None]
    mag = jnp.exp(lam_re * step)
    lbr = mag * jnp.cos(lam_im * step)
    lbi = mag * jnp.sin(lam_im * step)
    den = lam_re * lam_re + lam_im * lam_im
    cr = ((lbr - 1.0) * lam_re + lbi * lam_im) / den
    ci = (lbi * lam_re - (lbr - 1.0) * lam_im) / den
    bbr = cr[..., None] * b_re - ci[..., None] * b_im
    bbi = cr[..., None] * b_im + ci[..., None] * b_re
    gq = SSM_GROUPS // SSM_QUARTERS
    eye = jnp.eye(gq, dtype=F32)

    def blockdiag_in(m):
        m = m.reshape(SSM_QUARTERS, gq, SSM_STATE, SSM_GROUP_CH)
        return jnp.einsum("ab,qaph->qahbp", eye, m).reshape(SSM_QUARTERS, gq * SSM_GROUP_CH, gq * SSM_STATE)

    def blockdiag_out(m):
        m = m.reshape(SSM_QUARTERS, gq, SSM_GROUP_CH, SSM_STATE)
        return jnp.einsum("ab,qahp->qapbh", eye, m).reshape(SSM_QUARTERS, gq * SSM_STATE, gq * SSM_GROUP_CH)

    wb = jnp.concatenate([blockdiag_in(bbr), blockdiag_in(bbi)], axis=2).astype(BF16)
    wc = jnp.concatenate([blockdiag_out(c_re), -blockdiag_out(c_im)], axis=2).astype(BF16)
    lr = lbr.reshape(1, SSM_CH)
    li = lbi.reshape(1, SSM_CH)
    a = jnp.broadcast_to(lr, (8, SSM_CH))
    bsw = jnp.concatenate([jnp.broadcast_to(-li, (4, SSM_CH)), jnp.broadcast_to(li, (4, SSM_CH))], axis=0)
    return wb, a, bsw, wc


def kernel(x, mem, positions, norm_mix, w_in, ssm_lam_re, ssm_lam_im, ssm_b_re, ssm_b_im, ssm_c_re, ssm_c_im, ssm_d, ssm_log_step, ssm_w_glu, mla_q_norm, mla_kv_norm, mla_w_uq, mla_w_ukv, mla_w_o, hg_lb, hg_g_norm, hg_w_o, w_out, norm_cross, norm_mem, x_w_q, x_w_kv, x_w_o, norm_ffn, ffn_w_gate_up, ffn_w_down, norm_final):
    bsz, seq, d_model = x.shape
    depth = w_in.shape[0]
    n_mem = mem.shape[1]
    assert bsz == 4, "the s5 scan packs 4 batches x (re, im) on the 8 sublanes"
    n_tok = bsz * seq

    half = MLA_ROPE // 2
    inv_freq = ROPE_THETA ** (-jnp.arange(half, dtype=F32) / half)
    invf = jnp.zeros((1, HEAD_PAD), F32).at[0, MLA_NOPE:MLA_NOPE + half].set(inv_freq)
    invf = invf.at[0, MLA_NOPE + half:MLA_NOPE + MLA_ROPE].set(inv_freq)
    lb_p = jax.nn.softmax(hg_lb.astype(F32), axis=0)
    lower_bounds = jnp.cumsum(lb_p, axis=0) - lb_p[0:1]

    x2 = x.reshape(n_tok, d_model)
    mem2 = mem.reshape(bsz * n_mem, d_model)
    pos2 = positions.reshape(n_tok, 1)
    row = lambda a: a.reshape(1, -1)

    for l in range(depth):
        w_l = _in_proj_weights(w_in[l], d_model)
        wuq, wuqr, wuk, wuv, wmo = _mla_weights(mla_w_uq[l], mla_w_ukv[l], mla_w_o[l])
        u, q, k, v, hq, hf, hi, hg, gates = _in_proj(
            x2, pos2, row(norm_mix[l]), w_l, invf, row(mla_q_norm[l]), wuq, wuqr, row(mla_kv_norm[l]), wuk, wuv,
            row(lower_bounds[l]), bsz, seq)

        wb, a, bsw, wc = _s5_weights(ssm_lam_re[l], ssm_lam_im[l], ssm_b_re[l], ssm_b_im[l], ssm_c_re[l],
                                     ssm_c_im[l], ssm_log_step[l])
        u_tb = u.reshape(bsz, seq, SSM_WIDTH).transpose(1, 0, 2).reshape(n_tok, SSM_WIDTH)
        ys_tb = _s5(u_tb, wb, a, bsw, wc, row(ssm_d[l]), ssm_w_glu[l].astype(BF16), d_model)
        ys = ys_tb.reshape(seq, bsz, d_model).transpose(1, 0, 2).reshape(n_tok, d_model)

        om = _mla(q, k, v)
        oh = _hgrn(hq, hf, hi, hg, row(hg_g_norm[l]), bsz, seq)
        kv = _memkv(mem2, row(norm_mem[l]), x_w_kv[l].astype(BF16), n_mem)
        x2 = _mid(x2, ys, om, oh, gates, kv, wmo, hg_w_o[l].astype(BF16), w_out[l].astype(BF16),
                  row(norm_cross[l]), x_w_q[l].astype(BF16), x_w_o[l].astype(BF16), seq, n_mem)
        x2 = _ffn(x2, row(norm_ffn[l]), ffn_w_gate_up[l].astype(BF16), ffn_w_down[l].astype(BF16),
                  row(norm_final), final_norm=(l == depth - 1))
    return x2.reshape(bsz, seq, d_model)
```

```python
import functools
import math

import jax
import jax.numpy as jnp
from jax import lax
from jax.experimental import pallas as pl
from jax.experimental.pallas import tpu as pltpu

F32 = jnp.float32
BF16 = jnp.bfloat16

RMS_EPS = 1e-6
LOG2E = 1.4426950408889634

SSM_GROUPS = 32
SSM_GROUP_CH = 16
SSM_STATE = 64
SSM_WIDTH = SSM_GROUPS * SSM_GROUP_CH
SSM_CH = SSM_GROUPS * SSM_STATE
SSM_QUARTERS = 4

MLA_HEADS = 8
MLA_Q_RANK = 512
MLA_KV_RANK = 256
MLA_NOPE = 64
MLA_ROPE = 32
MLA_V = 64
ROPE_THETA = 10000.0
HEAD_PAD = 128

HG_HEADS = 4
HG_D = 128
HG_WIDTH = HG_HEADS * HG_D
HG_CHUNK = 64
HG_SUB = 8
HG_STEP_CHUNKS = 2

X_HEADS = 4
X_HEAD_DIM = 128
X_WIDTH = X_HEADS * X_HEAD_DIM

N_BRANCH = 3

VMEM_LIMIT = 56 * 1024 * 1024

IN_TILE = 256
S5_TIME = 128
MLA_BLOCK = 512
MID_TILE = 256
FFN_TILE = 512
FFN_CHUNK = 1408


def _dot(a, b):
    return jnp.dot(a, b, preferred_element_type=F32)


def _dot_nt(a, b):
    return lax.dot_general(a, b, (((1,), (1,)), ((), ())), preferred_element_type=F32)


def _dot_tn(a, b):
    return lax.dot_general(a, b, (((0,), (0,)), ((), ())), preferred_element_type=F32)


def _rms(x, g):
    return x * lax.rsqrt(jnp.mean(x * x, axis=-1, keepdims=True) + RMS_EPS) * g


def _sigmoid(x):
    return 1.0 / (1.0 + jnp.exp(-x))


def _const_spec(shape):
    nd = len(shape)
    return pl.BlockSpec(shape, lambda *_: (0,) * nd, pipeline_mode=pl.Buffered(1))


def _params(sem):
    return pltpu.CompilerParams(dimension_semantics=sem, vmem_limit_bytes=VMEM_LIMIT)


C_SSM = 0
C_QLAT = C_SSM + SSM_WIDTH
C_KVLAT = C_QLAT + MLA_Q_RANK
C_KPE = C_KVLAT + MLA_KV_RANK
C_KPE_ROT = C_KPE + HEAD_PAD
C_HQ = C_KPE_ROT + HEAD_PAD
C_HF = C_HQ + HG_WIDTH
C_HI = C_HF + HG_WIDTH
C_HG = C_HI + HG_WIDTH
C_GATE = C_HG + HG_WIDTH


def _in_proj_kernel(x_ref, pos_ref, g_ref, w_ref, invf_ref, qn_ref, wuq_ref, wuqr_ref, kvn_ref,
                    wuk_ref, wuv_ref, lb_ref,
                    u_ref, q_ref, k_ref, v_ref, hq_ref, hf_ref, hi_ref, hg_ref, gate_ref, *, q_scale):
    d_model = x_ref.shape[1]
    h = _rms(x_ref[...], g_ref[...]).astype(BF16)

    def seg(a, b):
        return _dot(h, w_ref[:, a:b])

    u_ref[...] = seg(C_SSM, C_QLAT)

    ang = pos_ref[...].astype(F32) * invf_ref[...]
    cos = jnp.cos(ang)
    sin = jnp.sin(ang)

    qn = _rms(seg(C_QLAT, C_KVLAT), qn_ref[...]).astype(BF16)
    qa = _dot(qn, wuq_ref[...])
    qr = _dot(qn, wuqr_ref[...])
    for hh in range(MLA_HEADS):
        sl = slice(hh * HEAD_PAD, (hh + 1) * HEAD_PAD)
        q_ref[0, hh] = ((qa[:, sl] * cos + qr[:, sl] * sin) * q_scale).astype(BF16)

    kvn = _rms(seg(C_KVLAT, C_KPE), kvn_ref[...]).astype(BF16)
    kn = _dot(kvn, wuk_ref[...])
    vv = _dot(kvn, wuv_ref[...])
    kpe = seg(C_KPE, C_KPE_ROT) * cos + seg(C_KPE_ROT, C_HQ) * sin
    ones_lane = (lax.broadcasted_iota(jnp.int32, (1, HEAD_PAD), 1) == MLA_V).astype(F32)
    for hh in range(MLA_HEADS):
        sl = slice(hh * HEAD_PAD, (hh + 1) * HEAD_PAD)
        k_ref[0, hh] = (kn[:, sl] + kpe).astype(BF16)
        v_ref[0, hh] = (vv[:, sl] + ones_lane).T.astype(BF16)

    hq = seg(C_HQ, C_HF)
    hq_ref[...] = hq * _sigmoid(hq)
    lb = lb_ref[...]
    hf_ref[...] = lb + (1.0 - lb) * _sigmoid(seg(C_HF, C_HI))
    hi_ref[...] = seg(C_HI, C_HG).astype(BF16)
    hg = seg(C_HG, C_GATE)
    hg_ref[...] = hg * _sigmoid(hg)
    gate_ref[...] = _sigmoid(seg(C_GATE, C_GATE + N_BRANCH * d_model))


def _in_proj(x2, pos2, g, w, invf, qn, wuq, wuqr, kvn, wuk, wuv, lb, bsz, seq):
    n_tok, d_model = x2.shape
    t = IN_TILE
    nt = seq // t
    tok = lambda w_: pl.BlockSpec((t, w_), lambda i: (i, 0))
    head = pl.BlockSpec((1, MLA_HEADS, t, HEAD_PAD), lambda i: (i // nt, 0, i % nt, 0))
    head_shape = jax.ShapeDtypeStruct((bsz, MLA_HEADS, seq, HEAD_PAD), BF16)
    head_t = pl.BlockSpec((1, MLA_HEADS, HEAD_PAD, t), lambda i: (i // nt, 0, 0, i % nt))
    head_t_shape = jax.ShapeDtypeStruct((bsz, MLA_HEADS, HEAD_PAD, seq), BF16)
    q_scale = LOG2E / math.sqrt(MLA_NOPE + MLA_ROPE)
    return pl.pallas_call(
        functools.partial(_in_proj_kernel, q_scale=q_scale),
        grid=(n_tok // t,),
        in_specs=[tok(d_model), tok(1), _const_spec(g.shape), _const_spec(w.shape), _const_spec(invf.shape),
                  _const_spec(qn.shape), _const_spec(wuq.shape), _const_spec(wuqr.shape), _const_spec(kvn.shape),
                  _const_spec(wuk.shape), _const_spec(wuv.shape), _const_spec(lb.shape)],
        out_specs=[tok(SSM_WIDTH), head, head, head_t, tok(HG_WIDTH), tok(HG_WIDTH), tok(HG_WIDTH), tok(HG_WIDTH),
                   tok(N_BRANCH * d_model)],
        out_shape=[jax.ShapeDtypeStruct((n_tok, SSM_WIDTH), F32), head_shape, head_shape, head_t_shape,
                   jax.ShapeDtypeStruct((n_tok, HG_WIDTH), F32), jax.ShapeDtypeStruct((n_tok, HG_WIDTH), F32),
                   jax.ShapeDtypeStruct((n_tok, HG_WIDTH), BF16), jax.ShapeDtypeStruct((n_tok, HG_WIDTH), F32),
                   jax.ShapeDtypeStruct((n_tok, N_BRANCH * d_model), F32)],
        compiler_params=_params(("parallel",)),
        name="in_proj",
    )(x2, pos2, g, w, invf, qn, wuq, wuqr, kvn, wuk, wuv, lb)


def _s5_kernel(u_ref, wb_ref, a_ref, bsw_ref, wc_ref, d_ref, wglu_ref, y_ref, se_ref, so_ref, st_ref, utb_ref,
               ytb_ref):
    bsz, tc, _ = u_ref.shape
    rows = bsz * tc
    d_model = y_ref.shape[2]
    qw = SSM_CH // SSM_QUARTERS
    uw = SSM_WIDTH // SSM_QUARTERS
    half = 4

    @pl.when(pl.program_id(0) == 0)
    def _():
        st_ref[...] = jnp.zeros_like(st_ref)

    for q in range(SSM_QUARTERS):
        for b in range(bsz):
            utb_ref[q, pl.ds(b, tc, stride=bsz), :] = u_ref[b, :, q * uw:(q + 1) * uw]

    low = (lax.broadcasted_iota(jnp.int32, (rows, qw), 0) % 8) < half
    for q in range(SSM_QUARTERS):
        o = _dot(utb_ref[q].astype(BF16), wb_ref[q])
        re = o[:, :qw]
        im = o[:, qw:]
        se_ref[:, q * qw:(q + 1) * qw] = jnp.where(low, re, pltpu.roll(im, half, 0))
        so_ref[:, q * qw:(q + 1) * qw] = jnp.where(low, pltpu.roll(re, rows - half, 0), im)

    n_blk = 2
    cw = SSM_CH // n_blk
    for cb in range(n_blk):
        cs = slice(cb * cw, (cb + 1) * cw)
        a = a_ref[:, cs]
        bs = bsw_ref[:, cs]

        def body(v, s, cs=cs, a=a, bs=bs):
            r = pl.multiple_of(v * 8, 8)
            s = a * s + bs * pltpu.roll(s, half, 0) + se_ref[pl.ds(r, 8), cs]
            se_ref[pl.ds(r, 8), cs] = s
            s = a * s + bs * pltpu.roll(s, half, 0) + so_ref[pl.ds(r, 8), cs]
            so_ref[pl.ds(r, 8), cs] = s
            return s

        st_ref[:, cs] = lax.fori_loop(0, rows // 8, body, st_ref[:, cs], unroll=4)

    yw = uw
    low_y = low[:, :yw]
    pieces = []
    for q in range(SSM_QUARTERS):
        oe = _dot(se_ref[:, q * qw:(q + 1) * qw].astype(BF16), wc_ref[q])
        oo = _dot(so_ref[:, q * qw:(q + 1) * qw].astype(BF16), wc_ref[q])
        ye = oe[:, :yw] + pltpu.roll(oe[:, yw:], rows - half, 0)
        yo = pltpu.roll(oo[:, :yw], half, 0) + oo[:, yw:]
        pieces.append(jnp.where(low_y, ye, yo) + d_ref[:, q * uw:(q + 1) * uw] * utb_ref[q])
    y = jax.nn.gelu(jnp.concatenate(pieces, axis=1)).astype(BF16)
    z = _dot(y, wglu_ref[...])
    res = z[:, :d_model] * _sigmoid(z[:, d_model:])
    n_slab = d_model // uw
    for j in range(n_slab):
        ytb_ref[j] = res[:, j * uw:(j + 1) * uw]
    for j in range(n_slab):
        for b in range(bsz):
            y_ref[b, :, j * uw:(j + 1) * uw] = ytb_ref[j, pl.ds(b, tc, stride=bsz), :]


def _s5(u3, wb, a, bsw, wc, d, wglu, d_model):
    bsz, seq, _ = u3.shape
    tc = S5_TIME
    rows = tc * bsz
    lanes = SSM_WIDTH // SSM_QUARTERS
    return pl.pallas_call(
        _s5_kernel,
        grid=(seq // tc,),
        in_specs=[pl.BlockSpec((bsz, tc, SSM_WIDTH), lambda i: (0, i, 0)), _const_spec(wb.shape),
                  _const_spec(a.shape), _const_spec(bsw.shape), _const_spec(wc.shape), _const_spec(d.shape),
                  _const_spec(wglu.shape)],
        out_specs=pl.BlockSpec((bsz, tc, d_model), lambda i: (0, i, 0)),
        out_shape=jax.ShapeDtypeStruct((bsz, seq, d_model), F32),
        scratch_shapes=[pltpu.VMEM((rows, SSM_CH), F32), pltpu.VMEM((rows, SSM_CH), F32),
                        pltpu.VMEM((8, SSM_CH), F32), pltpu.VMEM((SSM_QUARTERS, rows, lanes), F32),
                        pltpu.VMEM((d_model // lanes, rows, lanes), F32)],
        compiler_params=_params(("arbitrary",)),
        name="s5",
    )(u3, wb, a, bsw, wc, d, wglu)


def _mla_kernel(qi_ref, kj_ref, q_ref, k_ref, vt_ref, o_ref, m_ref, acc_ref):
    p = pl.program_id(1)
    qi = qi_ref[p]
    kj = kj_ref[p]
    blk = q_ref.shape[2]

    @pl.when(kj == 0)
    def _():
        m_ref[...] = jnp.full_like(m_ref, -1e30)
        acc_ref[...] = jnp.zeros_like(acc_ref)

    def step(masked):
        s_next = _dot_nt(k_ref[0, 0], q_ref[0, 0])
        for hh in range(MLA_HEADS):
            s = s_next
            if hh + 1 < MLA_HEADS:
                s_next = _dot_nt(k_ref[0, hh + 1], q_ref[0, hh + 1])
            if masked:
                key = lax.broadcasted_iota(jnp.int32, (blk, blk), 0)
                qry = lax.broadcasted_iota(jnp.int32, (blk, blk), 1)
                s = jnp.where(key <= qry, s, -1e30)
            m_prev = m_ref[hh]
            m_new = jnp.maximum(m_prev, jnp.max(s, axis=0, keepdims=True))
            alpha = jnp.exp2(m_prev - m_new)
            pr = jnp.exp2(s - m_new).astype(BF16)
            acc_ref[hh] = alpha * acc_ref[hh] + _dot(vt_ref[0, hh], pr)
            m_ref[hh] = m_new

    @pl.when(kj < qi)
    def _():
        step(False)

    @pl.when(kj == qi)
    def _():
        step(True)
        for hh in range(MLA_HEADS):
            acc = acc_ref[hh]
            o_ref[0, hh] = (acc[:MLA_V, :] / acc[MLA_V:MLA_V + 1, :]).astype(BF16)


def _mla(q, k, vt):
    bsz, nh, seq, hd = q.shape
    blk = MLA_BLOCK
    nb = seq // blk
    pairs = [(i, j) for i in range(nb) for j in range(i + 1)]
    qi = jnp.asarray([p[0] for p in pairs], jnp.int32)
    kj = jnp.asarray([p[1] for p in pairs], jnp.int32)
    qspec = pl.BlockSpec((1, nh, blk, hd), lambda b, p, qi, kj: (b, 0, qi[p], 0))
    kspec = pl.BlockSpec((1, nh, blk, hd), lambda b, p, qi, kj: (b, 0, kj[p], 0))
    vspec = pl.BlockSpec((1, nh, hd, blk), lambda b, p, qi, kj: (b, 0, 0, kj[p]))
    ospec = pl.BlockSpec((1, nh, MLA_V, blk), lambda b, p, qi, kj: (b, 0, 0, qi[p]))
    return pl.pallas_call(
        _mla_kernel,
        grid_spec=pltpu.PrefetchScalarGridSpec(
            num_scalar_prefetch=2,
            grid=(bsz, len(pairs)),
            in_specs=[qspec, kspec, vspec],
            out_specs=ospec,
            scratch_shapes=[pltpu.VMEM((nh, 1, blk), F32), pltpu.VMEM((nh, hd, blk), F32)],
        ),
        out_shape=jax.ShapeDtypeStruct((bsz, nh, MLA_V, seq), BF16),
        compiler_params=_params(("parallel", "arbitrary")),
        name="mla",
    )(qi, kj, q, k, vt)


def _hgrn_head(q, f, v, g, gn, st, ks_ref):
    c = HG_CHUNK
    sub = HG_SUB
    n_sub = c // sub
    k = 1.0 - f
    lf = jnp.log(f) * LOG2E

    row = lax.broadcasted_iota(jnp.int32, (c, c), 0)
    col = lax.broadcasted_iota(jnp.int32, (c, c), 1)
    tri = (col <= row).astype(BF16)
    l1 = lf.astype(BF16)
    r1 = lf - l1.astype(F32)
    l2 = r1.astype(BF16)
    l3 = (r1 - l2.astype(F32)).astype(BF16)
    b = _dot(tri, l1) + _dot(tri, l2) + _dot(tri, l3)

    o = _dot_nt((q * jnp.exp2(b)).astype(BF16), st.astype(BF16))

    lane = lax.broadcasted_iota(jnp.int32, (sub, HG_D), 1)
    trow = lax.broadcasted_iota(jnp.int32, (sub, HG_D), 0)
    bk = b - jnp.log(k) * LOG2E
    rows = []
    for i in range(n_sub):
        rs = slice(i * sub, (i + 1) * sub)
        bi = b[rs]
        qi = q[rs]
        tile = jnp.zeros((sub, HG_D), F32)
        for ss in range(sub):
            s = i * sub + ss
            kdec_s = jnp.exp2(jnp.minimum(bi - bk[s:s + 1, :], 0.0))
            a_s = jnp.sum(qi * kdec_s, axis=-1, keepdims=True)
            tile = jnp.where(lane == s, a_s, tile)
        a_row = jnp.where(trow + i * sub >= lane, tile, 0.0)[:, :c]
        if i > 0:
            r = b[i * sub - 1:i * sub, :]
            ks_ref[i - 1, :i * sub, :] = k[:i * sub] * jnp.exp2(r - b[:i * sub])
            qs = (qi * jnp.exp2(bi - r)).astype(BF16)
            a_row = a_row + _dot_nt(qs, ks_ref[i - 1].astype(BF16))
        rows.append(a_row)
    a = jnp.concatenate(rows, axis=0).astype(BF16)
    o = o + _dot(a, v)

    bl = b[c - 1:c, :]
    kdec = (k * jnp.exp2(bl - b)).astype(BF16)
    st_new = st * jnp.exp2(bl) + _dot_tn(v, kdec)

    on = o * lax.rsqrt(jnp.mean(o * o, axis=-1, keepdims=True) + RMS_EPS) * gn
    return (on * g).astype(BF16), st_new


def _hgrn_kernel(q_ref, f_ref, v_ref, g_ref, gn_ref, o_ref, st_ref, ks_ref):
    @pl.when(pl.program_id(1) == 0)
    def _():
        st_ref[...] = jnp.zeros_like(st_ref)
        ks_ref[...] = jnp.zeros_like(ks_ref)

    gn = gn_ref[...]
    for hh in range(HG_HEADS):
        sl = slice(hh * HG_D, (hh + 1) * HG_D)
        st = st_ref[hh]
        for ci in range(HG_STEP_CHUNKS):
            rs = slice(ci * HG_CHUNK, (ci + 1) * HG_CHUNK)
            o, st = _hgrn_head(q_ref[rs, sl], f_ref[rs, sl], v_ref[rs, sl], g_ref[rs, sl], gn, st,
                               ks_ref.at[hh * HG_STEP_CHUNKS + ci])
            o_ref[rs, sl] = o
        st_ref[hh] = st


def _hgrn(hq, hf, hi, hg, gn, bsz, seq):
    n_tok = hq.shape[0]
    c = HG_CHUNK
    rows = c * HG_STEP_CHUNKS
    nc = seq // rows
    spec = pl.BlockSpec((rows, HG_WIDTH), lambda b, i: (b * nc + i, 0))
    return pl.pallas_call(
        _hgrn_kernel,
        grid=(bsz, nc),
        in_specs=[spec, spec, spec, spec, pl.BlockSpec((1, HG_D), lambda b, i: (0, 0))],
        out_specs=spec,
        out_shape=jax.ShapeDtypeStruct((n_tok, HG_WIDTH), BF16),
        scratch_shapes=[pltpu.VMEM((HG_HEADS, HG_D, HG_D), F32),
                        pltpu.VMEM((HG_HEADS * HG_STEP_CHUNKS, c // HG_SUB - 1, c, HG_D), F32)],
        compiler_params=_params(("parallel", "arbitrary")),
        name="hgrn",
    )(hq, hf, hi, hg, gn)


def _memkv_kernel(mem_ref, g_ref, w_ref, kv_ref):
    kv_ref[...] = _dot(_rms(mem_ref[...], g_ref[...]).astype(BF16), w_ref[...]).astype(BF16)


def _memkv(mem2, g, w, n_mem):
    n_rows, d_model = mem2.shape
    return pl.pallas_call(
        _memkv_kernel,
        grid=(n_rows // n_mem,),
        in_specs=[pl.BlockSpec((n_mem, d_model), lambda i: (i, 0)), _const_spec(g.shape), _const_spec(w.shape)],
        out_specs=pl.BlockSpec((n_mem, 2 * X_WIDTH), lambda i: (i, 0)),
        out_shape=jax.ShapeDtypeStruct((n_rows, 2 * X_WIDTH), BF16),
        compiler_params=_params(("parallel",)),
        name="memkv",
    )(mem2, g, w)


def _mid_kernel(x_ref, ys_ref, om_ref, oh_ref, gate_ref, kv_ref, wmo_ref, who_ref, wout_ref, gx_ref, wq_ref,
                wxo_ref, o_ref, *, x_scale):
    d_model = x_ref.shape[1]
    om_t = om_ref[0].reshape(MLA_HEADS * MLA_V, x_ref.shape[0])
    y_mla = _dot_tn(om_t, wmo_ref[...])
    y_hg = _dot(oh_ref[...], who_ref[...])
    merged = (gate_ref[:, :d_model] * ys_ref[...] + gate_ref[:, d_model:2 * d_model] * y_mla
              + gate_ref[:, 2 * d_model:] * y_hg)
    x = x_ref[...] + _dot(merged.astype(BF16), wout_ref[...])

    q = (_dot(_rms(x, gx_ref[...]).astype(BF16), wq_ref[...]) * x_scale).astype(BF16)
    outs = []
    for hh in range(X_HEADS):
        sl = slice(hh * X_HEAD_DIM, (hh + 1) * X_HEAD_DIM)
        s = _dot_nt(q[:, sl], kv_ref[:, sl])
        pr = jnp.exp2(s - jnp.max(s, axis=-1, keepdims=True))
        den = jnp.sum(pr, axis=-1, keepdims=True)
        vh = kv_ref[:, X_WIDTH + hh * X_HEAD_DIM:X_WIDTH + (hh + 1) * X_HEAD_DIM]
        outs.append((_dot(pr.astype(BF16), vh) / den).astype(BF16))
    o_ref[...] = x + _dot(jnp.concatenate(outs, axis=1), wxo_ref[...])


def _mid(x2, ys, om, oh, gates, kv, wmo, who, wout, gx, wq, wxo, seq, n_mem):
    n_tok, d_model = x2.shape
    t = MID_TILE
    nt = seq // t
    tok = lambda w_: pl.BlockSpec((t, w_), lambda i: (i, 0))
    return pl.pallas_call(
        functools.partial(_mid_kernel, x_scale=LOG2E / math.sqrt(X_HEAD_DIM)),
        grid=(n_tok // t,),
        in_specs=[tok(d_model), tok(d_model),
                  pl.BlockSpec((1, MLA_HEADS, MLA_V, t), lambda i: (i // nt, 0, 0, i % nt)),
                  tok(HG_WIDTH), tok(N_BRANCH * d_model),
                  pl.BlockSpec((n_mem, 2 * X_WIDTH), lambda i: (i // nt, 0)),
                  _const_spec(wmo.shape), _const_spec(who.shape), _const_spec(wout.shape), _const_spec(gx.shape),
                  _const_spec(wq.shape), _const_spec(wxo.shape)],
        out_specs=tok(d_model),
        out_shape=jax.ShapeDtypeStruct((n_tok, d_model), F32),
        compiler_params=_params(("parallel",)),
        name="mid",
    )(x2, ys, om, oh, gates, kv, wmo, who, wout, gx, wq, wxo)


def _ffn_kernel(x_ref, g_ref, wgu_ref, wd_ref, gf_ref, o_ref, *, final_norm):
    d_ff = wd_ref.shape[0]
    x = x_ref[...]
    h = _rms(x, g_ref[...]).astype(BF16)
    y = x
    for c0 in range(0, d_ff, FFN_CHUNK):
        gt = _dot(h, wgu_ref[:, c0:c0 + FFN_CHUNK])
        up = _dot(h, wgu_ref[:, d_ff + c0:d_ff + c0 + FFN_CHUNK])
        act = (gt * _sigmoid(gt) * up).astype(BF16)
        y = y + _dot(act, wd_ref[c0:c0 + FFN_CHUNK, :])
    if final_norm:
        y = _rms(y, gf_ref[...])
    o_ref[...] = y


def _ffn(x2, g, wgu, wd, gf, final_norm):
    n_tok, d_model = x2.shape
    t = FFN_TILE
    tok = pl.BlockSpec((t, d_model), lambda i: (i, 0))
    return pl.pallas_call(
        functools.partial(_ffn_kernel, final_norm=final_norm),
        grid=(n_tok // t,),
        in_specs=[tok, _const_spec(g.shape), _const_spec(wgu.shape), _const_spec(wd.shape), _const_spec(gf.shape)],
        out_specs=tok,
        out_shape=jax.ShapeDtypeStruct((n_tok, d_model), F32),
        compiler_params=_params(("parallel",)),
        name="ffn",
    )(x2, g, wgu, wd, gf)


def _in_proj_weights(w_in, d_model):
    o = 0
    parts = {}
    for name, width in (("ssm", SSM_WIDTH), ("qlat", MLA_Q_RANK), ("kvlat", MLA_KV_RANK), ("kpe", MLA_ROPE),
                        ("hq", HG_WIDTH), ("hf", HG_WIDTH), ("hi", HG_WIDTH), ("hg", HG_WIDTH),
                        ("gate", N_BRANCH * d_model)):
        parts[name] = w_in[:, o:o + width]
        o += width
    half = MLA_ROPE // 2
    kpe = parts["kpe"]
    kpe_rot = jnp.concatenate([-kpe[:, half:], kpe[:, :half]], axis=1)
    pad = lambda a: jnp.pad(a, ((0, 0), (MLA_NOPE, HEAD_PAD - MLA_NOPE - MLA_ROPE)))
    cols = [parts["ssm"], parts["qlat"], parts["kvlat"], pad(kpe), pad(kpe_rot), parts["hq"], parts["hf"],
            parts["hi"], parts["hg"], parts["gate"]]
    return jnp.concatenate(cols, axis=1).astype(BF16)


def _mla_weights(w_uq, w_ukv, w_o):
    half = MLA_ROPE // 2
    dq = MLA_NOPE + MLA_ROPE
    rank = w_uq.shape[0]
    wq = w_uq.reshape(rank, MLA_HEADS, dq)
    pe = wq[:, :, MLA_NOPE:]
    pe_rot = jnp.concatenate([-pe[:, :, half:], pe[:, :, :half]], axis=2)
    zpad = jnp.zeros((rank, MLA_HEADS, HEAD_PAD - dq), F32)
    wuq = jnp.concatenate([wq, zpad], axis=2).reshape(rank, MLA_HEADS * HEAD_PAD)
    wuqr = jnp.concatenate([jnp.zeros_like(wq[:, :, :MLA_NOPE]), pe_rot, zpad], axis=2).reshape(rank, -1)
    kvr = w_ukv.shape[0]
    wkv = w_ukv.reshape(kvr, MLA_HEADS, MLA_NOPE + MLA_V)
    wuk = jnp.pad(wkv[:, :, :MLA_NOPE], ((0, 0), (0, 0), (0, HEAD_PAD - MLA_NOPE))).reshape(kvr, -1)
    wuv = jnp.pad(wkv[:, :, MLA_NOPE:], ((0, 0), (0, 0), (0, HEAD_PAD - MLA_V))).reshape(kvr, -1)
    return wuq.astype(BF16), wuqr.astype(BF16), wuk.astype(BF16), wuv.astype(BF16), w_o.astype(BF16)


def _s5_weights(lam_re, lam_im, b_re, b_im, c_re, c_im, log_step):
    step = jnp.exp(log_step)[:, None]
    mag = jnp.exp(lam_re * step)
    lbr = mag * jnp.cos(lam_im * step)
    lbi = mag * jnp.sin(lam_im * step)
    den = lam_re * lam_re + lam_im * lam_im
    cr = ((lbr - 1.0) * lam_re + lbi * lam_im) / den
    ci = (lbi * lam_re - (lbr - 1.0) * lam_im) / den
    bbr = cr[..., None] * b_re - ci[..., None] * b_im
    bbi = cr[..., None] * b_im + ci[..., None] * b_re
    gq = SSM_GROUPS // SSM_QUARTERS
    eye = jnp.eye(gq, dtype=F32)

    def blockdiag_in(m):
        m = m.reshape(SSM_QUARTERS, gq, SSM_STATE, SSM_GROUP_CH)
        return jnp.einsum("ab,qaph->qahbp", eye, m).reshape(SSM_QUARTERS, gq * SSM_GROUP_CH, gq * SSM_STATE)

    def blockdiag_out(m):
        m = m.reshape(SSM_QUARTERS, gq, SSM_GROUP_CH, SSM_STATE)
        return jnp.einsum("ab,qahp->qapbh", eye, m).reshape(SSM_QUARTERS, gq * SSM_STATE, gq * SSM_GROUP_CH)

    wb = jnp.concatenate([blockdiag_in(bbr), blockdiag_in(bbi)], axis=2).astype(BF16)
    wc = jnp.concatenate([blockdiag_out(c_re), -blockdiag_out(c_im)], axis=2).astype(BF16)
    lr = lbr.reshape(1, SSM_CH)
    li = lbi.reshape(1, SSM_CH)
    a = jnp.broadcast_to(lr, (8, SSM_CH))
    bsw = jnp.concatenate([jnp.broadcast_to(-li, (4, SSM_CH)), jnp.broadcast_to(li, (4, SSM_CH))], axis=0)
    return wb, a, bsw, wc


def kernel(x, mem, positions, norm_mix, w_in, ssm_lam_re, ssm_lam_im, ssm_b_re, ssm_b_im, ssm_c_re, ssm_c_im, ssm_d, ssm_log_step, ssm_w_glu, mla_q_norm, mla_kv_norm, mla_w_uq, mla_w_ukv, mla_w_o, hg_lb, hg_g_norm, hg_w_o, w_out, norm_cross, norm_mem, x_w_q, x_w_kv, x_w_o, norm_ffn, ffn_w_gate_up, ffn_w_down, norm_final):
    bsz, seq, d_model = x.shape
    depth = w_in.shape[0]
    n_mem = mem.shape[1]
    assert bsz == 4, "the s5 scan packs 4 batches x (re, im) on the 8 sublanes"
    n_tok = bsz * seq

    half = MLA_ROPE // 2
    inv_freq = ROPE_THETA ** (-jnp.arange(half, dtype=F32) / half)
    invf = jnp.zeros((1, HEAD_PAD), F32).at[0, MLA_NOPE:MLA_NOPE + half].set(inv_freq)
    invf = invf.at[0, MLA_NOPE + half:MLA_NOPE + MLA_ROPE].set(inv_freq)
    lb_p = jax.nn.softmax(hg_lb.astype(F32), axis=0)
    lower_bounds = jnp.cumsum(lb_p, axis=0) - lb_p[0:1]

    x2 = x.reshape(n_tok, d_model)
    mem2 = mem.reshape(bsz * n_mem, d_model)
    pos2 = positions.reshape(n_tok, 1)
    row = lambda a: a.reshape(1, -1)

    for l in range(depth):
        w_l = _in_proj_weights(w_in[l], d_model)
        wuq, wuqr, wuk, wuv, wmo = _mla_weights(mla_w_uq[l], mla_w_ukv[l], mla_w_o[l])
        u, q, k, v, hq, hf, hi, hg, gates = _in_proj(
            x2, pos2, row(norm_mix[l]), w_l, invf, row(mla_q_norm[l]), wuq, wuqr, row(mla_kv_norm[l]), wuk, wuv,
            row(lower_bounds[l]), bsz, seq)

        wb, a, bsw, wc = _s5_weights(ssm_lam_re[l], ssm_lam_im[l], ssm_b_re[l], ssm_b_im[l], ssm_c_re[l],
                                     ssm_c_im[l], ssm_log_step[l])
        ys = _s5(u.reshape(bsz, seq, SSM_WIDTH), wb, a, bsw, wc, row(ssm_d[l]), ssm_w_glu[l].astype(BF16),
                 d_model).reshape(n_tok, d_model)

        om = _mla(q, k, v)
        oh = _hgrn(hq, hf, hi, hg, row(hg_g_norm[l]), bsz, seq)
        kv = _memkv(mem2, row(norm_mem[l]), x_w_kv[l].astype(BF16), n_mem)
        x2 = _mid(x2, ys, om, oh, gates, kv, wmo, hg_w_o[l].astype(BF16), w_out[l].astype(BF16),
                  row(norm_cross[l]), x_w_q[l].astype(BF16), x_w_o[l].astype(BF16), seq, n_mem)
        x2 = _ffn(x2, row(norm_ffn[l]), ffn_w_gate_up[l].astype(BF16), ffn_w_down[l].astype(BF16),
                  row(norm_final), final_norm=(l == depth - 1))
    return x2.reshape(bsz, seq, d_model)
```

```python
import functools
import math

import jax
import jax.numpy as jnp
from jax import lax
from jax.experimental import pallas as pl
from jax.experimental.pallas import tpu as pltpu

F32 = jnp.float32
BF16 = jnp.bfloat16

RMS_EPS = 1e-6
LOG2E = 1.4426950408889634

SSM_GROUPS = 32
SSM_GROUP_CH = 16
SSM_STATE = 64
SSM_WIDTH = SSM_GROUPS * SSM_GROUP_CH
SSM_CH = SSM_GROUPS * SSM_STATE
SSM_QUARTERS = 4

MLA_HEADS = 8
MLA_Q_RANK = 512
MLA_KV_RANK = 256
MLA_NOPE = 64
MLA_ROPE = 32
MLA_V = 64
ROPE_THETA = 10000.0
HEAD_PAD = 128

HG_HEADS = 4
HG_D = 128
HG_WIDTH = HG_HEADS * HG_D
HG_CHUNK = 64
HG_SUB = 8
HG_STEP_CHUNKS = 4

X_HEADS = 4
X_HEAD_DIM = 128
X_WIDTH = X_HEADS * X_HEAD_DIM

N_BRANCH = 3

VMEM_LIMIT = 56 * 1024 * 1024

IN_TILE = 512
S5_TIME = 128
MLA_BLOCK = 512
MID_TILE = 512
FFN_TILE = 512
FFN_CHUNK = 1408


def _dot(a, b):
    return jnp.dot(a, b, preferred_element_type=F32)


def _dot_nt(a, b):
    return lax.dot_general(a, b, (((1,), (1,)), ((), ())), preferred_element_type=F32)


def _dot_tn(a, b):
    return lax.dot_general(a, b, (((0,), (0,)), ((), ())), preferred_element_type=F32)


def _rms(x, g):
    return x * lax.rsqrt(jnp.mean(x * x, axis=-1, keepdims=True) + RMS_EPS) * g


def _sigmoid(x):
    return 1.0 / (1.0 + jnp.exp(-x))


def _const_spec(shape):
    nd = len(shape)
    return pl.BlockSpec(shape, lambda *_: (0,) * nd, pipeline_mode=pl.Buffered(1))


def _layer_spec(shape, l):
    nd = len(shape)
    return pl.BlockSpec((None,) + tuple(shape[1:]), lambda *_: (l,) + (0,) * (nd - 1),
                        pipeline_mode=pl.Buffered(1))


def _params(sem):
    return pltpu.CompilerParams(dimension_semantics=sem, vmem_limit_bytes=VMEM_LIMIT)


C_SSM = 0
C_QLAT = C_SSM + SSM_WIDTH
C_KVLAT = C_QLAT + MLA_Q_RANK
C_KPE = C_KVLAT + MLA_KV_RANK
C_KPE_ROT = C_KPE + HEAD_PAD
C_HQ = C_KPE_ROT + HEAD_PAD
C_HF = C_HQ + HG_WIDTH
C_HI = C_HF + HG_WIDTH
C_HG = C_HI + HG_WIDTH
C_GATE = C_HG + HG_WIDTH


def _in_proj_kernel(x_ref, pos_ref, g_ref, w_ref, invf_ref, qn_ref, wuq_ref, wuqr_ref, kvn_ref,
                    wuk_ref, wuv_ref, lb_ref,
                    u_ref, q_ref, k_ref, v_ref, hq_ref, hf_ref, hi_ref, hg_ref, gate_ref, *, q_scale):
    d_model = x_ref.shape[1]
    h = _rms(x_ref[...], g_ref[...]).astype(BF16)

    def seg(a, b):
        return _dot(h, w_ref[:, a:b])

    u_ref[...] = seg(C_SSM, C_QLAT)

    ang = pos_ref[...].astype(F32) * invf_ref[...]
    cos = jnp.cos(ang)
    sin = jnp.sin(ang)

    qn = _rms(seg(C_QLAT, C_KVLAT), qn_ref[...]).astype(BF16)
    qa = _dot(qn, wuq_ref[...])
    qr = _dot(qn, wuqr_ref[...])
    for hh in range(MLA_HEADS):
        sl = slice(hh * HEAD_PAD, (hh + 1) * HEAD_PAD)
        q_ref[0, hh] = ((qa[:, sl] * cos + qr[:, sl] * sin) * q_scale).astype(BF16)

    kvn = _rms(seg(C_KVLAT, C_KPE), kvn_ref[...]).astype(BF16)
    kn = _dot(kvn, wuk_ref[...])
    vv = _dot(kvn, wuv_ref[...])
    kpe = seg(C_KPE, C_KPE_ROT) * cos + seg(C_KPE_ROT, C_HQ) * sin
    ones_lane = (lax.broadcasted_iota(jnp.int32, (1, HEAD_PAD), 1) == MLA_V).astype(F32)
    for hh in range(MLA_HEADS):
        sl = slice(hh * HEAD_PAD, (hh + 1) * HEAD_PAD)
        k_ref[0, hh] = (kn[:, sl] + kpe).astype(BF16)
        v_ref[0, hh] = (vv[:, sl] + ones_lane).T.astype(BF16)

    hq = seg(C_HQ, C_HF)
    hq_ref[...] = (hq * _sigmoid(hq)).astype(BF16)
    lb = lb_ref[...]
    hf_ref[...] = lb + (1.0 - lb) * _sigmoid(seg(C_HF, C_HI))
    hi_ref[...] = seg(C_HI, C_HG).astype(BF16)
    hg = seg(C_HG, C_GATE)
    hg_ref[...] = (hg * _sigmoid(hg)).astype(BF16)
    for br in range(N_BRANCH):
        gate_ref[:, br * d_model:(br + 1) * d_model] = _sigmoid(
            seg(C_GATE + br * d_model, C_GATE + (br + 1) * d_model)).astype(BF16)


def _in_proj(x2, pos2, g, w, invf, qn, wuq, wuqr, kvn, wuk, wuv, lb, l, bsz, seq):
    n_tok, d_model = x2.shape
    t = IN_TILE
    nt = seq // t
    tok = lambda w_: pl.BlockSpec((t, w_), lambda i: (i, 0))
    lay = lambda a: _layer_spec(a.shape, l)
    head = pl.BlockSpec((1, MLA_HEADS, t, HEAD_PAD), lambda i: (i // nt, 0, i % nt, 0))
    head_shape = jax.ShapeDtypeStruct((bsz, MLA_HEADS, seq, HEAD_PAD), BF16)
    head_t = pl.BlockSpec((1, MLA_HEADS, HEAD_PAD, t), lambda i: (i // nt, 0, 0, i % nt))
    head_t_shape = jax.ShapeDtypeStruct((bsz, MLA_HEADS, HEAD_PAD, seq), BF16)
    q_scale = LOG2E / math.sqrt(MLA_NOPE + MLA_ROPE)
    return pl.pallas_call(
        functools.partial(_in_proj_kernel, q_scale=q_scale),
        grid=(n_tok // t,),
        in_specs=[tok(d_model), tok(1), lay(g), lay(w), _const_spec(invf.shape), lay(qn), lay(wuq), lay(wuqr),
                  lay(kvn), lay(wuk), lay(wuv), lay(lb)],
        out_specs=[tok(SSM_WIDTH), head, head, head_t, tok(HG_WIDTH), tok(HG_WIDTH), tok(HG_WIDTH), tok(HG_WIDTH),
                   tok(N_BRANCH * d_model)],
        out_shape=[jax.ShapeDtypeStruct((n_tok, SSM_WIDTH), F32), head_shape, head_shape, head_t_shape,
                   jax.ShapeDtypeStruct((n_tok, HG_WIDTH), BF16), jax.ShapeDtypeStruct((n_tok, HG_WIDTH), F32),
                   jax.ShapeDtypeStruct((n_tok, HG_WIDTH), BF16), jax.ShapeDtypeStruct((n_tok, HG_WIDTH), BF16),
                   jax.ShapeDtypeStruct((n_tok, N_BRANCH * d_model), BF16)],
        compiler_params=_params(("parallel",)),
        name="in_proj",
    )(x2, pos2, g, w, invf, qn, wuq, wuqr, kvn, wuk, wuv, lb)


def _s5_kernel(u_ref, wb_ref, a_ref, bsw_ref, wc_ref, d_ref, wglu_ref, y_ref, se_ref, so_ref, st_ref, utb_ref,
               ytb_ref):
    bsz, tc, _ = u_ref.shape
    rows = bsz * tc
    d_model = y_ref.shape[2]
    qw = SSM_CH // SSM_QUARTERS
    uw = SSM_WIDTH // SSM_QUARTERS
    half = 4

    @pl.when(pl.program_id(0) == 0)
    def _():
        st_ref[...] = jnp.zeros_like(st_ref)

    for q in range(SSM_QUARTERS):
        for b in range(bsz):
            utb_ref[q, pl.ds(b, tc, stride=bsz), :] = u_ref[b, :, q * uw:(q + 1) * uw]

    low = (lax.broadcasted_iota(jnp.int32, (rows, qw), 0) % 8) < half
    for q in range(SSM_QUARTERS):
        o = _dot(utb_ref[q].astype(BF16), wb_ref[q])
        re = o[:, :qw]
        im = o[:, qw:]
        se_ref[:, q * qw:(q + 1) * qw] = jnp.where(low, re, pltpu.roll(im, half, 0))
        so_ref[:, q * qw:(q + 1) * qw] = jnp.where(low, pltpu.roll(re, rows - half, 0), im)

    n_blk = 2
    cw = SSM_CH // n_blk
    for cb in range(n_blk):
        cs = slice(cb * cw, (cb + 1) * cw)
        a = a_ref[:, cs]
        bs = bsw_ref[:, cs]

        def body(v, s, cs=cs, a=a, bs=bs):
            r = pl.multiple_of(v * 8, 8)
            s = a * s + bs * pltpu.roll(s, half, 0) + se_ref[pl.ds(r, 8), cs]
            se_ref[pl.ds(r, 8), cs] = s
            s = a * s + bs * pltpu.roll(s, half, 0) + so_ref[pl.ds(r, 8), cs]
            so_ref[pl.ds(r, 8), cs] = s
            return s

        st_ref[:, cs] = lax.fori_loop(0, rows // 8, body, st_ref[:, cs], unroll=4)

    yw = uw
    low_y = low[:, :yw]
    pieces = []
    for q in range(SSM_QUARTERS):
        oe = _dot(se_ref[:, q * qw:(q + 1) * qw].astype(BF16), wc_ref[q])
        oo = _dot(so_ref[:, q * qw:(q + 1) * qw].astype(BF16), wc_ref[q])
        ye = oe[:, :yw] + pltpu.roll(oe[:, yw:], rows - half, 0)
        yo = pltpu.roll(oo[:, :yw], half, 0) + oo[:, yw:]
        pieces.append(jnp.where(low_y, ye, yo) + d_ref[:, q * uw:(q + 1) * uw] * utb_ref[q])
    y = jax.nn.gelu(jnp.concatenate(pieces, axis=1)).astype(BF16)
    z = _dot(y, wglu_ref[...])
    res = z[:, :d_model] * _sigmoid(z[:, d_model:])
    n_slab = d_model // uw
    for j in range(n_slab):
        ytb_ref[j] = res[:, j * uw:(j + 1) * uw]
    for j in range(n_slab):
        for b in range(bsz):
            y_ref[b, :, j * uw:(j + 1) * uw] = ytb_ref[j, pl.ds(b, tc, stride=bsz), :]


def _s5(u3, wb, a, bsw, wc, d, wglu, l, d_model):
    bsz, seq, _ = u3.shape
    tc = S5_TIME
    rows = tc * bsz
    lanes = SSM_WIDTH // SSM_QUARTERS
    return pl.pallas_call(
        _s5_kernel,
        grid=(seq // tc,),
        in_specs=[pl.BlockSpec((bsz, tc, SSM_WIDTH), lambda i: (0, i, 0))]
        + [_layer_spec(w_.shape, l) for w_ in (wb, a, bsw, wc, d, wglu)],
        out_specs=pl.BlockSpec((bsz, tc, d_model), lambda i: (0, i, 0)),
        out_shape=jax.ShapeDtypeStruct((bsz, seq, d_model), F32),
        scratch_shapes=[pltpu.VMEM((rows, SSM_CH), F32), pltpu.VMEM((rows, SSM_CH), F32),
                        pltpu.VMEM((8, SSM_CH), F32), pltpu.VMEM((SSM_QUARTERS, rows, lanes), F32),
                        pltpu.VMEM((d_model // lanes, rows, lanes), F32)],
        compiler_params=_params(("arbitrary",)),
        name="s5",
    )(u3, wb, a, bsw, wc, d, wglu)


def _mla_kernel(qi_ref, kj_ref, q_ref, k_ref, vt_ref, o_ref, m_ref, acc_ref):
    p = pl.program_id(1)
    qi = qi_ref[p]
    kj = kj_ref[p]
    blk = q_ref.shape[2]

    @pl.when(kj == 0)
    def _():
        m_ref[...] = jnp.full_like(m_ref, -1e30)
        acc_ref[...] = jnp.zeros_like(acc_ref)

    def step(masked):
        s_next = _dot_nt(k_ref[0, 0], q_ref[0, 0])
        for hh in range(MLA_HEADS):
            s = s_next
            if hh + 1 < MLA_HEADS:
                s_next = _dot_nt(k_ref[0, hh + 1], q_ref[0, hh + 1])
            if masked:
                key = lax.broadcasted_iota(jnp.int32, (blk, blk), 0)
                qry = lax.broadcasted_iota(jnp.int32, (blk, blk), 1)
                s = jnp.where(key <= qry, s, -1e30)
            m_prev = m_ref[hh]
            m_new = jnp.maximum(m_prev, jnp.max(s, axis=0, keepdims=True))
            alpha = jnp.exp2(m_prev - m_new)
            pr = jnp.exp2(s - m_new).astype(BF16)
            acc_ref[hh] = alpha * acc_ref[hh] + _dot(vt_ref[0, hh], pr)
            m_ref[hh] = m_new

    @pl.when(kj < qi)
    def _():
        step(False)

    @pl.when(kj == qi)
    def _():
        step(True)
        for hh in range(MLA_HEADS):
            acc = acc_ref[hh]
            o_ref[0, hh] = (acc[:MLA_V, :] / acc[MLA_V:MLA_V + 1, :]).astype(BF16)


def _mla(q, k, vt):
    bsz, nh, seq, hd = q.shape
    blk = MLA_BLOCK
    nb = seq // blk
    pairs = [(i, j) for i in range(nb) for j in range(i + 1)]
    qi = jnp.asarray([p[0] for p in pairs], jnp.int32)
    kj = jnp.asarray([p[1] for p in pairs], jnp.int32)
    qspec = pl.BlockSpec((1, nh, blk, hd), lambda b, p, qi, kj: (b, 0, qi[p], 0))
    kspec = pl.BlockSpec((1, nh, blk, hd), lambda b, p, qi, kj: (b, 0, kj[p], 0))
    vspec = pl.BlockSpec((1, nh, hd, blk), lambda b, p, qi, kj: (b, 0, 0, kj[p]))
    ospec = pl.BlockSpec((1, nh, MLA_V, blk), lambda b, p, qi, kj: (b, 0, 0, qi[p]))
    return pl.pallas_call(
        _mla_kernel,
        grid_spec=pltpu.PrefetchScalarGridSpec(
            num_scalar_prefetch=2,
            grid=(bsz, len(pairs)),
            in_specs=[qspec, kspec, vspec],
            out_specs=ospec,
            scratch_shapes=[pltpu.VMEM((nh, 1, blk), F32), pltpu.VMEM((nh, hd, blk), F32)],
        ),
        out_shape=jax.ShapeDtypeStruct((bsz, nh, MLA_V, seq), BF16),
        compiler_params=_params(("parallel", "arbitrary")),
        name="mla",
    )(qi, kj, q, k, vt)


def _hgrn_kernel(q_ref, f_ref, v_ref, g_ref, gn_ref, o_ref, st_ref, ks_ref):
    c = HG_CHUNK
    sub = HG_SUB
    n_sub = c // sub

    @pl.when(pl.program_id(1) == 0)
    def _():
        st_ref[...] = jnp.zeros_like(st_ref)
        ks_ref[...] = jnp.zeros_like(ks_ref)

    chains = [(hh, ci) for hh in range(HG_HEADS) for ci in range(HG_STEP_CHUNKS)]

    def blk(ref, ch):
        hh, ci = ch
        return ref[ci * c:(ci + 1) * c, hh * HG_D:(hh + 1) * HG_D]

    row = lax.broadcasted_iota(jnp.int32, (c, c), 0)
    col = lax.broadcasted_iota(jnp.int32, (c, c), 1)
    tri = (col <= row).astype(BF16)
    lane = lax.broadcasted_iota(jnp.int32, (sub, HG_D), 1)
    trow = lax.broadcasted_iota(jnp.int32, (sub, HG_D), 0)

    q, k, b, bk = {}, {}, {}, {}
    for ch in chains:
        f = blk(f_ref, ch)
        k[ch] = 1.0 - f
        lf = jnp.log(f) * LOG2E
        l1 = lf.astype(BF16)
        r1 = lf - l1.astype(F32)
        l2 = r1.astype(BF16)
        l3 = (r1 - l2.astype(F32)).astype(BF16)
        b[ch] = _dot(tri, l1) + _dot(tri, l2) + _dot(tri, l3)
    for ch in chains:
        q[ch] = blk(q_ref, ch).astype(F32)
        bk[ch] = b[ch] - jnp.log(k[ch]) * LOG2E

    qb, kdec, ebl, o_inter = {}, {}, {}, {}
    for ch in chains:
        bl = b[ch][c - 1:c, :]
        qb[ch] = (q[ch] * jnp.exp2(b[ch])).astype(BF16)
        kdec[ch] = (k[ch] * jnp.exp2(bl - b[ch])).astype(BF16)
        ebl[ch] = jnp.exp2(bl)
    for hh in range(HG_HEADS):
        st = st_ref[hh]
        for ci in range(HG_STEP_CHUNKS):
            ch = (hh, ci)
            o_inter[ch] = _dot_nt(qb[ch], st.astype(BF16))
            st = st * ebl[ch] + _dot_tn(blk(v_ref, ch), kdec[ch])
        st_ref[hh] = st

    a_rows = {ch: [] for ch in chains}
    for i in range(n_sub):
        rs = slice(i * sub, (i + 1) * sub)
        tiles = {ch: jnp.zeros((sub, HG_D), F32) for ch in chains}
        for ss in range(sub):
            s = i * sub + ss
            for ch in chains:
                kdec_s = jnp.exp2(jnp.minimum(b[ch][rs] - bk[ch][s:s + 1, :], 0.0))
                a_s = jnp.sum(q[ch][rs] * kdec_s, axis=-1, keepdims=True)
                tiles[ch] = jnp.where(lane == s, a_s, tiles[ch])
        for ch in chains:
            a_rows[ch].append(jnp.where(trow + i * sub >= lane, tiles[ch], 0.0)[:, :c])
    for i in range(1, n_sub):
        rs = slice(i * sub, (i + 1) * sub)
        for n, ch in enumerate(chains):
            r = b[ch][i * sub - 1:i * sub, :]
            ks_ref[n, i - 1, :i * sub, :] = k[ch][:i * sub] * jnp.exp2(r - b[ch][:i * sub])
            qs = (q[ch][rs] * jnp.exp2(b[ch][rs] - r)).astype(BF16)
            a_rows[ch][i] = a_rows[ch][i] + _dot_nt(qs, ks_ref[n, i - 1].astype(BF16))
    o = {}
    for ch in chains:
        a = jnp.concatenate(a_rows[ch], axis=0).astype(BF16)
        o[ch] = o_inter[ch] + _dot(a, blk(v_ref, ch))

    gn = gn_ref[...]
    for ch in chains:
        hh, ci = ch
        on = o[ch] * lax.rsqrt(jnp.mean(o[ch] * o[ch], axis=-1, keepdims=True) + RMS_EPS) * gn
        o_ref[ci * c:(ci + 1) * c, hh * HG_D:(hh + 1) * HG_D] = (on * blk(g_ref, ch).astype(F32)).astype(BF16)


def _hgrn(hq, hf, hi, hg, gn, l, bsz, seq):
    n_tok = hq.shape[0]
    c = HG_CHUNK
    rows = c * HG_STEP_CHUNKS
    nc = seq // rows
    spec = pl.BlockSpec((rows, HG_WIDTH), lambda b, i: (b * nc + i, 0))
    return pl.pallas_call(
        _hgrn_kernel,
        grid=(bsz, nc),
        in_specs=[spec, spec, spec, spec, _layer_spec(gn.shape, l)],
        out_specs=spec,
        out_shape=jax.ShapeDtypeStruct((n_tok, HG_WIDTH), BF16),
        scratch_shapes=[pltpu.VMEM((HG_HEADS, HG_D, HG_D), F32),
                        pltpu.VMEM((HG_HEADS * HG_STEP_CHUNKS, c // HG_SUB - 1, c, HG_D), F32)],
        compiler_params=_params(("parallel", "arbitrary")),
        name="hgrn",
    )(hq, hf, hi, hg, gn)


def _memkv_kernel(mem_ref, g_ref, w_ref, kv_ref):
    kv_ref[...] = _dot(_rms(mem_ref[...], g_ref[...]).astype(BF16), w_ref[...]).astype(BF16)


def _memkv(mem2, g, w, l, n_mem):
    n_rows, d_model = mem2.shape
    return pl.pallas_call(
        _memkv_kernel,
        grid=(n_rows // n_mem,),
        in_specs=[pl.BlockSpec((n_mem, d_model), lambda i: (i, 0)), _layer_spec(g.shape, l), _layer_spec(w.shape, l)],
        out_specs=pl.BlockSpec((n_mem, 2 * X_WIDTH), lambda i: (i, 0)),
        out_shape=jax.ShapeDtypeStruct((n_rows, 2 * X_WIDTH), BF16),
        compiler_params=_params(("parallel",)),
        name="memkv",
    )(mem2, g, w)


def _mid_kernel(x_ref, ys_ref, om_ref, oh_ref, gate_ref, kv_ref, wmo_ref, who_ref, wout_ref, gx_ref, wq_ref,
                wxo_ref, o_ref, *, x_scale):
    d_model = x_ref.shape[1]
    om_t = om_ref[0].reshape(MLA_HEADS * MLA_V, x_ref.shape[0])
    y_mla = _dot_tn(om_t, wmo_ref[...])
    y_hg = _dot(oh_ref[...], who_ref[...])
    gate = lambda br: gate_ref[:, br * d_model:(br + 1) * d_model].astype(F32)
    merged = gate(0) * ys_ref[...] + gate(1) * y_mla + gate(2) * y_hg
    x = x_ref[...] + _dot(merged.astype(BF16), wout_ref[...])

    q = (_dot(_rms(x, gx_ref[...]).astype(BF16), wq_ref[...]) * x_scale).astype(BF16)
    outs = []
    for hh in range(X_HEADS):
        sl = slice(hh * X_HEAD_DIM, (hh + 1) * X_HEAD_DIM)
        s = _dot_nt(q[:, sl], kv_ref[:, sl])
        pr = jnp.exp2(s - jnp.max(s, axis=-1, keepdims=True))
        den = jnp.sum(pr, axis=-1, keepdims=True)
        vh = kv_ref[:, X_WIDTH + hh * X_HEAD_DIM:X_WIDTH + (hh + 1) * X_HEAD_DIM]
        outs.append((_dot(pr.astype(BF16), vh) / den).astype(BF16))
    o_ref[...] = x + _dot(jnp.concatenate(outs, axis=1), wxo_ref[...])


def _mid(x2, ys, om, oh, gates, kv, wmo, who, wout, gx, wq, wxo, l, seq, n_mem):
    n_tok, d_model = x2.shape
    t = MID_TILE
    nt = seq // t
    tok = lambda w_: pl.BlockSpec((t, w_), lambda i: (i, 0))
    return pl.pallas_call(
        functools.partial(_mid_kernel, x_scale=LOG2E / math.sqrt(X_HEAD_DIM)),
        grid=(n_tok // t,),
        in_specs=[tok(d_model), tok(d_model),
                  pl.BlockSpec((1, MLA_HEADS, MLA_V, t), lambda i: (i // nt, 0, 0, i % nt)),
                  tok(HG_WIDTH), tok(N_BRANCH * d_model),
                  pl.BlockSpec((n_mem, 2 * X_WIDTH), lambda i: (i // nt, 0)),
                  ] + [_layer_spec(w_.shape, l) for w_ in (wmo, who, wout, gx, wq, wxo)],
        out_specs=tok(d_model),
        out_shape=jax.ShapeDtypeStruct((n_tok, d_model), F32),
        compiler_params=_params(("parallel",)),
        name="mid",
    )(x2, ys, om, oh, gates, kv, wmo, who, wout, gx, wq, wxo)


def _ffn_kernel(x_ref, g_ref, wgu_ref, wd_ref, gf_ref, o_ref, *, final_norm):
    d_ff = wd_ref.shape[0]
    x = x_ref[...]
    h = _rms(x, g_ref[...]).astype(BF16)
    y = x
    for c0 in range(0, d_ff, FFN_CHUNK):
        gt = _dot(h, wgu_ref[:, c0:c0 + FFN_CHUNK])
        up = _dot(h, wgu_ref[:, d_ff + c0:d_ff + c0 + FFN_CHUNK])
        act = (gt * _sigmoid(gt) * up).astype(BF16)
        y = y + _dot(act, wd_ref[c0:c0 + FFN_CHUNK, :])
    if final_norm:
        y = _rms(y, gf_ref[...])
    o_ref[...] = y


def _ffn(x2, g, wgu, wd, gf, l, final_norm):
    n_tok, d_model = x2.shape
    t = FFN_TILE
    tok = pl.BlockSpec((t, d_model), lambda i: (i, 0))
    return pl.pallas_call(
        functools.partial(_ffn_kernel, final_norm=final_norm),
        grid=(n_tok // t,),
        in_specs=[tok, _layer_spec(g.shape, l), _layer_spec(wgu.shape, l), _layer_spec(wd.shape, l),
                  _const_spec(gf.shape)],
        out_specs=tok,
        out_shape=jax.ShapeDtypeStruct((n_tok, d_model), F32),
        compiler_params=_params(("parallel",)),
        name="ffn",
    )(x2, g, wgu, wd, gf)


def _in_proj_weights(w_in, d_model):
    c_kpe = SSM_WIDTH + MLA_Q_RANK + MLA_KV_RANK
    half = MLA_ROPE // 2
    kpe = w_in[:, :, c_kpe:c_kpe + MLA_ROPE]
    kpe_rot = jnp.concatenate([-kpe[:, :, half:], kpe[:, :, :half]], axis=2)
    pad = lambda a: jnp.pad(a, ((0, 0), (0, 0), (MLA_NOPE, HEAD_PAD - MLA_NOPE - MLA_ROPE)))
    cols = [w_in[:, :, :c_kpe], pad(kpe), pad(kpe_rot), w_in[:, :, c_kpe + MLA_ROPE:]]
    return jnp.concatenate(cols, axis=2).astype(BF16)


def _mla_weights(w_uq, w_ukv):
    depth, rank, _ = w_uq.shape
    half = MLA_ROPE // 2
    dq = MLA_NOPE + MLA_ROPE
    wq = w_uq.reshape(depth, rank, MLA_HEADS, dq)
    pe = wq[..., MLA_NOPE:]
    pe_rot = jnp.concatenate([-pe[..., half:], pe[..., :half]], axis=3)
    zpad = jnp.zeros((depth, rank, MLA_HEADS, HEAD_PAD - dq), F32)
    wuq = jnp.concatenate([wq, zpad], axis=3).reshape(depth, rank, MLA_HEADS * HEAD_PAD)
    wuqr = jnp.concatenate([jnp.zeros_like(wq[..., :MLA_NOPE]), pe_rot, zpad], axis=3).reshape(depth, rank, -1)
    kvr = w_ukv.shape[1]
    wkv = w_ukv.reshape(depth, kvr, MLA_HEADS, MLA_NOPE + MLA_V)
    hpad = lambda a: jnp.pad(a, ((0, 0), (0, 0), (0, 0), (0, HEAD_PAD - a.shape[3]))).reshape(depth, kvr, -1)
    wuk = hpad(wkv[..., :MLA_NOPE])
    wuv = hpad(wkv[..., MLA_NOPE:])
    return wuq.astype(BF16), wuqr.astype(BF16), wuk.astype(BF16), wuv.astype(BF16)


def _s5_weights(lam_re, lam_im, b_re, b_im, c_re, c_im, log_step):
    depth = lam_re.shape[0]
    step = jnp.exp(log_step)[..., None]
    mag = jnp.exp(lam_re * step)
    lbr = mag * jnp.cos(lam_im * step)
    lbi = mag * jnp.sin(lam_im * step)
    den = lam_re * lam_re + lam_im * lam_im
    cr = ((lbr - 1.0) * lam_re + lbi * lam_im) / den
    ci = (lbi * lam_re - (lbr - 1.0) * lam_im) / den
    bbr = cr[..., None] * b_re - ci[..., None] * b_im
    bbi = cr[..., None] * b_im + ci[..., None] * b_re
    gq = SSM_GROUPS // SSM_QUARTERS
    eye = jnp.eye(gq, dtype=F32)

    def blockdiag_in(m):
        m = m.reshape(depth, SSM_QUARTERS, gq, SSM_STATE, SSM_GROUP_CH)
        return jnp.einsum("ab,lqaph->lqahbp", eye, m).reshape(depth, SSM_QUARTERS, gq * SSM_GROUP_CH,
                                                              gq * SSM_STATE)

    def blockdiag_out(m):
        m = m.reshape(depth, SSM_QUARTERS, gq, SSM_GROUP_CH, SSM_STATE)
        return jnp.einsum("ab,lqahp->lqapbh", eye, m).reshape(depth, SSM_QUARTERS, gq * SSM_STATE,
                                                              gq * SSM_GROUP_CH)

    wb = jnp.concatenate([blockdiag_in(bbr), blockdiag_in(bbi)], axis=3).astype(BF16)
    wc = jnp.concatenate([blockdiag_out(c_re), -blockdiag_out(c_im)], axis=3).astype(BF16)
    lr = lbr.reshape(depth, 1, SSM_CH)
    li = lbi.reshape(depth, 1, SSM_CH)
    a = jnp.broadcast_to(lr, (depth, 8, SSM_CH))
    bsw = jnp.concatenate([jnp.broadcast_to(-li, (depth, 4, SSM_CH)), jnp.broadcast_to(li, (depth, 4, SSM_CH))],
                          axis=1)
    return wb, a, bsw, wc


def kernel(x, mem, positions, norm_mix, w_in, ssm_lam_re, ssm_lam_im, ssm_b_re, ssm_b_im, ssm_c_re, ssm_c_im, ssm_d, ssm_log_step, ssm_w_glu, mla_q_norm, mla_kv_norm, mla_w_uq, mla_w_ukv, mla_w_o, hg_lb, hg_g_norm, hg_w_o, w_out, norm_cross, norm_mem, x_w_q, x_w_kv, x_w_o, norm_ffn, ffn_w_gate_up, ffn_w_down, norm_final):
    bsz, seq, d_model = x.shape
    depth = w_in.shape[0]
    n_mem = mem.shape[1]
    assert bsz == 4, "the s5 scan packs 4 batches x (re, im) on the 8 sublanes"
    n_tok = bsz * seq

    half = MLA_ROPE // 2
    inv_freq = ROPE_THETA ** (-jnp.arange(half, dtype=F32) / half)
    invf = jnp.zeros((1, HEAD_PAD), F32).at[0, MLA_NOPE:MLA_NOPE + half].set(inv_freq)
    invf = invf.at[0, MLA_NOPE + half:MLA_NOPE + MLA_ROPE].set(inv_freq)
    lb_p = jax.nn.softmax(hg_lb.astype(F32), axis=0)
    lower_bounds = jnp.cumsum(lb_p, axis=0) - lb_p[0:1]

    rows = lambda a: a.reshape(depth, 1, -1)
    bf = lambda a: a.astype(BF16)
    w_inp = _in_proj_weights(w_in, d_model)
    wuq, wuqr, wuk, wuv = _mla_weights(mla_w_uq, mla_w_ukv)
    wb, a, bsw, wc = _s5_weights(ssm_lam_re, ssm_lam_im, ssm_b_re, ssm_b_im, ssm_c_re, ssm_c_im, ssm_log_step)
    wglu, wmo, who, wout = bf(ssm_w_glu), bf(mla_w_o), bf(hg_w_o), bf(w_out)
    wxq, wxkv, wxo, wgu, wdn = bf(x_w_q), bf(x_w_kv), bf(x_w_o), bf(ffn_w_gate_up), bf(ffn_w_down)
    g_mix, g_q, g_kv, g_hg = rows(norm_mix), rows(mla_q_norm), rows(mla_kv_norm), rows(hg_g_norm)
    g_cross, g_mem, g_ffn = rows(norm_cross), rows(norm_mem), rows(norm_ffn)
    lbs, ssm_dr = rows(lower_bounds), rows(ssm_d)

    x2 = x.reshape(n_tok, d_model)
    mem2 = mem.reshape(bsz * n_mem, d_model)
    pos2 = positions.reshape(n_tok, 1)

    for l in range(depth):
        u, q, k, v, hq, hf, hi, hg, gates = _in_proj(x2, pos2, g_mix, w_inp, invf, g_q, wuq, wuqr, g_kv, wuk, wuv,
                                                     lbs, l, bsz, seq)
        ys = _s5(u.reshape(bsz, seq, SSM_WIDTH), wb, a, bsw, wc, ssm_dr, wglu, l, d_model).reshape(n_tok, d_model)
        om = _mla(q, k, v)
        oh = _hgrn(hq, hf, hi, hg, g_hg, l, bsz, seq)
        kv = _memkv(mem2, g_mem, wxkv, l, n_mem)
        x2 = _mid(x2, ys, om, oh, gates, kv, wmo, who, wout, g_cross, wxq, wxo, l, seq, n_mem)
        x2 = _ffn(x2, g_ffn, wgu, wdn, norm_final.reshape(1, -1), l, final_norm=(l == depth - 1))
    return x2.reshape(bsz, seq, d_model)
```

```python
import functools
import math

import jax
import jax.numpy as jnp
from jax import lax
from jax.experimental import pallas as pl
from jax.experimental.pallas import tpu as pltpu

F32 = jnp.float32
BF16 = jnp.bfloat16

RMS_EPS = 1e-6
LOG2E = 1.4426950408889634

SSM_GROUPS = 32
SSM_GROUP_CH = 16
SSM_STATE = 64
SSM_WIDTH = SSM_GROUPS * SSM_GROUP_CH
SSM_CH = SSM_GROUPS * SSM_STATE
SSM_QUARTERS = 4

MLA_HEADS = 8
MLA_Q_RANK = 512
MLA_KV_RANK = 256
MLA_NOPE = 64
MLA_ROPE = 32
MLA_V = 64
ROPE_THETA = 10000.0
HEAD_PAD = 128
MLA_VT_ROWS = 80

HG_HEADS = 4
HG_D = 128
HG_WIDTH = HG_HEADS * HG_D
HG_CHUNK = 64
HG_SUB = 8
HG_STEP_CHUNKS = 4

X_HEADS = 4
X_HEAD_DIM = 128
X_WIDTH = X_HEADS * X_HEAD_DIM

N_BRANCH = 3

VMEM_LIMIT = 56 * 1024 * 1024

IN_TILE = 512
S5_TIME = 128
MLA_BLOCK = 512
MID_TILE = 512
FFN_TILE = 512
FFN_CHUNK = 1408


def _dot(a, b):
    return jnp.dot(a, b, preferred_element_type=F32)


def _dot_nt(a, b):
    return lax.dot_general(a, b, (((1,), (1,)), ((), ())), preferred_element_type=F32)


def _dot_tn(a, b):
    return lax.dot_general(a, b, (((0,), (0,)), ((), ())), preferred_element_type=F32)


def _rms(x, g):
    return x * lax.rsqrt(jnp.mean(x * x, axis=-1, keepdims=True) + RMS_EPS) * g


def _sigmoid(x):
    return 1.0 / (1.0 + jnp.exp(-x))


def _const_spec(shape):
    nd = len(shape)
    return pl.BlockSpec(shape, lambda *_: (0,) * nd, pipeline_mode=pl.Buffered(1))


def _layer_spec(shape, l):
    nd = len(shape)
    return pl.BlockSpec((None,) + tuple(shape[1:]), lambda *_: (l,) + (0,) * (nd - 1),
                        pipeline_mode=pl.Buffered(1))


def _params(sem):
    return pltpu.CompilerParams(dimension_semantics=sem, vmem_limit_bytes=VMEM_LIMIT)


C_SSM = 0
C_QLAT = C_SSM + SSM_WIDTH
C_KVLAT = C_QLAT + MLA_Q_RANK
C_KPE = C_KVLAT + MLA_KV_RANK
C_HQ = C_KPE + HEAD_PAD
C_HF = C_HQ + HG_WIDTH
C_HI = C_HF + HG_WIDTH
C_HG = C_HI + HG_WIDTH
C_GATE = C_HG + HG_WIDTH


def _in_proj_kernel(x_ref, pos_ref, g_ref, w_ref, invf_ref, qn_ref, wuq_ref, kvn_ref, wuk_ref, wuvt_ref, lb_ref,
                    u_ref, q_ref, k_ref, v_ref, hq_ref, hf_ref, hi_ref, hg_ref, gate_ref, *, q_scale):
    d_model = x_ref.shape[1]
    h = _rms(x_ref[...], g_ref[...]).astype(BF16)

    def seg(a, b):
        return _dot(h, w_ref[:, a:b])

    qlat = seg(C_QLAT, C_KVLAT)
    kvlat = seg(C_KVLAT, C_KPE)
    kp = seg(C_KPE, C_HQ)
    qn = _rms(qlat, qn_ref[...]).astype(BF16)
    kvn = _rms(kvlat, kvn_ref[...]).astype(BF16)

    t = x_ref.shape[0]
    ang = invf_ref[...] * pos_ref[0].astype(F32)
    cos_c = jnp.cos(ang)
    sin_c = jnp.sin(ang)
    fill = lambda v, n: jnp.full((n, t), v, F32)
    tail = HEAD_PAD - MLA_NOPE - MLA_ROPE
    cos = jnp.concatenate([fill(1.0, MLA_NOPE), cos_c, cos_c, fill(1.0, tail)], axis=0).T
    sin = jnp.concatenate([fill(0.0, MLA_NOPE), -sin_c, sin_c, fill(0.0, tail)], axis=0).T
    lane = lax.broadcasted_iota(jnp.int32, (1, HEAD_PAD), 1)
    first_half = lane < MLA_NOPE + MLA_ROPE // 2

    def rope(v):
        swapped = jnp.where(first_half, pltpu.roll(v, HEAD_PAD - MLA_ROPE // 2, 1),
                            pltpu.roll(v, MLA_ROPE // 2, 1))
        return v * cos + swapped * sin

    u_ref[...] = seg(C_SSM, C_QLAT)
    hq = seg(C_HQ, C_HF)
    hq_ref[...] = (hq * _sigmoid(hq)).astype(BF16)
    lb = lb_ref[...]
    hf_ref[...] = lb + (1.0 - lb) * _sigmoid(seg(C_HF, C_HI))
    hi_ref[...] = seg(C_HI, C_HG).astype(BF16)
    hg = seg(C_HG, C_GATE)
    hg_ref[...] = (hg * _sigmoid(hg)).astype(BF16)
    for br in range(N_BRANCH):
        gate_ref[:, br * d_model:(br + 1) * d_model] = _sigmoid(
            seg(C_GATE + br * d_model, C_GATE + (br + 1) * d_model)).astype(BF16)

    qa = _dot(qn, wuq_ref[...])
    for hh in range(MLA_HEADS):
        q_ref[0, hh] = (rope(qa[:, hh * HEAD_PAD:(hh + 1) * HEAD_PAD]) * q_scale).astype(BF16)

    kn = _dot(kvn, wuk_ref[...])
    vt = _dot_nt(wuvt_ref[...], kvn)
    kpe = rope(kp)
    nope_lanes = lane < MLA_NOPE
    ones_rows = (lax.broadcasted_iota(jnp.int32, (MLA_VT_ROWS - MLA_V, t), 0) == 0).astype(BF16)
    for hh in range(MLA_HEADS):
        piece = kn[:, (hh // 2) * HEAD_PAD:(hh // 2 + 1) * HEAD_PAD]
        if hh % 2:
            piece = pltpu.roll(piece, MLA_NOPE, 1)
        k_ref[0, hh] = jnp.where(nope_lanes, piece, kpe).astype(BF16)
        v_ref[0, hh, :MLA_V, :] = vt[hh * MLA_V:(hh + 1) * MLA_V].astype(BF16)
        v_ref[0, hh, MLA_V:, :] = ones_rows


def _in_proj(x2, pos2, g, w, invf, qn, wuq, kvn, wuk, wuvt, lb, l, bsz, seq):
    n_tok, d_model = x2.shape
    t = IN_TILE
    nt = seq // t
    tok = lambda w_: pl.BlockSpec((t, w_), lambda i: (i, 0))
    lay = lambda a: _layer_spec(a.shape, l)
    head = pl.BlockSpec((1, MLA_HEADS, t, HEAD_PAD), lambda i: (i // nt, 0, i % nt, 0))
    head_shape = jax.ShapeDtypeStruct((bsz, MLA_HEADS, seq, HEAD_PAD), BF16)
    head_t = pl.BlockSpec((1, MLA_HEADS, MLA_VT_ROWS, t), lambda i: (i // nt, 0, 0, i % nt))
    head_t_shape = jax.ShapeDtypeStruct((bsz, MLA_HEADS, MLA_VT_ROWS, seq), BF16)
    q_scale = LOG2E / math.sqrt(MLA_NOPE + MLA_ROPE)
    return pl.pallas_call(
        functools.partial(_in_proj_kernel, q_scale=q_scale),
        grid=(n_tok // t,),
        in_specs=[tok(d_model), pl.BlockSpec((1, 1, t), lambda i: (i, 0, 0)), lay(g), lay(w),
                  _const_spec(invf.shape), lay(qn), lay(wuq), lay(kvn), lay(wuk), lay(wuvt), lay(lb)],
        out_specs=[tok(SSM_WIDTH), head, head, head_t, tok(HG_WIDTH), tok(HG_WIDTH), tok(HG_WIDTH), tok(HG_WIDTH),
                   tok(N_BRANCH * d_model)],
        out_shape=[jax.ShapeDtypeStruct((n_tok, SSM_WIDTH), F32), head_shape, head_shape, head_t_shape,
                   jax.ShapeDtypeStruct((n_tok, HG_WIDTH), BF16), jax.ShapeDtypeStruct((n_tok, HG_WIDTH), F32),
                   jax.ShapeDtypeStruct((n_tok, HG_WIDTH), BF16), jax.ShapeDtypeStruct((n_tok, HG_WIDTH), BF16),
                   jax.ShapeDtypeStruct((n_tok, N_BRANCH * d_model), BF16)],
        compiler_params=_params(("parallel",)),
        name="in_proj",
    )(x2, pos2, g, w, invf, qn, wuq, kvn, wuk, wuvt, lb)


def _s5_kernel(u_ref, wb_ref, a_ref, bsw_ref, wc_ref, d_ref, wglu_ref, y_ref, se_ref, so_ref, st_ref, utb_ref,
               ytb_ref):
    bsz, tc, _ = u_ref.shape
    rows = bsz * tc
    d_model = y_ref.shape[2]
    qw = SSM_CH // SSM_QUARTERS
    uw = SSM_WIDTH // SSM_QUARTERS
    half = 4

    @pl.when(pl.program_id(0) == 0)
    def _():
        st_ref[...] = jnp.zeros_like(st_ref)

    for q in range(SSM_QUARTERS):
        for b in range(bsz):
            utb_ref[q, pl.ds(b, tc, stride=bsz), :] = u_ref[b, :, q * uw:(q + 1) * uw]

    low = (lax.broadcasted_iota(jnp.int32, (rows, qw), 0) % 8) < half
    for q in range(SSM_QUARTERS):
        o = _dot(utb_ref[q].astype(BF16), wb_ref[q])
        re = o[:, :qw]
        im = o[:, qw:]
        se_ref[:, q * qw:(q + 1) * qw] = jnp.where(low, re, pltpu.roll(im, half, 0))
        so_ref[:, q * qw:(q + 1) * qw] = jnp.where(low, pltpu.roll(re, rows - half, 0), im)

    n_blk = 2
    cw = SSM_CH // n_blk
    for cb in range(n_blk):
        cs = slice(cb * cw, (cb + 1) * cw)
        a = a_ref[:, cs]
        bs = bsw_ref[:, cs]

        def body(v, s, cs=cs, a=a, bs=bs):
            r = pl.multiple_of(v * 8, 8)
            s = a * s + bs * pltpu.roll(s, half, 0) + se_ref[pl.ds(r, 8), cs]
            se_ref[pl.ds(r, 8), cs] = s
            s = a * s + bs * pltpu.roll(s, half, 0) + so_ref[pl.ds(r, 8), cs]
            so_ref[pl.ds(r, 8), cs] = s
            return s

        st_ref[:, cs] = lax.fori_loop(0, rows // 8, body, st_ref[:, cs], unroll=4)

    yw = uw
    low_y = low[:, :yw]
    pieces = []
    for q in range(SSM_QUARTERS):
        oe = _dot(se_ref[:, q * qw:(q + 1) * qw].astype(BF16), wc_ref[q])
        oo = _dot(so_ref[:, q * qw:(q + 1) * qw].astype(BF16), wc_ref[q])
        ye = oe[:, :yw] + pltpu.roll(oe[:, yw:], rows - half, 0)
        yo = pltpu.roll(oo[:, :yw], half, 0) + oo[:, yw:]
        pieces.append(jnp.where(low_y, ye, yo) + d_ref[:, q * uw:(q + 1) * uw] * utb_ref[q])
    y = jax.nn.gelu(jnp.concatenate(pieces, axis=1)).astype(BF16)
    z = _dot(y, wglu_ref[...])
    res = z[:, :d_model] * _sigmoid(z[:, d_model:])
    n_slab = d_model // uw
    for j in range(n_slab):
        ytb_ref[j] = res[:, j * uw:(j + 1) * uw]
    for j in range(n_slab):
        for b in range(bsz):
            y_ref[b, :, j * uw:(j + 1) * uw] = ytb_ref[j, pl.ds(b, tc, stride=bsz), :]


def _s5(u3, wb, a, bsw, wc, d, wglu, l, d_model):
    bsz, seq, _ = u3.shape
    tc = S5_TIME
    rows = tc * bsz
    lanes = SSM_WIDTH // SSM_QUARTERS
    return pl.pallas_call(
        _s5_kernel,
        grid=(seq // tc,),
        in_specs=[pl.BlockSpec((bsz, tc, SSM_WIDTH), lambda i: (0, i, 0))]
        + [_layer_spec(w_.shape, l) for w_ in (wb, a, bsw, wc, d, wglu)],
        out_specs=pl.BlockSpec((bsz, tc, d_model), lambda i: (0, i, 0)),
        out_shape=jax.ShapeDtypeStruct((bsz, seq, d_model), F32),
        scratch_shapes=[pltpu.VMEM((rows, SSM_CH), F32), pltpu.VMEM((rows, SSM_CH), F32),
                        pltpu.VMEM((8, SSM_CH), F32), pltpu.VMEM((SSM_QUARTERS, rows, lanes), F32),
                        pltpu.VMEM((d_model // lanes, rows, lanes), F32)],
        compiler_params=_params(("arbitrary",)),
        name="s5",
    )(u3, wb, a, bsw, wc, d, wglu)


def _mla_kernel(qi_ref, kj_ref, q_ref, k_ref, vt_ref, o_ref, m_ref, acc_ref):
    p = pl.program_id(1)
    qi = qi_ref[p]
    kj = kj_ref[p]
    blk = q_ref.shape[2]

    @pl.when(kj == 0)
    def _():
        m_ref[...] = jnp.full_like(m_ref, -1e30)
        acc_ref[...] = jnp.zeros_like(acc_ref)

    def step(masked):
        if masked:
            key = lax.broadcasted_iota(jnp.int32, (blk, blk), 0)
            qry = lax.broadcasted_iota(jnp.int32, (blk, blk), 1)
            causal = key <= qry
        sc, mx = {}, {}
        for t in range(MLA_HEADS + 2):
            if t < MLA_HEADS:
                s = _dot_nt(k_ref[0, t], q_ref[0, t])
                sc[t] = jnp.where(causal, s, -1e30) if masked else s
            if 1 <= t <= MLA_HEADS:
                hh = t - 1
                m_prev = m_ref[hh]
                m_new = jnp.maximum(m_prev, jnp.max(sc[hh], axis=0, keepdims=True))
                mx[hh] = (m_new, jnp.exp2(m_prev - m_new))
                m_ref[hh] = m_new
            if t >= 2:
                hh = t - 2
                m_new, alpha = mx.pop(hh)
                pr = jnp.exp2(sc.pop(hh) - m_new).astype(BF16)
                acc_ref[hh] = alpha * acc_ref[hh] + _dot(vt_ref[0, hh], pr)

    @pl.when(kj < qi)
    def _():
        step(False)

    @pl.when(kj == qi)
    def _():
        step(True)
        for hh in range(MLA_HEADS):
            acc = acc_ref[hh]
            o_ref[0, hh] = (acc[:MLA_V, :] / acc[MLA_V:MLA_V + 1, :]).astype(BF16)


def _mla(q, k, vt):
    bsz, nh, seq, hd = q.shape
    blk = MLA_BLOCK
    nb = seq // blk
    pairs = [(i, j) for i in range(nb) for j in range(i + 1)]
    qi = jnp.asarray([p[0] for p in pairs], jnp.int32)
    kj = jnp.asarray([p[1] for p in pairs], jnp.int32)
    qspec = pl.BlockSpec((1, nh, blk, hd), lambda b, p, qi, kj: (b, 0, qi[p], 0))
    kspec = pl.BlockSpec((1, nh, blk, hd), lambda b, p, qi, kj: (b, 0, kj[p], 0))
    vspec = pl.BlockSpec((1, nh, MLA_VT_ROWS, blk), lambda b, p, qi, kj: (b, 0, 0, kj[p]))
    ospec = pl.BlockSpec((1, nh, MLA_V, blk), lambda b, p, qi, kj: (b, 0, 0, qi[p]))
    return pl.pallas_call(
        _mla_kernel,
        grid_spec=pltpu.PrefetchScalarGridSpec(
            num_scalar_prefetch=2,
            grid=(bsz, len(pairs)),
            in_specs=[qspec, kspec, vspec],
            out_specs=ospec,
            scratch_shapes=[pltpu.VMEM((nh, 1, blk), F32), pltpu.VMEM((nh, MLA_VT_ROWS, blk), F32)],
        ),
        out_shape=jax.ShapeDtypeStruct((bsz, nh, MLA_V, seq), BF16),
        compiler_params=_params(("parallel", "arbitrary")),
        name="mla",
    )(qi, kj, q, k, vt)


def _hgrn_kernel(q_ref, f_ref, v_ref, g_ref, gn_ref, o_ref, st_ref, ks_ref):
    c = HG_CHUNK
    sub = HG_SUB
    n_sub = c // sub

    @pl.when(pl.program_id(1) == 0)
    def _():
        st_ref[...] = jnp.zeros_like(st_ref)
        ks_ref[...] = jnp.zeros_like(ks_ref)

    chains = [(hh, ci) for hh in range(HG_HEADS) for ci in range(HG_STEP_CHUNKS)]

    def blk(ref, ch):
        hh, ci = ch
        return ref[ci * c:(ci + 1) * c, hh * HG_D:(hh + 1) * HG_D]

    row = lax.broadcasted_iota(jnp.int32, (c, c), 0)
    col = lax.broadcasted_iota(jnp.int32, (c, c), 1)
    tri = (col <= row).astype(BF16)
    lane = lax.broadcasted_iota(jnp.int32, (sub, HG_D), 1)
    trow = lax.broadcasted_iota(jnp.int32, (sub, HG_D), 0)

    q, k, b, bk = {}, {}, {}, {}
    for ch in chains:
        f = blk(f_ref, ch)
        k[ch] = 1.0 - f
        lf = jnp.log(f) * LOG2E
        l1 = lf.astype(BF16)
        r1 = lf - l1.astype(F32)
        l2 = r1.astype(BF16)
        l3 = (r1 - l2.astype(F32)).astype(BF16)
        b[ch] = _dot(tri, l1) + _dot(tri, l2) + _dot(tri, l3)
    for ch in chains:
        q[ch] = blk(q_ref, ch).astype(F32)
        bk[ch] = b[ch] - jnp.log(k[ch]) * LOG2E

    qb, kdec, ebl, o_inter = {}, {}, {}, {}
    for ch in chains:
        bl = b[ch][c - 1:c, :]
        qb[ch] = (q[ch] * jnp.exp2(b[ch])).astype(BF16)
        kdec[ch] = (k[ch] * jnp.exp2(bl - b[ch])).astype(BF16)
        ebl[ch] = jnp.exp2(bl)
    for hh in range(HG_HEADS):
        st = st_ref[hh]
        for ci in range(HG_STEP_CHUNKS):
            ch = (hh, ci)
            o_inter[ch] = _dot_nt(qb[ch], st.astype(BF16))
            st = st * ebl[ch] + _dot_tn(blk(v_ref, ch), kdec[ch])
        st_ref[hh] = st

    a_rows = {ch: [] for ch in chains}
    for i in range(n_sub):
        rs = slice(i * sub, (i + 1) * sub)
        tiles = {ch: jnp.zeros((sub, HG_D), F32) for ch in chains}
        for ss in range(sub):
            s = i * sub + ss
            for ch in chains:
                kdec_s = jnp.exp2(jnp.minimum(b[ch][rs] - bk[ch][s:s + 1, :], 0.0))
                a_s = jnp.sum(q[ch][rs] * kdec_s, axis=-1, keepdims=True)
                tiles[ch] = jnp.where(lane == s, a_s, tiles[ch])
        for ch in chains:
            a_rows[ch].append(jnp.where(trow + i * sub >= lane, tiles[ch], 0.0)[:, :c])
    for i in range(1, n_sub):
        rs = slice(i * sub, (i + 1) * sub)
        for n, ch in enumerate(chains):
            r = b[ch][i * sub - 1:i * sub, :]
            ks_ref[n, i - 1, :i * sub, :] = k[ch][:i * sub] * jnp.exp2(r - b[ch][:i * sub])
            qs = (q[ch][rs] * jnp.exp2(b[ch][rs] - r)).astype(BF16)
            a_rows[ch][i] = a_rows[ch][i] + _dot_nt(qs, ks_ref[n, i - 1].astype(BF16))
    o = {}
    for ch in chains:
        a = jnp.concatenate(a_rows[ch], axis=0).astype(BF16)
        o[ch] = o_inter[ch] + _dot(a, blk(v_ref, ch))

    gn = gn_ref[...]
    for ch in chains:
        hh, ci = ch
        on = o[ch] * lax.rsqrt(jnp.mean(o[ch] * o[ch], axis=-1, keepdims=True) + RMS_EPS) * gn
        o_ref[ci * c:(ci + 1) * c, hh * HG_D:(hh + 1) * HG_D] = (on * blk(g_ref, ch).astype(F32)).astype(BF16)


def _hgrn(hq, hf, hi, hg, gn, l, bsz, seq):
    n_tok = hq.shape[0]
    c = HG_CHUNK
    rows = c * HG_STEP_CHUNKS
    nc = seq // rows
    spec = pl.BlockSpec((rows, HG_WIDTH), lambda b, i: (b * nc + i, 0))
    return pl.pallas_call(
        _hgrn_kernel,
        grid=(bsz, nc),
        in_specs=[spec, spec, spec, spec, _layer_spec(gn.shape, l)],
        out_specs=spec,
        out_shape=jax.ShapeDtypeStruct((n_tok, HG_WIDTH), BF16),
        scratch_shapes=[pltpu.VMEM((HG_HEADS, HG_D, HG_D), F32),
                        pltpu.VMEM((HG_HEADS * HG_STEP_CHUNKS, c // HG_SUB - 1, c, HG_D), F32)],
        compiler_params=_params(("parallel", "arbitrary")),
        name="hgrn",
    )(hq, hf, hi, hg, gn)


def _memkv_kernel(mem_ref, g_ref, w_ref, kv_ref):
    kv_ref[...] = _dot(_rms(mem_ref[...], g_ref[...]).astype(BF16), w_ref[...]).astype(BF16)


def _memkv(mem2, g, w, l, n_mem):
    n_rows, d_model = mem2.shape
    return pl.pallas_call(
        _memkv_kernel,
        grid=(n_rows // n_mem,),
        in_specs=[pl.BlockSpec((n_mem, d_model), lambda i: (i, 0)), _layer_spec(g.shape, l), _layer_spec(w.shape, l)],
        out_specs=pl.BlockSpec((n_mem, 2 * X_WIDTH), lambda i: (i, 0)),
        out_shape=jax.ShapeDtypeStruct((n_rows, 2 * X_WIDTH), BF16),
        compiler_params=_params(("parallel",)),
        name="memkv",
    )(mem2, g, w)


def _mid_kernel(x_ref, ys_ref, om_ref, oh_ref, gate_ref, kv_ref, wmo_ref, who_ref, wout_ref, gx_ref, wq_ref,
                wxo_ref, o_ref, *, x_scale):
    d_model = x_ref.shape[1]
    om_t = om_ref[0].reshape(MLA_HEADS * MLA_V, x_ref.shape[0])
    y_mla = _dot_tn(om_t, wmo_ref[...])
    y_hg = _dot(oh_ref[...], who_ref[...])
    gate = lambda br: gate_ref[:, br * d_model:(br + 1) * d_model].astype(F32)
    merged = gate(0) * ys_ref[...] + gate(1) * y_mla + gate(2) * y_hg
    x = x_ref[...] + _dot(merged.astype(BF16), wout_ref[...])

    q = (_dot(_rms(x, gx_ref[...]).astype(BF16), wq_ref[...]) * x_scale).astype(BF16)
    outs = []
    for hh in range(X_HEADS):
        sl = slice(hh * X_HEAD_DIM, (hh + 1) * X_HEAD_DIM)
        s = _dot_nt(q[:, sl], kv_ref[:, sl])
        pr = jnp.exp2(s - jnp.max(s, axis=-1, keepdims=True))
        den = jnp.sum(pr, axis=-1, keepdims=True)
        vh = kv_ref[:, X_WIDTH + hh * X_HEAD_DIM:X_WIDTH + (hh + 1) * X_HEAD_DIM]
        outs.append((_dot(pr.astype(BF16), vh) / den).astype(BF16))
    o_ref[...] = x + _dot(jnp.concatenate(outs, axis=1), wxo_ref[...])


def _mid(x2, ys, om, oh, gates, kv, wmo, who, wout, gx, wq, wxo, l, seq, n_mem):
    n_tok, d_model = x2.shape
    t = MID_TILE
    nt = seq // t
    tok = lambda w_: pl.BlockSpec((t, w_), lambda i: (i, 0))
    return pl.pallas_call(
        functools.partial(_mid_kernel, x_scale=LOG2E / math.sqrt(X_HEAD_DIM)),
        grid=(n_tok // t,),
        in_specs=[tok(d_model), tok(d_model),
                  pl.BlockSpec((1, MLA_HEADS, MLA_V, t), lambda i: (i // nt, 0, 0, i % nt)),
                  tok(HG_WIDTH), tok(N_BRANCH * d_model),
                  pl.BlockSpec((n_mem, 2 * X_WIDTH), lambda i: (i // nt, 0)),
                  ] + [_layer_spec(w_.shape, l) for w_ in (wmo, who, wout, gx, wq, wxo)],
        out_specs=tok(d_model),
        out_shape=jax.ShapeDtypeStruct((n_tok, d_model), F32),
        compiler_params=_params(("parallel",)),
        name="mid",
    )(x2, ys, om, oh, gates, kv, wmo, who, wout, gx, wq, wxo)


def _ffn_kernel(x_ref, g_ref, wgu_ref, wd_ref, gf_ref, o_ref, *, final_norm):
    d_ff = wd_ref.shape[0]
    x = x_ref[...]
    h = _rms(x, g_ref[...]).astype(BF16)
    y = x
    for c0 in range(0, d_ff, FFN_CHUNK):
        gt = _dot(h, wgu_ref[:, c0:c0 + FFN_CHUNK])
        up = _dot(h, wgu_ref[:, d_ff + c0:d_ff + c0 + FFN_CHUNK])
        act = (gt * _sigmoid(gt) * up).astype(BF16)
        y = y + _dot(act, wd_ref[c0:c0 + FFN_CHUNK, :])
    if final_norm:
        y = _rms(y, gf_ref[...])
    o_ref[...] = y


def _ffn(x2, g, wgu, wd, gf, l, final_norm):
    n_tok, d_model = x2.shape
    t = FFN_TILE
    tok = pl.BlockSpec((t, d_model), lambda i: (i, 0))
    return pl.pallas_call(
        functools.partial(_ffn_kernel, final_norm=final_norm),
        grid=(n_tok // t,),
        in_specs=[tok, _layer_spec(g.shape, l), _layer_spec(wgu.shape, l), _layer_spec(wd.shape, l),
                  _const_spec(gf.shape)],
        out_specs=tok,
        out_shape=jax.ShapeDtypeStruct((n_tok, d_model), F32),
        compiler_params=_params(("parallel",)),
        name="ffn",
    )(x2, g, wgu, wd, gf)


def _in_proj_weights(w_in, d_model):
    c_kpe = SSM_WIDTH + MLA_Q_RANK + MLA_KV_RANK
    kpe = jnp.pad(w_in[:, :, c_kpe:c_kpe + MLA_ROPE], ((0, 0), (0, 0), (MLA_NOPE, HEAD_PAD - MLA_NOPE - MLA_ROPE)))
    cols = [w_in[:, :, :c_kpe], kpe, w_in[:, :, c_kpe + MLA_ROPE:]]
    return jnp.concatenate(cols, axis=2).astype(BF16)


def _mla_weights(w_uq, w_ukv):
    depth, rank, _ = w_uq.shape
    dq = MLA_NOPE + MLA_ROPE
    wq = w_uq.reshape(depth, rank, MLA_HEADS, dq)
    wuq = jnp.pad(wq, ((0, 0), (0, 0), (0, 0), (0, HEAD_PAD - dq))).reshape(depth, rank, MLA_HEADS * HEAD_PAD)
    kvr = w_ukv.shape[1]
    wkv = w_ukv.reshape(depth, kvr, MLA_HEADS, MLA_NOPE + MLA_V)
    wuk = wkv[..., :MLA_NOPE].reshape(depth, kvr, -1)
    wuvt = wkv[..., MLA_NOPE:].reshape(depth, kvr, -1).transpose(0, 2, 1)
    return wuq.astype(BF16), wuk.astype(BF16), wuvt.astype(BF16)


def _s5_weights(lam_re, lam_im, b_re, b_im, c_re, c_im, log_step):
    depth = lam_re.shape[0]
    step = jnp.exp(log_step)[..., None]
    mag = jnp.exp(lam_re * step)
    lbr = mag * jnp.cos(lam_im * step)
    lbi = mag * jnp.sin(lam_im * step)
    den = lam_re * lam_re + lam_im * lam_im
    cr = ((lbr - 1.0) * lam_re + lbi * lam_im) / den
    ci = (lbi * lam_re - (lbr - 1.0) * lam_im) / den
    bbr = cr[..., None] * b_re - ci[..., None] * b_im
    bbi = cr[..., None] * b_im + ci[..., None] * b_re
    gq = SSM_GROUPS // SSM_QUARTERS
    eye = jnp.eye(gq, dtype=F32)

    def blockdiag_in(m):
        m = m.reshape(depth, SSM_QUARTERS, gq, SSM_STATE, SSM_GROUP_CH)
        return jnp.einsum("ab,lqaph->lqahbp", eye, m).reshape(depth, SSM_QUARTERS, gq * SSM_GROUP_CH,
                                                              gq * SSM_STATE)

    def blockdiag_out(m):
        m = m.reshape(depth, SSM_QUARTERS, gq, SSM_GROUP_CH, SSM_STATE)
        return jnp.einsum("ab,lqahp->lqapbh", eye, m).reshape(depth, SSM_QUARTERS, gq * SSM_STATE,
                                                              gq * SSM_GROUP_CH)

    wb = jnp.concatenate([blockdiag_in(bbr), blockdiag_in(bbi)], axis=3).astype(BF16)
    wc = jnp.concatenate([blockdiag_out(c_re), -blockdiag_out(c_im)], axis=3).astype(BF16)
    lr = lbr.reshape(depth, 1, SSM_CH)
    li = lbi.reshape(depth, 1, SSM_CH)
    a = jnp.broadcast_to(lr, (depth, 8, SSM_CH))
    bsw = jnp.concatenate([jnp.broadcast_to(-li, (depth, 4, SSM_CH)), jnp.broadcast_to(li, (depth, 4, SSM_CH))],
                          axis=1)
    return wb, a, bsw, wc


def kernel(x, mem, positions, norm_mix, w_in, ssm_lam_re, ssm_lam_im, ssm_b_re, ssm_b_im, ssm_c_re, ssm_c_im, ssm_d, ssm_log_step, ssm_w_glu, mla_q_norm, mla_kv_norm, mla_w_uq, mla_w_ukv, mla_w_o, hg_lb, hg_g_norm, hg_w_o, w_out, norm_cross, norm_mem, x_w_q, x_w_kv, x_w_o, norm_ffn, ffn_w_gate_up, ffn_w_down, norm_final):
    bsz, seq, d_model = x.shape
    depth = w_in.shape[0]
    n_mem = mem.shape[1]
    assert bsz == 4, "the s5 scan packs 4 batches x (re, im) on the 8 sublanes"
    n_tok = bsz * seq

    half = MLA_ROPE // 2
    invf = (ROPE_THETA ** (-jnp.arange(half, dtype=F32) / half)).reshape(half, 1)
    lb_p = jax.nn.softmax(hg_lb.astype(F32), axis=0)
    lower_bounds = jnp.cumsum(lb_p, axis=0) - lb_p[0:1]

    rows = lambda a: a.reshape(depth, 1, -1)
    bf = lambda a: a.astype(BF16)
    w_inp = _in_proj_weights(w_in, d_model)
    wuq, wuk, wuvt = _mla_weights(mla_w_uq, mla_w_ukv)
    wb, a, bsw, wc = _s5_weights(ssm_lam_re, ssm_lam_im, ssm_b_re, ssm_b_im, ssm_c_re, ssm_c_im, ssm_log_step)
    wglu, wmo, who, wout = bf(ssm_w_glu), bf(mla_w_o), bf(hg_w_o), bf(w_out)
    wxq, wxkv, wxo, wgu, wdn = bf(x_w_q), bf(x_w_kv), bf(x_w_o), bf(ffn_w_gate_up), bf(ffn_w_down)
    g_mix, g_q, g_kv, g_hg = rows(norm_mix), rows(mla_q_norm), rows(mla_kv_norm), rows(hg_g_norm)
    g_cross, g_mem, g_ffn = rows(norm_cross), rows(norm_mem), rows(norm_ffn)
    lbs, ssm_dr = rows(lower_bounds), rows(ssm_d)

    x2 = x.reshape(n_tok, d_model)
    mem2 = mem.reshape(bsz * n_mem, d_model)
    pos2 = positions.reshape(n_tok // IN_TILE, 1, IN_TILE)

    for l in range(depth):
        u, q, k, v, hq, hf, hi, hg, gates = _in_proj(x2, pos2, g_mix, w_inp, invf, g_q, wuq, g_kv, wuk, wuvt,
                                                     lbs, l, bsz, seq)
        ys = _s5(u.reshape(bsz, seq, SSM_WIDTH), wb, a, bsw, wc, ssm_dr, wglu, l, d_model).reshape(n_tok, d_model)
        om = _mla(q, k, v)
        oh = _hgrn(hq, hf, hi, hg, g_hg, l, bsz, seq)
        kv = _memkv(mem2, g_mem, wxkv, l, n_mem)
        x2 = _mid(x2, ys, om, oh, gates, kv, wmo, who, wout, g_cross, wxq, wxo, l, seq, n_mem)
        x2 = _ffn(x2, g_ffn, wgu, wdn, norm_final.reshape(1, -1), l, final_norm=(l == depth - 1))
    return x2.reshape(bsz, seq, d_model)
```

```python
import functools
import math

import jax
import jax.numpy as jnp
from jax import lax
from jax.experimental import pallas as pl
from jax.experimental.pallas import tpu as pltpu

F32 = jnp.float32
BF16 = jnp.bfloat16

RMS_EPS = 1e-6
LOG2E = 1.4426950408889634

SSM_GROUPS = 32
SSM_GROUP_CH = 16
SSM_STATE = 64
SSM_WIDTH = SSM_GROUPS * SSM_GROUP_CH
SSM_CH = SSM_GROUPS * SSM_STATE
SSM_QUARTERS = 4

MLA_HEADS = 8
MLA_Q_RANK = 512
MLA_KV_RANK = 256
MLA_NOPE = 64
MLA_ROPE = 32
MLA_V = 64
ROPE_THETA = 10000.0
HEAD_PAD = 128
MLA_VT_ROWS = 80

HG_HEADS = 4
HG_D = 128
HG_WIDTH = HG_HEADS * HG_D
HG_CHUNK = 64
HG_SUB = 8
HG_STEP_CHUNKS = 4

X_HEADS = 4
X_HEAD_DIM = 128
X_WIDTH = X_HEADS * X_HEAD_DIM

N_BRANCH = 3

VMEM_LIMIT = 56 * 1024 * 1024

IN_TILE = 512
S5_TIME = 128
MLA_BLOCK = 512
MLA_KEY_BLOCKS = 2
MID_TILE = 512
FFN_TILE = 512
FFN_CHUNK = 1408


def _dot(a, b):
    return jnp.dot(a, b, preferred_element_type=F32)


def _dot_nt(a, b):
    return lax.dot_general(a, b, (((1,), (1,)), ((), ())), preferred_element_type=F32)


def _dot_tn(a, b):
    return lax.dot_general(a, b, (((0,), (0,)), ((), ())), preferred_element_type=F32)


def _rms(x, g):
    return x * lax.rsqrt(jnp.mean(x * x, axis=-1, keepdims=True) + RMS_EPS) * g


def _sigmoid(x):
    return 1.0 / (1.0 + jnp.exp(-x))


def _const_spec(shape):
    nd = len(shape)
    return pl.BlockSpec(shape, lambda *_: (0,) * nd, pipeline_mode=pl.Buffered(1))


def _layer_spec(shape, l):
    nd = len(shape)
    return pl.BlockSpec((None,) + tuple(shape[1:]), lambda *_: (l,) + (0,) * (nd - 1),
                        pipeline_mode=pl.Buffered(1))


def _params(sem):
    return pltpu.CompilerParams(dimension_semantics=sem, vmem_limit_bytes=VMEM_LIMIT)


C_SSM = 0
C_QLAT = C_SSM + SSM_WIDTH
C_KVLAT = C_QLAT + MLA_Q_RANK
C_FRONT = C_KVLAT + MLA_KV_RANK
C_HQ = 0
C_HF = C_HQ + HG_WIDTH
C_HI = C_HF + HG_WIDTH
C_HG = C_HI + HG_WIDTH
C_GATE = C_HG + HG_WIDTH


def _in_proj_kernel(x_ref, pos_ref, g_ref, wf_ref, wkpe_ref, w_ref, invf_ref, qn_ref, wuq_ref, kvn_ref, wuk_ref, wuvt_ref, lb_ref,
                    u_ref, q_ref, k_ref, v_ref, hq_ref, hf_ref, hi_ref, hg_ref, gate_ref, *, q_scale):
    d_model = x_ref.shape[1]
    h = _rms(x_ref[...], g_ref[...]).astype(BF16)

    def seg(a, b):
        return _dot(h, w_ref[:, a:b])

    qlat = _dot(h, wf_ref[:, C_QLAT:C_KVLAT])
    kvlat = _dot(h, wf_ref[:, C_KVLAT:C_FRONT])
    kp = _dot(h, wkpe_ref[...])
    qn = _rms(qlat, qn_ref[...]).astype(BF16)
    kvn = _rms(kvlat, kvn_ref[...]).astype(BF16)

    t = x_ref.shape[0]
    ang = invf_ref[...] * pos_ref[0].astype(F32)
    cos_c = jnp.cos(ang)
    sin_c = jnp.sin(ang)
    fill = lambda v, n: jnp.full((n, t), v, F32)
    tail = HEAD_PAD - MLA_NOPE - MLA_ROPE
    cos = jnp.concatenate([fill(1.0, MLA_NOPE), cos_c, cos_c, fill(1.0, tail)], axis=0).T
    sin = jnp.concatenate([fill(0.0, MLA_NOPE), -sin_c, sin_c, fill(0.0, tail)], axis=0).T
    lane = lax.broadcasted_iota(jnp.int32, (1, HEAD_PAD), 1)
    first_half = lane < MLA_NOPE + MLA_ROPE // 2

    def rope(v):
        swapped = jnp.where(first_half, pltpu.roll(v, HEAD_PAD - MLA_ROPE // 2, 1),
                            pltpu.roll(v, MLA_ROPE // 2, 1))
        return v * cos + swapped * sin

    u_ref[...] = _dot(h, wf_ref[:, C_SSM:C_QLAT])
    hq = seg(C_HQ, C_HF)
    hq_ref[...] = (hq * _sigmoid(hq)).astype(BF16)
    lb = lb_ref[...]
    hf_ref[...] = lb + (1.0 - lb) * _sigmoid(seg(C_HF, C_HI))
    hi_ref[...] = seg(C_HI, C_HG).astype(BF16)
    hg = seg(C_HG, C_GATE)
    hg_ref[...] = (hg * _sigmoid(hg)).astype(BF16)
    for br in range(N_BRANCH):
        gate_ref[:, br * d_model:(br + 1) * d_model] = _sigmoid(
            seg(C_GATE + br * d_model, C_GATE + (br + 1) * d_model)).astype(BF16)

    qa = _dot(qn, wuq_ref[...])
    for hh in range(MLA_HEADS):
        q_ref[0, hh] = (rope(qa[:, hh * HEAD_PAD:(hh + 1) * HEAD_PAD]) * q_scale).astype(BF16)

    kn = _dot(kvn, wuk_ref[...])
    vt = _dot_nt(wuvt_ref[...], kvn)
    kpe = rope(kp)
    nope_lanes = lane < MLA_NOPE
    ones_rows = (lax.broadcasted_iota(jnp.int32, (MLA_VT_ROWS - MLA_V, t), 0) == 0).astype(BF16)
    for hh in range(MLA_HEADS):
        piece = kn[:, (hh // 2) * HEAD_PAD:(hh // 2 + 1) * HEAD_PAD]
        if hh % 2:
            piece = pltpu.roll(piece, MLA_NOPE, 1)
        k_ref[0, hh] = jnp.where(nope_lanes, piece, kpe).astype(BF16)
        v_ref[0, hh, :MLA_V, :] = vt[hh * MLA_V:(hh + 1) * MLA_V].astype(BF16)
        v_ref[0, hh, MLA_V:, :] = ones_rows


def _in_proj(x2, pos2, g, wf, wkpe, w, invf, qn, wuq, kvn, wuk, wuvt, lb, l, bsz, seq):
    n_tok, d_model = x2.shape
    t = IN_TILE
    nt = seq // t
    tok = lambda w_: pl.BlockSpec((t, w_), lambda i: (i, 0))
    lay = lambda a: _layer_spec(a.shape, l)
    head = pl.BlockSpec((1, MLA_HEADS, t, HEAD_PAD), lambda i: (i // nt, 0, i % nt, 0))
    head_shape = jax.ShapeDtypeStruct((bsz, MLA_HEADS, seq, HEAD_PAD), BF16)
    head_t = pl.BlockSpec((1, MLA_HEADS, MLA_VT_ROWS, t), lambda i: (i // nt, 0, 0, i % nt))
    head_t_shape = jax.ShapeDtypeStruct((bsz, MLA_HEADS, MLA_VT_ROWS, seq), BF16)
    q_scale = LOG2E / math.sqrt(MLA_NOPE + MLA_ROPE)
    return pl.pallas_call(
        functools.partial(_in_proj_kernel, q_scale=q_scale),
        grid=(n_tok // t,),
        in_specs=[tok(d_model), pl.BlockSpec((1, 1, t), lambda i: (i, 0, 0)), lay(g), lay(wf), lay(wkpe), lay(w),
                  _const_spec(invf.shape), lay(qn), lay(wuq), lay(kvn), lay(wuk), lay(wuvt), lay(lb)],
        out_specs=[tok(SSM_WIDTH), head, head, head_t, tok(HG_WIDTH), tok(HG_WIDTH), tok(HG_WIDTH), tok(HG_WIDTH),
                   tok(N_BRANCH * d_model)],
        out_shape=[jax.ShapeDtypeStruct((n_tok, SSM_WIDTH), F32), head_shape, head_shape, head_t_shape,
                   jax.ShapeDtypeStruct((n_tok, HG_WIDTH), BF16), jax.ShapeDtypeStruct((n_tok, HG_WIDTH), F32),
                   jax.ShapeDtypeStruct((n_tok, HG_WIDTH), BF16), jax.ShapeDtypeStruct((n_tok, HG_WIDTH), BF16),
                   jax.ShapeDtypeStruct((n_tok, N_BRANCH * d_model), BF16)],
        compiler_params=_params(("parallel",)),
        name="in_proj",
    )(x2, pos2, g, wf, wkpe, w, invf, qn, wuq, kvn, wuk, wuvt, lb)


def _s5_kernel(u_ref, wb_ref, a_ref, bsw_ref, wc_ref, d_ref, wglu_ref, y_ref, se_ref, so_ref, st_ref, utb_ref,
               ytb_ref):
    bsz, tc, _ = u_ref.shape
    rows = bsz * tc
    d_model = y_ref.shape[2]
    qw = SSM_CH // SSM_QUARTERS
    uw = SSM_WIDTH // SSM_QUARTERS
    half = 4

    @pl.when(pl.program_id(0) == 0)
    def _():
        st_ref[...] = jnp.zeros_like(st_ref)

    for q in range(SSM_QUARTERS):
        for b in range(bsz):
            utb_ref[q, pl.ds(b, tc, stride=bsz), :] = u_ref[b, :, q * uw:(q + 1) * uw]

    low = (lax.broadcasted_iota(jnp.int32, (rows, qw), 0) % 8) < half
    for q in range(SSM_QUARTERS):
        o = _dot(utb_ref[q].astype(BF16), wb_ref[q])
        re = o[:, :qw]
        im = o[:, qw:]
        se_ref[:, q * qw:(q + 1) * qw] = jnp.where(low, re, pltpu.roll(im, half, 0))
        so_ref[:, q * qw:(q + 1) * qw] = jnp.where(low, pltpu.roll(re, rows - half, 0), im)

    n_blk = 2
    cw = SSM_CH // n_blk
    for cb in range(n_blk):
        cs = slice(cb * cw, (cb + 1) * cw)
        a = a_ref[:, cs]
        bs = bsw_ref[:, cs]

        def body(v, s, cs=cs, a=a, bs=bs):
            r = pl.multiple_of(v * 8, 8)
            s = a * s + bs * pltpu.roll(s, half, 0) + se_ref[pl.ds(r, 8), cs]
            se_ref[pl.ds(r, 8), cs] = s
            s = a * s + bs * pltpu.roll(s, half, 0) + so_ref[pl.ds(r, 8), cs]
            so_ref[pl.ds(r, 8), cs] = s
            return s

        st_ref[:, cs] = lax.fori_loop(0, rows // 8, body, st_ref[:, cs], unroll=4)

    yw = uw
    low_y = low[:, :yw]
    pieces = []
    for q in range(SSM_QUARTERS):
        oe = _dot(se_ref[:, q * qw:(q + 1) * qw].astype(BF16), wc_ref[q])
        oo = _dot(so_ref[:, q * qw:(q + 1) * qw].astype(BF16), wc_ref[q])
        ye = oe[:, :yw] + pltpu.roll(oe[:, yw:], rows - half, 0)
        yo = pltpu.roll(oo[:, :yw], half, 0) + oo[:, yw:]
        pieces.append(jnp.where(low_y, ye, yo) + d_ref[:, q * uw:(q + 1) * uw] * utb_ref[q])
    y = jax.nn.gelu(jnp.concatenate(pieces, axis=1)).astype(BF16)
    z = _dot(y, wglu_ref[...])
    res = z[:, :d_model] * _sigmoid(z[:, d_model:])
    n_slab = d_model // uw
    for j in range(n_slab):
        ytb_ref[j] = res[:, j * uw:(j + 1) * uw]
    for j in range(n_slab):
        for b in range(bsz):
            y_ref[b, :, j * uw:(j + 1) * uw] = ytb_ref[j, pl.ds(b, tc, stride=bsz), :]


def _s5(u3, wb, a, bsw, wc, d, wglu, l, d_model):
    bsz, seq, _ = u3.shape
    tc = S5_TIME
    rows = tc * bsz
    lanes = SSM_WIDTH // SSM_QUARTERS
    return pl.pallas_call(
        _s5_kernel,
        grid=(seq // tc,),
        in_specs=[pl.BlockSpec((bsz, tc, SSM_WIDTH), lambda i: (0, i, 0))]
        + [_layer_spec(w_.shape, l) for w_ in (wb, a, bsw, wc, d, wglu)],
        out_specs=pl.BlockSpec((bsz, tc, d_model), lambda i: (0, i, 0)),
        out_shape=jax.ShapeDtypeStruct((bsz, seq, d_model), F32),
        scratch_shapes=[pltpu.VMEM((rows, SSM_CH), F32), pltpu.VMEM((rows, SSM_CH), F32),
                        pltpu.VMEM((8, SSM_CH), F32), pltpu.VMEM((SSM_QUARTERS, rows, lanes), F32),
                        pltpu.VMEM((d_model // lanes, rows, lanes), F32)],
        compiler_params=_params(("arbitrary",)),
        name="s5",
    )(u3, wb, a, bsw, wc, d, wglu)


def _mla_kernel(qi_ref, kj_ref, q_ref, k_ref, vt_ref, o_ref, m_ref, acc_ref):
    p = pl.program_id(1)
    qi = qi_ref[p]
    kj = kj_ref[p]
    blk = q_ref.shape[2]

    @pl.when(kj == 0)
    def _():
        m_ref[...] = jnp.full_like(m_ref, -1e30)
        acc_ref[...] = jnp.zeros_like(acc_ref)

    def step(masked, sub):
        if masked:
            key = lax.broadcasted_iota(jnp.int32, (blk, blk), 0)
            qry = lax.broadcasted_iota(jnp.int32, (blk, blk), 1)
            causal = key <= qry
        sc, mx = {}, {}
        for t in range(MLA_HEADS + 2):
            if t < MLA_HEADS:
                s = _dot_nt(k_ref[0, t, sub * blk:(sub + 1) * blk, :], q_ref[0, t])
                sc[t] = jnp.where(causal, s, -1e30) if masked else s
            if 1 <= t <= MLA_HEADS:
                hh = t - 1
                m_prev = m_ref[hh]
                m_new = jnp.maximum(m_prev, jnp.max(sc[hh], axis=0, keepdims=True))
                mx[hh] = (m_new, jnp.exp2(m_prev - m_new))
                m_ref[hh] = m_new
            if t >= 2:
                hh = t - 2
                m_new, alpha = mx.pop(hh)
                pr = jnp.exp2(sc.pop(hh) - m_new).astype(BF16)
                acc_ref[hh] = alpha * acc_ref[hh] + _dot(vt_ref[0, hh, :, sub * blk:(sub + 1) * blk], pr)

    for sub in range(MLA_KEY_BLOCKS):
        kb = kj * MLA_KEY_BLOCKS + sub

        @pl.when(kb < qi)
        def _(sub=sub):
            step(False, sub)

        @pl.when(kb == qi)
        def _(sub=sub):
            step(True, sub)
            for hh in range(MLA_HEADS):
                acc = acc_ref[hh]
                o_ref[0, hh] = (acc[:MLA_V, :] / acc[MLA_V:MLA_V + 1, :]).astype(BF16)


def _mla(q, k, vt):
    bsz, nh, seq, hd = q.shape
    blk = MLA_BLOCK
    nb = seq // blk
    pairs = [(i, j) for i in range(nb) for j in range(i // MLA_KEY_BLOCKS + 1)]
    qi = jnp.asarray([p[0] for p in pairs], jnp.int32)
    kj = jnp.asarray([p[1] for p in pairs], jnp.int32)
    qspec = pl.BlockSpec((1, nh, blk, hd), lambda b, p, qi, kj: (b, 0, qi[p], 0))
    kspec = pl.BlockSpec((1, nh, blk * MLA_KEY_BLOCKS, hd), lambda b, p, qi, kj: (b, 0, kj[p], 0))
    vspec = pl.BlockSpec((1, nh, MLA_VT_ROWS, blk * MLA_KEY_BLOCKS), lambda b, p, qi, kj: (b, 0, 0, kj[p]))
    ospec = pl.BlockSpec((1, nh, MLA_V, blk), lambda b, p, qi, kj: (b, 0, 0, qi[p]))
    return pl.pallas_call(
        _mla_kernel,
        grid_spec=pltpu.PrefetchScalarGridSpec(
            num_scalar_prefetch=2,
            grid=(bsz, len(pairs)),
            in_specs=[qspec, kspec, vspec],
            out_specs=ospec,
            scratch_shapes=[pltpu.VMEM((nh, 1, blk), F32), pltpu.VMEM((nh, MLA_VT_ROWS, blk), F32)],
        ),
        out_shape=jax.ShapeDtypeStruct((bsz, nh, MLA_V, seq), BF16),
        compiler_params=_params(("parallel", "arbitrary")),
        name="mla",
    )(qi, kj, q, k, vt)


def _hgrn_kernel(q_ref, f_ref, v_ref, g_ref, gn_ref, o_ref, st_ref, ks_ref):
    c = HG_CHUNK
    sub = HG_SUB
    n_sub = c // sub

    @pl.when(pl.program_id(1) == 0)
    def _():
        st_ref[...] = jnp.zeros_like(st_ref)
        ks_ref[...] = jnp.zeros_like(ks_ref)

    chains = [(hh, ci) for hh in range(HG_HEADS) for ci in range(HG_STEP_CHUNKS)]

    def blk(ref, ch):
        hh, ci = ch
        return ref[ci * c:(ci + 1) * c, hh * HG_D:(hh + 1) * HG_D]

    row = lax.broadcasted_iota(jnp.int32, (c, c), 0)
    col = lax.broadcasted_iota(jnp.int32, (c, c), 1)
    tri = (col <= row).astype(BF16)
    lane = lax.broadcasted_iota(jnp.int32, (sub, HG_D), 1)
    trow = lax.broadcasted_iota(jnp.int32, (sub, HG_D), 0)

    q, k, b, bk = {}, {}, {}, {}
    for ch in chains:
        f = blk(f_ref, ch)
        k[ch] = 1.0 - f
        lf = jnp.log(f) * LOG2E
        l1 = lf.astype(BF16)
        r1 = lf - l1.astype(F32)
        l2 = r1.astype(BF16)
        l3 = (r1 - l2.astype(F32)).astype(BF16)
        b[ch] = _dot(tri, l1) + _dot(tri, l2) + _dot(tri, l3)
    for ch in chains:
        q[ch] = blk(q_ref, ch).astype(F32)
        bk[ch] = b[ch] - jnp.log(k[ch]) * LOG2E

    qb, kdec, ebl, o_inter = {}, {}, {}, {}
    for ch in chains:
        bl = b[ch][c - 1:c, :]
        qb[ch] = (q[ch] * jnp.exp2(b[ch])).astype(BF16)
        kdec[ch] = (k[ch] * jnp.exp2(bl - b[ch])).astype(BF16)
        ebl[ch] = jnp.exp2(bl)
    for hh in range(HG_HEADS):
        st = st_ref[hh]
        for ci in range(HG_STEP_CHUNKS):
            ch = (hh, ci)
            o_inter[ch] = _dot_nt(qb[ch], st.astype(BF16))
            st = st * ebl[ch] + _dot_tn(blk(v_ref, ch), kdec[ch])
        st_ref[hh] = st

    a_rows = {ch: [] for ch in chains}
    for i in range(n_sub):
        rs = slice(i * sub, (i + 1) * sub)
        tiles = {ch: jnp.zeros((sub, HG_D), F32) for ch in chains}
        for ss in range(sub):
            s = i * sub + ss
            for ch in chains:
                kdec_s = jnp.exp2(jnp.minimum(b[ch][rs] - bk[ch][s:s + 1, :], 0.0))
                a_s = jnp.sum(q[ch][rs] * kdec_s, axis=-1, keepdims=True)
                tiles[ch] = jnp.where(lane == s, a_s, tiles[ch])
        for ch in chains:
            a_rows[ch].append(jnp.where(trow + i * sub >= lane, tiles[ch], 0.0)[:, :c])
    for i in range(1, n_sub):
        rs = slice(i * sub, (i + 1) * sub)
        for n, ch in enumerate(chains):
            r = b[ch][i * sub - 1:i * sub, :]
            ks_ref[n, i - 1, :i * sub, :] = k[ch][:i * sub] * jnp.exp2(r - b[ch][:i * sub])
            qs = (q[ch][rs] * jnp.exp2(b[ch][rs] - r)).astype(BF16)
            a_rows[ch][i] = a_rows[ch][i] + _dot_nt(qs, ks_ref[n, i - 1].astype(BF16))
    o = {}
    for ch in chains:
        a = jnp.concatenate(a_rows[ch], axis=0).astype(BF16)
        o[ch] = o_inter[ch] + _dot(a, blk(v_ref, ch))

    gn = gn_ref[...]
    for ch in chains:
        hh, ci = ch
        on = o[ch] * lax.rsqrt(jnp.mean(o[ch] * o[ch], axis=-1, keepdims=True) + RMS_EPS) * gn
        o_ref[ci * c:(ci + 1) * c, hh * HG_D:(hh + 1) * HG_D] = (on * blk(g_ref, ch).astype(F32)).astype(BF16)


def _hgrn(hq, hf, hi, hg, gn, l, bsz, seq):
    n_tok = hq.shape[0]
    c = HG_CHUNK
    rows = c * HG_STEP_CHUNKS
    nc = seq // rows
    spec = pl.BlockSpec((rows, HG_WIDTH), lambda b, i: (b * nc + i, 0))
    return pl.pallas_call(
        _hgrn_kernel,
        grid=(bsz, nc),
        in_specs=[spec, spec, spec, spec, _layer_spec(gn.shape, l)],
        out_specs=spec,
        out_shape=jax.ShapeDtypeStruct((n_tok, HG_WIDTH), BF16),
        scratch_shapes=[pltpu.VMEM((HG_HEADS, HG_D, HG_D), F32),
                        pltpu.VMEM((HG_HEADS * HG_STEP_CHUNKS, c // HG_SUB - 1, c, HG_D), F32)],
        compiler_params=_params(("parallel", "arbitrary")),
        name="hgrn",
    )(hq, hf, hi, hg, gn)


def _memkv_kernel(mem_ref, g_ref, w_ref, kv_ref):
    kv_ref[...] = _dot(_rms(mem_ref[...], g_ref[...]).astype(BF16), w_ref[...]).astype(BF16)


def _memkv(mem2, g, w, l, n_mem):
    n_rows, d_model = mem2.shape
    return pl.pallas_call(
        _memkv_kernel,
        grid=(n_rows // n_mem,),
        in_specs=[pl.BlockSpec((n_mem, d_model), lambda i: (i, 0)), _layer_spec(g.shape, l), _layer_spec(w.shape, l)],
        out_specs=pl.BlockSpec((n_mem, 2 * X_WIDTH), lambda i: (i, 0)),
        out_shape=jax.ShapeDtypeStruct((n_rows, 2 * X_WIDTH), BF16),
        compiler_params=_params(("parallel",)),
        name="memkv",
    )(mem2, g, w)


def _mid_kernel(x_ref, ys_ref, om_ref, oh_ref, gate_ref, kv_ref, wmo_ref, who_ref, wout_ref, gx_ref, wq_ref,
                wxo_ref, o_ref, *, x_scale):
    d_model = x_ref.shape[1]
    om_t = om_ref[0].reshape(MLA_HEADS * MLA_V, x_ref.shape[0])
    y_mla = _dot_tn(om_t, wmo_ref[...])
    y_hg = _dot(oh_ref[...], who_ref[...])
    gate = lambda br: gate_ref[:, br * d_model:(br + 1) * d_model].astype(F32)
    merged = gate(0) * ys_ref[...] + gate(1) * y_mla + gate(2) * y_hg
    x = x_ref[...] + _dot(merged.astype(BF16), wout_ref[...])

    q = (_dot(_rms(x, gx_ref[...]).astype(BF16), wq_ref[...]) * x_scale).astype(BF16)
    outs = []
    for hh in range(X_HEADS):
        sl = slice(hh * X_HEAD_DIM, (hh + 1) * X_HEAD_DIM)
        s = _dot_nt(q[:, sl], kv_ref[:, sl])
        pr = jnp.exp2(s - jnp.max(s, axis=-1, keepdims=True))
        den = jnp.sum(pr, axis=-1, keepdims=True)
        vh = kv_ref[:, X_WIDTH + hh * X_HEAD_DIM:X_WIDTH + (hh + 1) * X_HEAD_DIM]
        outs.append((_dot(pr.astype(BF16), vh) / den).astype(BF16))
    o_ref[...] = x + _dot(jnp.concatenate(outs, axis=1), wxo_ref[...])


def _mid(x2, ys, om, oh, gates, kv, wmo, who, wout, gx, wq, wxo, l, seq, n_mem):
    n_tok, d_model = x2.shape
    t = MID_TILE
    nt = seq // t
    tok = lambda w_: pl.BlockSpec((t, w_), lambda i: (i, 0))
    return pl.pallas_call(
        functools.partial(_mid_kernel, x_scale=LOG2E / math.sqrt(X_HEAD_DIM)),
        grid=(n_tok // t,),
        in_specs=[tok(d_model), tok(d_model),
                  pl.BlockSpec((1, MLA_HEADS, MLA_V, t), lambda i: (i // nt, 0, 0, i % nt)),
                  tok(HG_WIDTH), tok(N_BRANCH * d_model),
                  pl.BlockSpec((n_mem, 2 * X_WIDTH), lambda i: (i // nt, 0)),
                  ] + [_layer_spec(w_.shape, l) for w_ in (wmo, who, wout, gx, wq, wxo)],
        out_specs=tok(d_model),
        out_shape=jax.ShapeDtypeStruct((n_tok, d_model), F32),
        compiler_params=_params(("parallel",)),
        name="mid",
    )(x2, ys, om, oh, gates, kv, wmo, who, wout, gx, wq, wxo)


def _ffn_kernel(x_ref, g_ref, wgu_ref, wd_ref, gf_ref, o_ref, *, final_norm):
    d_ff = wd_ref.shape[0]
    x = x_ref[...]
    h = _rms(x, g_ref[...]).astype(BF16)
    y = x
    for c0 in range(0, d_ff, FFN_CHUNK):
        gt = _dot(h, wgu_ref[:, c0:c0 + FFN_CHUNK])
        up = _dot(h, wgu_ref[:, d_ff + c0:d_ff + c0 + FFN_CHUNK])
        act = (gt * _sigmoid(gt) * up).astype(BF16)
        y = y + _dot(act, wd_ref[c0:c0 + FFN_CHUNK, :])
    if final_norm:
        y = _rms(y, gf_ref[...])
    o_ref[...] = y


def _ffn(x2, g, wgu, wd, gf, l, final_norm):
    n_tok, d_model = x2.shape
    t = FFN_TILE
    tok = pl.BlockSpec((t, d_model), lambda i: (i, 0))
    return pl.pallas_call(
        functools.partial(_ffn_kernel, final_norm=final_norm),
        grid=(n_tok // t,),
        in_specs=[tok, _layer_spec(g.shape, l), _layer_spec(wgu.shape, l), _layer_spec(wd.shape, l),
                  _const_spec(gf.shape)],
        out_specs=tok,
        out_shape=jax.ShapeDtypeStruct((n_tok, d_model), F32),
        compiler_params=_params(("parallel",)),
        name="ffn",
    )(x2, g, wgu, wd, gf)


def _in_proj_weights(w_in):
    kpe = jnp.pad(w_in[:, :, C_FRONT:C_FRONT + MLA_ROPE], ((0, 0), (0, 0), (MLA_NOPE, HEAD_PAD - MLA_NOPE - MLA_ROPE)))
    return w_in[:, :, :C_FRONT].astype(BF16), kpe.astype(BF16), w_in[:, :, C_FRONT + MLA_ROPE:].astype(BF16)


def _mla_weights(w_uq, w_ukv):
    depth, rank, _ = w_uq.shape
    dq = MLA_NOPE + MLA_ROPE
    wq = w_uq.reshape(depth, rank, MLA_HEADS, dq)
    wuq = jnp.pad(wq, ((0, 0), (0, 0), (0, 0), (0, HEAD_PAD - dq))).reshape(depth, rank, MLA_HEADS * HEAD_PAD)
    kvr = w_ukv.shape[1]
    wkv = w_ukv.reshape(depth, kvr, MLA_HEADS, MLA_NOPE + MLA_V)
    wuk = wkv[..., :MLA_NOPE].reshape(depth, kvr, -1)
    wuvt = wkv[..., MLA_NOPE:].reshape(depth, kvr, -1).transpose(0, 2, 1)
    return wuq.astype(BF16), wuk.astype(BF16), wuvt.astype(BF16)


def _s5_weights(lam_re, lam_im, b_re, b_im, c_re, c_im, log_step):
    depth = lam_re.shape[0]
    step = jnp.exp(log_step)[..., None]
    mag = jnp.exp(lam_re * step)
    lbr = mag * jnp.cos(lam_im * step)
    lbi = mag * jnp.sin(lam_im * step)
    den = lam_re * lam_re + lam_im * lam_im
    cr = ((lbr - 1.0) * lam_re + lbi * lam_im) / den
    ci = (lbi * lam_re - (lbr - 1.0) * lam_im) / den
    bbr = cr[..., None] * b_re - ci[..., None] * b_im
    bbi = cr[..., None] * b_im + ci[..., None] * b_re
    gq = SSM_GROUPS // SSM_QUARTERS
    eye = jnp.eye(gq, dtype=F32)

    def blockdiag(m):
        m = m.reshape(depth, SSM_QUARTERS, gq, m.shape[2], m.shape[3]).transpose(0, 1, 2, 4, 3)
        out = m[:, :, :, :, None, :] * eye[None, None, :, None, :, None]
        return out.reshape(depth, SSM_QUARTERS, gq * m.shape[3], gq * m.shape[4])

    wb = jnp.concatenate([blockdiag(bbr), blockdiag(bbi)], axis=3).astype(BF16)
    wc = jnp.concatenate([blockdiag(c_re), -blockdiag(c_im)], axis=3).astype(BF16)
    lr = lbr.reshape(depth, 1, SSM_CH)
    li = lbi.reshape(depth, 1, SSM_CH)
    a = jnp.broadcast_to(lr, (depth, 8, SSM_CH))
    bsw = jnp.concatenate([jnp.broadcast_to(-li, (depth, 4, SSM_CH)), jnp.broadcast_to(li, (depth, 4, SSM_CH))],
                          axis=1)
    return wb, a, bsw, wc


def kernel(x, mem, positions, norm_mix, w_in, ssm_lam_re, ssm_lam_im, ssm_b_re, ssm_b_im, ssm_c_re, ssm_c_im, ssm_d, ssm_log_step, ssm_w_glu, mla_q_norm, mla_kv_norm, mla_w_uq, mla_w_ukv, mla_w_o, hg_lb, hg_g_norm, hg_w_o, w_out, norm_cross, norm_mem, x_w_q, x_w_kv, x_w_o, norm_ffn, ffn_w_gate_up, ffn_w_down, norm_final):
    bsz, seq, d_model = x.shape
    depth = w_in.shape[0]
    n_mem = mem.shape[1]
    assert bsz == 4, "the s5 scan packs 4 batches x (re, im) on the 8 sublanes"
    n_tok = bsz * seq

    half = MLA_ROPE // 2
    invf = (ROPE_THETA ** (-jnp.arange(half, dtype=F32) / half)).reshape(half, 1)
    lb_p = jax.nn.softmax(hg_lb.astype(F32), axis=0)
    lower_bounds = jnp.cumsum(lb_p, axis=0) - lb_p[0:1]

    rows = lambda a: a.reshape(depth, 1, -1)
    bf = lambda a: a.astype(BF16)
    w_front, w_kpe, w_back = _in_proj_weights(w_in)
    wuq, wuk, wuvt = _mla_weights(mla_w_uq, mla_w_ukv)
    wb, a, bsw, wc = _s5_weights(ssm_lam_re, ssm_lam_im, ssm_b_re, ssm_b_im, ssm_c_re, ssm_c_im, ssm_log_step)
    wglu, wmo, who, wout = bf(ssm_w_glu), bf(mla_w_o), bf(hg_w_o), bf(w_out)
    wxq, wxkv, wxo, wgu, wdn = bf(x_w_q), bf(x_w_kv), bf(x_w_o), bf(ffn_w_gate_up), bf(ffn_w_down)
    g_mix, g_q, g_kv, g_hg = rows(norm_mix), rows(mla_q_norm), rows(mla_kv_norm), rows(hg_g_norm)
    g_cross, g_mem, g_ffn = rows(norm_cross), rows(norm_mem), rows(norm_ffn)
    lbs, ssm_dr = rows(lower_bounds), rows(ssm_d)

    x2 = x.reshape(n_tok, d_model)
    mem2 = mem.reshape(bsz * n_mem, d_model)
    pos2 = positions.reshape(n_tok // IN_TILE, 1, IN_TILE)

    for l in range(depth):
        u, q, k, v, hq, hf, hi, hg, gates = _in_proj(x2, pos2, g_mix, w_front, w_kpe, w_back, invf, g_q, wuq, g_kv, wuk, wuvt,
                                                     lbs, l, bsz, seq)
        ys = _s5(u.reshape(bsz, seq, SSM_WIDTH), wb, a, bsw, wc, ssm_dr, wglu, l, d_model).reshape(n_tok, d_model)
        om = _mla(q, k, v)
        oh = _hgrn(hq, hf, hi, hg, g_hg, l, bsz, seq)
        kv = _memkv(mem2, g_mem, wxkv, l, n_mem)
        x2 = _mid(x2, ys, om, oh, gates, kv, wmo, who, wout, g_cross, wxq, wxo, l, seq, n_mem)
        x2 = _ffn(x2, g_ffn, wgu, wdn, norm_final.reshape(1, -1), l, final_norm=(l == depth - 1))
    return x2.reshape(bsz, seq, d_model)
```

```python
import functools
import math

import jax
import jax.numpy as jnp
from jax import lax
from jax.experimental import pallas as pl
from jax.experimental.pallas import tpu as pltpu

F32 = jnp.float32
BF16 = jnp.bfloat16

RMS_EPS = 1e-6
LOG2E = 1.4426950408889634

SSM_GROUPS = 32
SSM_GROUP_CH = 16
SSM_STATE = 64
SSM_WIDTH = SSM_GROUPS * SSM_GROUP_CH
SSM_CH = SSM_GROUPS * SSM_STATE
SSM_QUARTERS = 4

MLA_HEADS = 8
MLA_Q_RANK = 512
MLA_KV_RANK = 256
MLA_NOPE = 64
MLA_ROPE = 32
MLA_V = 64
ROPE_THETA = 10000.0
HEAD_PAD = 128
MLA_VT_ROWS = 80

HG_HEADS = 4
HG_D = 128
HG_WIDTH = HG_HEADS * HG_D
HG_CHUNK = 64
HG_SUB = 8
HG_STEP_CHUNKS = 8

X_HEADS = 4
X_HEAD_DIM = 128
X_WIDTH = X_HEADS * X_HEAD_DIM

N_BRANCH = 3

VMEM_LIMIT = 56 * 1024 * 1024

IN_TILE = 512
S5_TIME = 128
MLA_BLOCK = 512
MLA_KEY_BLOCKS = 1
MID_TILE = 512
FFN_TILE = 512
FFN_CHUNK = 2816


def _dot(a, b):
    return jnp.dot(a, b, preferred_element_type=F32)


def _dot_nt(a, b):
    return lax.dot_general(a, b, (((1,), (1,)), ((), ())), preferred_element_type=F32)


def _dot_tn(a, b):
    return lax.dot_general(a, b, (((0,), (0,)), ((), ())), preferred_element_type=F32)


def _rms(x, g):
    return x * lax.rsqrt(jnp.mean(x * x, axis=-1, keepdims=True) + RMS_EPS) * g


def _sigmoid(x):
    return 1.0 / (1.0 + jnp.exp(-x))


def _const_spec(shape):
    nd = len(shape)
    return pl.BlockSpec(shape, lambda *_: (0,) * nd, pipeline_mode=pl.Buffered(1))


def _layer_spec(shape, l):
    nd = len(shape)
    return pl.BlockSpec((None,) + tuple(shape[1:]), lambda *_: (l,) + (0,) * (nd - 1),
                        pipeline_mode=pl.Buffered(1))


def _params(sem):
    return pltpu.CompilerParams(dimension_semantics=sem, vmem_limit_bytes=VMEM_LIMIT)


C_SSM = 0
C_QLAT = C_SSM + SSM_WIDTH
C_KVLAT = C_QLAT + MLA_Q_RANK
C_KPE = C_KVLAT + MLA_KV_RANK
C_HQ = C_KPE + HEAD_PAD
C_HF = C_HQ + HG_WIDTH
C_HI = C_HF + HG_WIDTH
C_HG = C_HI + HG_WIDTH
C_GATE = C_HG + HG_WIDTH


def _in_proj_kernel(x_ref, pos_ref, g_ref, w_ref, invf_ref, qn_ref, wuq_ref, kvn_ref, wuk_ref, wuvt_ref, lb_ref,
                    u_ref, q_ref, k_ref, v_ref, hq_ref, hf_ref, hi_ref, hg_ref, gate_ref, *, q_scale):
    d_model = x_ref.shape[1]
    h = _rms(x_ref[...], g_ref[...]).astype(BF16)

    def seg(a, b):
        return _dot(h, w_ref[:, a:b])

    qlat = seg(C_QLAT, C_KVLAT)
    kvlat = seg(C_KVLAT, C_KPE)
    kp = seg(C_KPE, C_HQ)
    qn = _rms(qlat, qn_ref[...]).astype(BF16)
    kvn = _rms(kvlat, kvn_ref[...]).astype(BF16)

    t = x_ref.shape[0]
    ang = invf_ref[...] * pos_ref[0].astype(F32)
    cos_c = jnp.cos(ang)
    sin_c = jnp.sin(ang)
    fill = lambda v, n: jnp.full((n, t), v, F32)
    tail = HEAD_PAD - MLA_NOPE - MLA_ROPE
    cos = jnp.concatenate([fill(1.0, MLA_NOPE), cos_c, cos_c, fill(1.0, tail)], axis=0).T
    sin = jnp.concatenate([fill(0.0, MLA_NOPE), -sin_c, sin_c, fill(0.0, tail)], axis=0).T
    lane = lax.broadcasted_iota(jnp.int32, (1, HEAD_PAD), 1)
    first_half = lane < MLA_NOPE + MLA_ROPE // 2

    def rope(v):
        swapped = jnp.where(first_half, pltpu.roll(v, HEAD_PAD - MLA_ROPE // 2, 1),
                            pltpu.roll(v, MLA_ROPE // 2, 1))
        return v * cos + swapped * sin

    u_ref[...] = seg(C_SSM, C_QLAT)
    hq = seg(C_HQ, C_HF)
    hq_ref[...] = (hq * _sigmoid(hq)).astype(BF16)
    lb = lb_ref[...]
    hf_ref[...] = lb + (1.0 - lb) * _sigmoid(seg(C_HF, C_HI))
    hi_ref[...] = seg(C_HI, C_HG).astype(BF16)
    hg = seg(C_HG, C_GATE)
    hg_ref[...] = (hg * _sigmoid(hg)).astype(BF16)
    for br in range(N_BRANCH):
        gate_ref[:, br * d_model:(br + 1) * d_model] = _sigmoid(
            seg(C_GATE + br * d_model, C_GATE + (br + 1) * d_model)).astype(BF16)

    qa = _dot(qn, wuq_ref[...])
    for hh in range(MLA_HEADS):
        q_ref[0, hh] = (rope(qa[:, hh * HEAD_PAD:(hh + 1) * HEAD_PAD]) * q_scale).astype(BF16)

    kn = _dot(kvn, wuk_ref[...])
    vt = _dot_nt(wuvt_ref[...], kvn)
    kpe = rope(kp)
    nope_lanes = lane < MLA_NOPE
    ones_rows = (lax.broadcasted_iota(jnp.int32, (MLA_VT_ROWS - MLA_V, t), 0) == 0).astype(BF16)
    for hh in range(MLA_HEADS):
        piece = kn[:, (hh // 2) * HEAD_PAD:(hh // 2 + 1) * HEAD_PAD]
        if hh % 2:
            piece = pltpu.roll(piece, MLA_NOPE, 1)
        k_ref[0, hh] = jnp.where(nope_lanes, piece, kpe).astype(BF16)
        v_ref[0, hh, :MLA_V, :] = vt[hh * MLA_V:(hh + 1) * MLA_V].astype(BF16)
        v_ref[0, hh, MLA_V:, :] = ones_rows


def _in_proj(x2, pos2, g, w, invf, qn, wuq, kvn, wuk, wuvt, lb, l, bsz, seq):
    n_tok, d_model = x2.shape
    t = IN_TILE
    nt = seq // t
    tok = lambda w_: pl.BlockSpec((t, w_), lambda i: (i, 0))
    lay = lambda a: _layer_spec(a.shape, l)
    head = pl.BlockSpec((1, MLA_HEADS, t, HEAD_PAD), lambda i: (i // nt, 0, i % nt, 0))
    head_shape = jax.ShapeDtypeStruct((bsz, MLA_HEADS, seq, HEAD_PAD), BF16)
    head_t = pl.BlockSpec((1, MLA_HEADS, MLA_VT_ROWS, t), lambda i: (i // nt, 0, 0, i % nt))
    head_t_shape = jax.ShapeDtypeStruct((bsz, MLA_HEADS, MLA_VT_ROWS, seq), BF16)
    q_scale = LOG2E / math.sqrt(MLA_NOPE + MLA_ROPE)
    return pl.pallas_call(
        functools.partial(_in_proj_kernel, q_scale=q_scale),
        grid=(n_tok // t,),
        in_specs=[tok(d_model), pl.BlockSpec((1, 1, t), lambda i: (i, 0, 0)), lay(g), lay(w),
                  _const_spec(invf.shape), lay(qn), lay(wuq), lay(kvn), lay(wuk), lay(wuvt), lay(lb)],
        out_specs=[tok(SSM_WIDTH), head, head, head_t, tok(HG_WIDTH), tok(HG_WIDTH), tok(HG_WIDTH), tok(HG_WIDTH),
                   tok(N_BRANCH * d_model)],
        out_shape=[jax.ShapeDtypeStruct((n_tok, SSM_WIDTH), F32), head_shape, head_shape, head_t_shape,
                   jax.ShapeDtypeStruct((n_tok, HG_WIDTH), BF16), jax.ShapeDtypeStruct((n_tok, HG_WIDTH), F32),
                   jax.ShapeDtypeStruct((n_tok, HG_WIDTH), BF16), jax.ShapeDtypeStruct((n_tok, HG_WIDTH), BF16),
                   jax.ShapeDtypeStruct((n_tok, N_BRANCH * d_model), BF16)],
        compiler_params=_params(("parallel",)),
        name="in_proj",
    )(x2, pos2, g, w, invf, qn, wuq, kvn, wuk, wuvt, lb)


def _s5_kernel(u_ref, wb_ref, a_ref, bsw_ref, wc_ref, d_ref, wglu_ref, y_ref, se_ref, so_ref, st_ref, utb_ref,
               ytb_ref):
    bsz, tc, _ = u_ref.shape
    rows = bsz * tc
    d_model = y_ref.shape[2]
    qw = SSM_CH // SSM_QUARTERS
    uw = SSM_WIDTH // SSM_QUARTERS
    half = 4

    @pl.when(pl.program_id(0) == 0)
    def _():
        st_ref[...] = jnp.zeros_like(st_ref)

    for q in range(SSM_QUARTERS):
        for b in range(bsz):
            utb_ref[q, pl.ds(b, tc, stride=bsz), :] = u_ref[b, :, q * uw:(q + 1) * uw]

    low = (lax.broadcasted_iota(jnp.int32, (rows, qw), 0) % 8) < half
    for q in range(SSM_QUARTERS):
        o = _dot(utb_ref[q].astype(BF16), wb_ref[q])
        re = o[:, :qw]
        im = o[:, qw:]
        se_ref[:, q * qw:(q + 1) * qw] = jnp.where(low, re, pltpu.roll(im, half, 0))
        so_ref[:, q * qw:(q + 1) * qw] = jnp.where(low, pltpu.roll(re, rows - half, 0), im)

    n_blk = 2
    cw = SSM_CH // n_blk
    for cb in range(n_blk):
        cs = slice(cb * cw, (cb + 1) * cw)
        a = a_ref[:, cs]
        bs = bsw_ref[:, cs]

        def body(v, s, cs=cs, a=a, bs=bs):
            r = pl.multiple_of(v * 8, 8)
            s = a * s + bs * pltpu.roll(s, half, 0) + se_ref[pl.ds(r, 8), cs]
            se_ref[pl.ds(r, 8), cs] = s
            s = a * s + bs * pltpu.roll(s, half, 0) + so_ref[pl.ds(r, 8), cs]
            so_ref[pl.ds(r, 8), cs] = s
            return s

        st_ref[:, cs] = lax.fori_loop(0, rows // 8, body, st_ref[:, cs], unroll=4)

    yw = uw
    low_y = low[:, :yw]
    pieces = []
    for q in range(SSM_QUARTERS):
        oe = _dot(se_ref[:, q * qw:(q + 1) * qw].astype(BF16), wc_ref[q])
        oo = _dot(so_ref[:, q * qw:(q + 1) * qw].astype(BF16), wc_ref[q])
        ye = oe[:, :yw] + pltpu.roll(oe[:, yw:], rows - half, 0)
        yo = pltpu.roll(oo[:, :yw], half, 0) + oo[:, yw:]
        pieces.append(jnp.where(low_y, ye, yo) + d_ref[:, q * uw:(q + 1) * uw] * utb_ref[q])
    y = jax.nn.gelu(jnp.concatenate(pieces, axis=1)).astype(BF16)
    z = _dot(y, wglu_ref[...])
    res = z[:, :d_model] * _sigmoid(z[:, d_model:])
    n_slab = d_model // uw
    for j in range(n_slab):
        ytb_ref[j] = res[:, j * uw:(j + 1) * uw]
    for j in range(n_slab):
        for b in range(bsz):
            y_ref[b, :, j * uw:(j + 1) * uw] = ytb_ref[j, pl.ds(b, tc, stride=bsz), :]


def _s5(u3, wb, a, bsw, wc, d, wglu, l, d_model):
    bsz, seq, _ = u3.shape
    tc = S5_TIME
    rows = tc * bsz
    lanes = SSM_WIDTH // SSM_QUARTERS
    return pl.pallas_call(
        _s5_kernel,
        grid=(seq // tc,),
        in_specs=[pl.BlockSpec((bsz, tc, SSM_WIDTH), lambda i: (0, i, 0))]
        + [_layer_spec(w_.shape, l) for w_ in (wb, a, bsw, wc, d, wglu)],
        out_specs=pl.BlockSpec((bsz, tc, d_model), lambda i: (0, i, 0)),
        out_shape=jax.ShapeDtypeStruct((bsz, seq, d_model), F32),
        scratch_shapes=[pltpu.VMEM((rows, SSM_CH), F32), pltpu.VMEM((rows, SSM_CH), F32),
                        pltpu.VMEM((8, SSM_CH), F32), pltpu.VMEM((SSM_QUARTERS, rows, lanes), F32),
                        pltpu.VMEM((d_model // lanes, rows, lanes), F32)],
        compiler_params=_params(("arbitrary",)),
        name="s5",
    )(u3, wb, a, bsw, wc, d, wglu)


def _mla_kernel(qi_ref, kj_ref, q_ref, k_ref, vt_ref, o_ref, m_ref, acc_ref):
    p = pl.program_id(1)
    qi = qi_ref[p]
    kj = kj_ref[p]
    blk = q_ref.shape[2]

    @pl.when(kj == 0)
    def _():
        m_ref[...] = jnp.full_like(m_ref, -1e30)
        acc_ref[...] = jnp.zeros_like(acc_ref)

    def step(masked, sub):
        if masked:
            key = lax.broadcasted_iota(jnp.int32, (blk, blk), 0)
            qry = lax.broadcasted_iota(jnp.int32, (blk, blk), 1)
            causal = key <= qry
        sc, mx = {}, {}
        for t in range(MLA_HEADS + 2):
            if t < MLA_HEADS:
                s = _dot_nt(k_ref[0, t, sub * blk:(sub + 1) * blk, :], q_ref[0, t])
                sc[t] = jnp.where(causal, s, -1e30) if masked else s
            if 1 <= t <= MLA_HEADS:
                hh = t - 1
                m_prev = m_ref[hh]
                m_new = jnp.maximum(m_prev, jnp.max(sc[hh], axis=0, keepdims=True))
                mx[hh] = (m_new, jnp.exp2(m_prev - m_new))
                m_ref[hh] = m_new
            if t >= 2:
                hh = t - 2
                m_new, alpha = mx.pop(hh)
                pr = jnp.exp2(sc.pop(hh) - m_new).astype(BF16)
                acc_ref[hh] = alpha * acc_ref[hh] + _dot(vt_ref[0, hh, :, sub * blk:(sub + 1) * blk], pr)

    for sub in range(MLA_KEY_BLOCKS):
        kb = kj * MLA_KEY_BLOCKS + sub

        @pl.when(kb < qi)
        def _(sub=sub):
            step(False, sub)

        @pl.when(kb == qi)
        def _(sub=sub):
            step(True, sub)
            for hh in range(MLA_HEADS):
                acc = acc_ref[hh]
                o_ref[0, hh] = (acc[:MLA_V, :] / acc[MLA_V:MLA_V + 1, :]).astype(BF16)


def _mla(q, k, vt):
    bsz, nh, seq, hd = q.shape
    blk = MLA_BLOCK
    nb = seq // blk
    pairs = [(i, j) for i in range(nb) for j in range(i // MLA_KEY_BLOCKS + 1)]
    qi = jnp.asarray([p[0] for p in pairs], jnp.int32)
    kj = jnp.asarray([p[1] for p in pairs], jnp.int32)
    qspec = pl.BlockSpec((1, nh, blk, hd), lambda b, p, qi, kj: (b, 0, qi[p], 0))
    kspec = pl.BlockSpec((1, nh, blk * MLA_KEY_BLOCKS, hd), lambda b, p, qi, kj: (b, 0, kj[p], 0))
    vspec = pl.BlockSpec((1, nh, MLA_VT_ROWS, blk * MLA_KEY_BLOCKS), lambda b, p, qi, kj: (b, 0, 0, kj[p]))
    ospec = pl.BlockSpec((1, nh, MLA_V, blk), lambda b, p, qi, kj: (b, 0, 0, qi[p]))
    return pl.pallas_call(
        _mla_kernel,
        grid_spec=pltpu.PrefetchScalarGridSpec(
            num_scalar_prefetch=2,
            grid=(bsz, len(pairs)),
            in_specs=[qspec, kspec, vspec],
            out_specs=ospec,
            scratch_shapes=[pltpu.VMEM((nh, 1, blk), F32), pltpu.VMEM((nh, MLA_VT_ROWS, blk), F32)],
        ),
        out_shape=jax.ShapeDtypeStruct((bsz, nh, MLA_V, seq), BF16),
        compiler_params=_params(("parallel", "arbitrary")),
        name="mla",
    )(qi, kj, q, k, vt)


def _hgrn_kernel(q_ref, f_ref, v_ref, g_ref, gn_ref, o_ref, st_ref, ks_ref):
    c = HG_CHUNK
    sub = HG_SUB
    n_sub = c // sub

    @pl.when(pl.program_id(1) == 0)
    def _():
        st_ref[...] = jnp.zeros_like(st_ref)
        ks_ref[...] = jnp.zeros_like(ks_ref)

    chains = [(hh, ci) for hh in range(HG_HEADS) for ci in range(HG_STEP_CHUNKS)]

    def blk(ref, ch):
        hh, ci = ch
        return ref[ci * c:(ci + 1) * c, hh * HG_D:(hh + 1) * HG_D]

    row = lax.broadcasted_iota(jnp.int32, (c, c), 0)
    col = lax.broadcasted_iota(jnp.int32, (c, c), 1)
    tri = (col <= row).astype(BF16)
    lane = lax.broadcasted_iota(jnp.int32, (sub, HG_D), 1)
    trow = lax.broadcasted_iota(jnp.int32, (sub, HG_D), 0)

    q, k, b, bk = {}, {}, {}, {}
    for ch in chains:
        f = blk(f_ref, ch)
        k[ch] = 1.0 - f
        lf = jnp.log(f) * LOG2E
        l1 = lf.astype(BF16)
        r1 = lf - l1.astype(F32)
        l2 = r1.astype(BF16)
        l3 = (r1 - l2.astype(F32)).astype(BF16)
        b[ch] = _dot(tri, l1) + _dot(tri, l2) + _dot(tri, l3)
    for ch in chains:
        q[ch] = blk(q_ref, ch).astype(F32)
        bk[ch] = b[ch] - jnp.log(k[ch]) * LOG2E

    qb, kdec, ebl, o_inter = {}, {}, {}, {}
    for ch in chains:
        bl = b[ch][c - 1:c, :]
        qb[ch] = (q[ch] * jnp.exp2(b[ch])).astype(BF16)
        kdec[ch] = (k[ch] * jnp.exp2(bl - b[ch])).astype(BF16)
        ebl[ch] = jnp.exp2(bl)
    for hh in range(HG_HEADS):
        st = st_ref[hh]
        for ci in range(HG_STEP_CHUNKS):
            ch = (hh, ci)
            o_inter[ch] = _dot_nt(qb[ch], st.astype(BF16))
            st = st * ebl[ch] + _dot_tn(blk(v_ref, ch), kdec[ch])
        st_ref[hh] = st

    a_rows = {ch: [] for ch in chains}
    for i in range(n_sub):
        rs = slice(i * sub, (i + 1) * sub)
        tiles = {ch: jnp.zeros((sub, HG_D), F32) for ch in chains}
        for ss in range(sub):
            s = i * sub + ss
            for ch in chains:
                kdec_s = jnp.exp2(jnp.minimum(b[ch][rs] - bk[ch][s:s + 1, :], 0.0))
                a_s = jnp.sum(q[ch][rs] * kdec_s, axis=-1, keepdims=True)
                tiles[ch] = jnp.where(lane == s, a_s, tiles[ch])
        for ch in chains:
            a_rows[ch].append(jnp.where(trow + i * sub >= lane, tiles[ch], 0.0)[:, :c])
    for i in range(1, n_sub):
        rs = slice(i * sub, (i + 1) * sub)
        for n, ch in enumerate(chains):
            r = b[ch][i * sub - 1:i * sub, :]
            ks_ref[n, i - 1, :i * sub, :] = k[ch][:i * sub] * jnp.exp2(r - b[ch][:i * sub])
            qs = (q[ch][rs] * jnp.exp2(b[ch][rs] - r)).astype(BF16)
            a_rows[ch][i] = a_rows[ch][i] + _dot_nt(qs, ks_ref[n, i - 1].astype(BF16))
    o = {}
    for ch in chains:
        a = jnp.concatenate(a_rows[ch], axis=0).astype(BF16)
        o[ch] = o_inter[ch] + _dot(a, blk(v_ref, ch))

    gn = gn_ref[...]
    for ch in chains:
        hh, ci = ch
        on = o[ch] * lax.rsqrt(jnp.mean(o[ch] * o[ch], axis=-1, keepdims=True) + RMS_EPS) * gn
        o_ref[ci * c:(ci + 1) * c, hh * HG_D:(hh + 1) * HG_D] = (on * blk(g_ref, ch).astype(F32)).astype(BF16)


def _hgrn(hq, hf, hi, hg, gn, l, bsz, seq):
    n_tok = hq.shape[0]
    c = HG_CHUNK
    rows = c * HG_STEP_CHUNKS
    nc = seq // rows
    spec = pl.BlockSpec((rows, HG_WIDTH), lambda b, i: (b * nc + i, 0))
    return pl.pallas_call(
        _hgrn_kernel,
        grid=(bsz, nc),
        in_specs=[spec, spec, spec, spec, _layer_spec(gn.shape, l)],
        out_specs=spec,
        out_shape=jax.ShapeDtypeStruct((n_tok, HG_WIDTH), BF16),
        scratch_shapes=[pltpu.VMEM((HG_HEADS, HG_D, HG_D), F32),
                        pltpu.VMEM((HG_HEADS * HG_STEP_CHUNKS, c // HG_SUB - 1, c, HG_D), F32)],
        compiler_params=_params(("parallel", "arbitrary")),
        name="hgrn",
    )(hq, hf, hi, hg, gn)


def _memkv_kernel(mem_ref, g_ref, w_ref, kv_ref):
    kv_ref[...] = _dot(_rms(mem_ref[...], g_ref[...]).astype(BF16), w_ref[...]).astype(BF16)


def _memkv(mem2, g, w, l, n_mem):
    n_rows, d_model = mem2.shape
    return pl.pallas_call(
        _memkv_kernel,
        grid=(n_rows // n_mem,),
        in_specs=[pl.BlockSpec((n_mem, d_model), lambda i: (i, 0)), _layer_spec(g.shape, l), _layer_spec(w.shape, l)],
        out_specs=pl.BlockSpec((n_mem, 2 * X_WIDTH), lambda i: (i, 0)),
        out_shape=jax.ShapeDtypeStruct((n_rows, 2 * X_WIDTH), BF16),
        compiler_params=_params(("parallel",)),
        name="memkv",
    )(mem2, g, w)


def _mid_kernel(x_ref, ys_ref, om_ref, oh_ref, gate_ref, kv_ref, wmo_ref, who_ref, wout_ref, gx_ref, wq_ref,
                wxo_ref, o_ref, *, x_scale):
    d_model = x_ref.shape[1]
    om_t = om_ref[0].reshape(MLA_HEADS * MLA_V, x_ref.shape[0])
    y_mla = _dot_tn(om_t, wmo_ref[...])
    y_hg = _dot(oh_ref[...], who_ref[...])
    gate = lambda br: gate_ref[:, br * d_model:(br + 1) * d_model].astype(F32)
    merged = gate(0) * ys_ref[...] + gate(1) * y_mla + gate(2) * y_hg
    x = x_ref[...] + _dot(merged.astype(BF16), wout_ref[...])

    q = (_dot(_rms(x, gx_ref[...]).astype(BF16), wq_ref[...]) * x_scale).astype(BF16)
    outs = []
    for hh in range(X_HEADS):
        sl = slice(hh * X_HEAD_DIM, (hh + 1) * X_HEAD_DIM)
        s = _dot_nt(q[:, sl], kv_ref[:, sl])
        pr = jnp.exp2(s - jnp.max(s, axis=-1, keepdims=True))
        den = jnp.sum(pr, axis=-1, keepdims=True)
        vh = kv_ref[:, X_WIDTH + hh * X_HEAD_DIM:X_WIDTH + (hh + 1) * X_HEAD_DIM]
        outs.append((_dot(pr.astype(BF16), vh) / den).astype(BF16))
    o_ref[...] = x + _dot(jnp.concatenate(outs, axis=1), wxo_ref[...])


def _mid(x2, ys, om, oh, gates, kv, wmo, who, wout, gx, wq, wxo, l, seq, n_mem):
    n_tok, d_model = x2.shape
    t = MID_TILE
    nt = seq // t
    tok = lambda w_: pl.BlockSpec((t, w_), lambda i: (i, 0))
    return pl.pallas_call(
        functools.partial(_mid_kernel, x_scale=LOG2E / math.sqrt(X_HEAD_DIM)),
        grid=(n_tok // t,),
        in_specs=[tok(d_model), tok(d_model),
                  pl.BlockSpec((1, MLA_HEADS, MLA_V, t), lambda i: (i // nt, 0, 0, i % nt)),
                  tok(HG_WIDTH), tok(N_BRANCH * d_model),
                  pl.BlockSpec((n_mem, 2 * X_WIDTH), lambda i: (i // nt, 0)),
                  ] + [_layer_spec(w_.shape, l) for w_ in (wmo, who, wout, gx, wq, wxo)],
        out_specs=tok(d_model),
        out_shape=jax.ShapeDtypeStruct((n_tok, d_model), F32),
        compiler_params=_params(("parallel",)),
        name="mid",
    )(x2, ys, om, oh, gates, kv, wmo, who, wout, gx, wq, wxo)


def _ffn_kernel(x_ref, g_ref, wgu_ref, wd_ref, gf_ref, o_ref, *, final_norm):
    d_ff = wd_ref.shape[0]
    x = x_ref[...]
    h = _rms(x, g_ref[...]).astype(BF16)
    y = x
    for c0 in range(0, d_ff, FFN_CHUNK):
        gt = _dot(h, wgu_ref[:, c0:c0 + FFN_CHUNK])
        up = _dot(h, wgu_ref[:, d_ff + c0:d_ff + c0 + FFN_CHUNK])
        act = (gt * _sigmoid(gt) * up).astype(BF16)
        y = y + _dot(act, wd_ref[c0:c0 + FFN_CHUNK, :])
    if final_norm:
        y = _rms(y, gf_ref[...])
    o_ref[...] = y


def _ffn(x2, g, wgu, wd, gf, l, final_norm):
    n_tok, d_model = x2.shape
    t = FFN_TILE
    tok = pl.BlockSpec((t, d_model), lambda i: (i, 0))
    return pl.pallas_call(
        functools.partial(_ffn_kernel, final_norm=final_norm),
        grid=(n_tok // t,),
        in_specs=[tok, _layer_spec(g.shape, l), _layer_spec(wgu.shape, l), _layer_spec(wd.shape, l),
                  _const_spec(gf.shape)],
        out_specs=tok,
        out_shape=jax.ShapeDtypeStruct((n_tok, d_model), F32),
        compiler_params=_params(("parallel",)),
        name="ffn",
    )(x2, g, wgu, wd, gf)


def _in_proj_weights(w_in):
    kpe = jnp.pad(w_in[:, :, C_KPE:C_KPE + MLA_ROPE], ((0, 0), (0, 0), (MLA_NOPE, HEAD_PAD - MLA_NOPE - MLA_ROPE)))
    cols = [w_in[:, :, :C_KPE], kpe, w_in[:, :, C_KPE + MLA_ROPE:]]
    return jnp.concatenate(cols, axis=2).astype(BF16)


def _mla_weights(w_uq, w_ukv):
    depth, rank, _ = w_uq.shape
    dq = MLA_NOPE + MLA_ROPE
    wq = w_uq.reshape(depth, rank, MLA_HEADS, dq)
    wuq = jnp.pad(wq, ((0, 0), (0, 0), (0, 0), (0, HEAD_PAD - dq))).reshape(depth, rank, MLA_HEADS * HEAD_PAD)
    kvr = w_ukv.shape[1]
    wkv = w_ukv.reshape(depth, kvr, MLA_HEADS, MLA_NOPE + MLA_V)
    wuk = wkv[..., :MLA_NOPE].reshape(depth, kvr, -1)
    wuvt = wkv[..., MLA_NOPE:].reshape(depth, kvr, -1).transpose(0, 2, 1)
    return wuq.astype(BF16), wuk.astype(BF16), wuvt.astype(BF16)


def _s5_weights(lam_re, lam_im, b_re, b_im, c_re, c_im, log_step):
    depth = lam_re.shape[0]
    step = jnp.exp(log_step)[..., None]
    mag = jnp.exp(lam_re * step)
    lbr = mag * jnp.cos(lam_im * step)
    lbi = mag * jnp.sin(lam_im * step)
    den = lam_re * lam_re + lam_im * lam_im
    cr = ((lbr - 1.0) * lam_re + lbi * lam_im) / den
    ci = (lbi * lam_re - (lbr - 1.0) * lam_im) / den
    bbr = cr[..., None] * b_re - ci[..., None] * b_im
    bbi = cr[..., None] * b_im + ci[..., None] * b_re
    gq = SSM_GROUPS // SSM_QUARTERS
    eye = jnp.eye(gq, dtype=F32)

    def blockdiag(m):
        m = m.reshape(depth, SSM_QUARTERS, gq, m.shape[2], m.shape[3]).transpose(0, 1, 2, 4, 3)
        out = m[:, :, :, :, None, :] * eye[None, None, :, None, :, None]
        return out.reshape(depth, SSM_QUARTERS, gq * m.shape[3], gq * m.shape[4])

    wb = jnp.concatenate([blockdiag(bbr), blockdiag(bbi)], axis=3).astype(BF16)
    wc = jnp.concatenate([blockdiag(c_re), -blockdiag(c_im)], axis=3).astype(BF16)
    lr = lbr.reshape(depth, 1, SSM_CH)
    li = lbi.reshape(depth, 1, SSM_CH)
    a = jnp.broadcast_to(lr, (depth, 8, SSM_CH))
    bsw = jnp.concatenate([jnp.broadcast_to(-li, (depth, 4, SSM_CH)), jnp.broadcast_to(li, (depth, 4, SSM_CH))],
                          axis=1)
    return wb, a, bsw, wc


def kernel(x, mem, positions, norm_mix, w_in, ssm_lam_re, ssm_lam_im, ssm_b_re, ssm_b_im, ssm_c_re, ssm_c_im, ssm_d, ssm_log_step, ssm_w_glu, mla_q_norm, mla_kv_norm, mla_w_uq, mla_w_ukv, mla_w_o, hg_lb, hg_g_norm, hg_w_o, w_out, norm_cross, norm_mem, x_w_q, x_w_kv, x_w_o, norm_ffn, ffn_w_gate_up, ffn_w_down, norm_final):
    bsz, seq, d_model = x.shape
    depth = w_in.shape[0]
    n_mem = mem.shape[1]
    assert bsz == 4, "the s5 scan packs 4 batches x (re, im) on the 8 sublanes"
    n_tok = bsz * seq

    half = MLA_ROPE // 2
    invf = (ROPE_THETA ** (-jnp.arange(half, dtype=F32) / half)).reshape(half, 1)
    lb_p = jax.nn.softmax(hg_lb.astype(F32), axis=0)
    lower_bounds = jnp.cumsum(lb_p, axis=0) - lb_p[0:1]

    rows = lambda a: a.reshape(depth, 1, -1)
    bf = lambda a: a.astype(BF16)
    w_inp = _in_proj_weights(w_in)
    wuq, wuk, wuvt = _mla_weights(mla_w_uq, mla_w_ukv)
    wb, a, bsw, wc = _s5_weights(ssm_lam_re, ssm_lam_im, ssm_b_re, ssm_b_im, ssm_c_re, ssm_c_im, ssm_log_step)
    wglu, wmo, who, wout = bf(ssm_w_glu), bf(mla_w_o), bf(hg_w_o), bf(w_out)
    wxq, wxkv, wxo, wgu, wdn = bf(x_w_q), bf(x_w_kv), bf(x_w_o), bf(ffn_w_gate_up), bf(ffn_w_down)
    g_mix, g_q, g_kv, g_hg = rows(norm_mix), rows(mla_q_norm), rows(mla_kv_norm), rows(hg_g_norm)
    g_cross, g_mem, g_ffn = rows(norm_cross), rows(norm_mem), rows(norm_ffn)
    lbs, ssm_dr = rows(lower_bounds), rows(ssm_d)

    x2 = x.reshape(n_tok, d_model)
    mem2 = mem.reshape(bsz * n_mem, d_model)
    pos2 = positions.reshape(n_tok // IN_TILE, 1, IN_TILE)

    for l in range(depth):
        u, q, k, v, hq, hf, hi, hg, gates = _in_proj(x2, pos2, g_mix, w_inp, invf, g_q, wuq, g_kv, wuk, wuvt,
                                                     lbs, l, bsz, seq)
        ys = _s5(u.reshape(bsz, seq, SSM_WIDTH), wb, a, bsw, wc, ssm_dr, wglu, l, d_model).reshape(n_tok, d_model)
        om = _mla(q, k, v)
        oh = _hgrn(hq, hf, hi, hg, g_hg, l, bsz, seq)
        kv = _memkv(mem2, g_mem, wxkv, l, n_mem)
        x2 = _mid(x2, ys, om, oh, gates, kv, wmo, who, wout, g_cross, wxq, wxo, l, seq, n_mem)
        x2 = _ffn(x2, g_ffn, wgu, wdn, norm_final.reshape(1, -1), l, final_norm=(l == depth - 1))
    return x2.reshape(bsz, seq, d_model)
```

```python
import functools
import math

import jax
import jax.numpy as jnp
from jax import lax
from jax.experimental import pallas as pl
from jax.experimental.pallas import tpu as pltpu

F32 = jnp.float32
BF16 = jnp.bfloat16

RMS_EPS = 1e-6
LOG2E = 1.4426950408889634

SSM_GROUPS = 32
SSM_GROUP_CH = 16
SSM_STATE = 64
SSM_WIDTH = SSM_GROUPS * SSM_GROUP_CH
SSM_CH = SSM_GROUPS * SSM_STATE
SSM_QUARTERS = 4

MLA_HEADS = 8
MLA_Q_RANK = 512
MLA_KV_RANK = 256
MLA_NOPE = 64
MLA_ROPE = 32
MLA_V = 64
ROPE_THETA = 10000.0
HEAD_PAD = 128
MLA_VT_ROWS = 80

HG_HEADS = 4
HG_D = 128
HG_WIDTH = HG_HEADS * HG_D
HG_CHUNK = 64
HG_SUB = 8
HG_STEP_CHUNKS = 8

X_HEADS = 4
X_HEAD_DIM = 128
X_WIDTH = X_HEADS * X_HEAD_DIM

N_BRANCH = 3

VMEM_LIMIT = 56 * 1024 * 1024

IN_TILE = 512
S5_TIME = 128
MLA_BLOCK = 512
MLA_KEY_BLOCKS = 1
MID_TILE = 512
FFN_TILE = 512
FFN_CHUNK = 2816
assert IN_TILE == MLA_BLOCK and MID_TILE == MLA_BLOCK


def _dot(a, b):
    return jnp.dot(a, b, preferred_element_type=F32)


def _dot_nt(a, b):
    return lax.dot_general(a, b, (((1,), (1,)), ((), ())), preferred_element_type=F32)


def _dot_tn(a, b):
    return lax.dot_general(a, b, (((0,), (0,)), ((), ())), preferred_element_type=F32)


def _rms(x, g):
    return x * lax.rsqrt(jnp.mean(x * x, axis=-1, keepdims=True) + RMS_EPS) * g


def _sigmoid(x):
    return 1.0 / (1.0 + jnp.exp(-x))


def _const_spec(shape):
    nd = len(shape)
    return pl.BlockSpec(shape, lambda *_: (0,) * nd, pipeline_mode=pl.Buffered(1))


def _layer_spec(shape, l):
    nd = len(shape)
    return pl.BlockSpec((None,) + tuple(shape[1:]), lambda *_: (l,) + (0,) * (nd - 1),
                        pipeline_mode=pl.Buffered(1))


def _params(sem):
    return pltpu.CompilerParams(dimension_semantics=sem, vmem_limit_bytes=VMEM_LIMIT)


C_SSM = 0
C_QLAT = C_SSM + SSM_WIDTH
C_KVLAT = C_QLAT + MLA_Q_RANK
C_KPE = C_KVLAT + MLA_KV_RANK
C_HQ = C_KPE + HEAD_PAD
C_HF = C_HQ + HG_WIDTH
C_HI = C_HF + HG_WIDTH
C_HG = C_HI + HG_WIDTH
C_GATE = C_HG + HG_WIDTH


def _in_proj_kernel(x_ref, pos_ref, g_ref, w_ref, invf_ref, qn_ref, wuq_ref, kvn_ref, wuk_ref, wuvt_ref, lb_ref,
                    u_ref, q_ref, k_ref, v_ref, hq_ref, hf_ref, hi_ref, hg_ref, gate_ref, *, q_scale):
    d_model = x_ref.shape[1]
    h = _rms(x_ref[...], g_ref[...]).astype(BF16)

    def seg(a, b):
        return _dot(h, w_ref[:, a:b])

    qlat = seg(C_QLAT, C_KVLAT)
    kvlat = seg(C_KVLAT, C_KPE)
    kp = seg(C_KPE, C_HQ)
    qn = _rms(qlat, qn_ref[...]).astype(BF16)
    kvn = _rms(kvlat, kvn_ref[...]).astype(BF16)

    t = x_ref.shape[0]
    ang = invf_ref[...] * pos_ref[0].astype(F32)
    cos_c = jnp.cos(ang)
    sin_c = jnp.sin(ang)
    fill = lambda v, n: jnp.full((n, t), v, F32)
    tail = HEAD_PAD - MLA_NOPE - MLA_ROPE
    cos = jnp.concatenate([fill(1.0, MLA_NOPE), cos_c, cos_c, fill(1.0, tail)], axis=0).T
    sin = jnp.concatenate([fill(0.0, MLA_NOPE), -sin_c, sin_c, fill(0.0, tail)], axis=0).T
    lane = lax.broadcasted_iota(jnp.int32, (1, HEAD_PAD), 1)
    first_half = lane < MLA_NOPE + MLA_ROPE // 2

    def rope(v):
        swapped = jnp.where(first_half, pltpu.roll(v, HEAD_PAD - MLA_ROPE // 2, 1),
                            pltpu.roll(v, MLA_ROPE // 2, 1))
        return v * cos + swapped * sin

    u_ref[...] = seg(C_SSM, C_QLAT)
    hq = seg(C_HQ, C_HF)
    hq_ref[...] = (hq * _sigmoid(hq)).astype(BF16)
    lb = lb_ref[...]
    hf_ref[...] = lb + (1.0 - lb) * _sigmoid(seg(C_HF, C_HI))
    hi_ref[...] = seg(C_HI, C_HG).astype(BF16)
    hg = seg(C_HG, C_GATE)
    hg_ref[...] = (hg * _sigmoid(hg)).astype(BF16)
    for br in range(N_BRANCH):
        gate_ref[:, br * d_model:(br + 1) * d_model] = _sigmoid(
            seg(C_GATE + br * d_model, C_GATE + (br + 1) * d_model)).astype(BF16)

    qa = _dot(qn, wuq_ref[...])
    for hh in range(MLA_HEADS):
        q_ref[0, hh] = (rope(qa[:, hh * HEAD_PAD:(hh + 1) * HEAD_PAD]) * q_scale).astype(BF16)

    kn = _dot(kvn, wuk_ref[...])
    vt = _dot_nt(wuvt_ref[...], kvn)
    kpe = rope(kp)
    nope_lanes = lane < MLA_NOPE
    ones_rows = (lax.broadcasted_iota(jnp.int32, (MLA_VT_ROWS - MLA_V, t), 0) == 0).astype(BF16)
    for hh in range(MLA_HEADS):
        piece = kn[:, (hh // 2) * HEAD_PAD:(hh // 2 + 1) * HEAD_PAD]
        if hh % 2:
            piece = pltpu.roll(piece, MLA_NOPE, 1)
        k_ref[0, hh] = jnp.where(nope_lanes, piece, kpe).astype(BF16)
        v_ref[0, 0, hh, :MLA_V, :] = vt[hh * MLA_V:(hh + 1) * MLA_V].astype(BF16)
        v_ref[0, 0, hh, MLA_V:, :] = ones_rows


def _in_proj(x2, pos2, g, w, invf, qn, wuq, kvn, wuk, wuvt, lb, l, bsz, seq):
    n_tok, d_model = x2.shape
    t = IN_TILE
    nt = seq // t
    tok = lambda w_: pl.BlockSpec((t, w_), lambda i: (i, 0))
    lay = lambda a: _layer_spec(a.shape, l)
    head = pl.BlockSpec((1, MLA_HEADS, t, HEAD_PAD), lambda i: (i // nt, 0, i % nt, 0))
    head_shape = jax.ShapeDtypeStruct((bsz, MLA_HEADS, seq, HEAD_PAD), BF16)
    head_t = pl.BlockSpec((1, 1, MLA_HEADS, MLA_VT_ROWS, t), lambda i: (i // nt, i % nt, 0, 0, 0))
    head_t_shape = jax.ShapeDtypeStruct((bsz, nt, MLA_HEADS, MLA_VT_ROWS, t), BF16)
    q_scale = LOG2E / math.sqrt(MLA_NOPE + MLA_ROPE)
    return pl.pallas_call(
        functools.partial(_in_proj_kernel, q_scale=q_scale),
        grid=(n_tok // t,),
        in_specs=[tok(d_model), pl.BlockSpec((1, 1, t), lambda i: (i, 0, 0)), lay(g), lay(w),
                  _const_spec(invf.shape), lay(qn), lay(wuq), lay(kvn), lay(wuk), lay(wuvt), lay(lb)],
        out_specs=[tok(SSM_WIDTH), head, head, head_t, tok(HG_WIDTH), tok(HG_WIDTH), tok(HG_WIDTH), tok(HG_WIDTH),
                   tok(N_BRANCH * d_model)],
        out_shape=[jax.ShapeDtypeStruct((n_tok, SSM_WIDTH), F32), head_shape, head_shape, head_t_shape,
                   jax.ShapeDtypeStruct((n_tok, HG_WIDTH), BF16), jax.ShapeDtypeStruct((n_tok, HG_WIDTH), F32),
                   jax.ShapeDtypeStruct((n_tok, HG_WIDTH), BF16), jax.ShapeDtypeStruct((n_tok, HG_WIDTH), BF16),
                   jax.ShapeDtypeStruct((n_tok, N_BRANCH * d_model), BF16)],
        compiler_params=_params(("parallel",)),
        name="in_proj",
    )(x2, pos2, g, w, invf, qn, wuq, kvn, wuk, wuvt, lb)


def _s5_kernel(u_ref, wb_ref, a_ref, bsw_ref, wc_ref, d_ref, wglu_ref, y_ref, se_ref, so_ref, st_ref, utb_ref,
               ytb_ref):
    bsz, tc, _ = u_ref.shape
    rows = bsz * tc
    d_model = y_ref.shape[2]
    qw = SSM_CH // SSM_QUARTERS
    uw = SSM_WIDTH // SSM_QUARTERS
    half = 4

    @pl.when(pl.program_id(0) == 0)
    def _():
        st_ref[...] = jnp.zeros_like(st_ref)

    for q in range(SSM_QUARTERS):
        for b in range(bsz):
            utb_ref[q, pl.ds(b, tc, stride=bsz), :] = u_ref[b, :, q * uw:(q + 1) * uw]

    low = (lax.broadcasted_iota(jnp.int32, (rows, qw), 0) % 8) < half
    for q in range(SSM_QUARTERS):
        o = _dot(utb_ref[q].astype(BF16), wb_ref[q])
        re = o[:, :qw]
        im = o[:, qw:]
        se_ref[:, q * qw:(q + 1) * qw] = jnp.where(low, re, pltpu.roll(im, half, 0))
        so_ref[:, q * qw:(q + 1) * qw] = jnp.where(low, pltpu.roll(re, rows - half, 0), im)

    n_blk = 2
    cw = SSM_CH // n_blk
    for cb in range(n_blk):
        cs = slice(cb * cw, (cb + 1) * cw)
        a = a_ref[:, cs]
        bs = bsw_ref[:, cs]

        def body(v, s, cs=cs, a=a, bs=bs):
            r = pl.multiple_of(v * 8, 8)
            s = a * s + bs * pltpu.roll(s, half, 0) + se_ref[pl.ds(r, 8), cs]
            se_ref[pl.ds(r, 8), cs] = s
            s = a * s + bs * pltpu.roll(s, half, 0) + so_ref[pl.ds(r, 8), cs]
            so_ref[pl.ds(r, 8), cs] = s
            return s

        st_ref[:, cs] = lax.fori_loop(0, rows // 8, body, st_ref[:, cs], unroll=4)

    yw = uw
    low_y = low[:, :yw]
    pieces = []
    for q in range(SSM_QUARTERS):
        oe = _dot(se_ref[:, q * qw:(q + 1) * qw].astype(BF16), wc_ref[q])
        oo = _dot(so_ref[:, q * qw:(q + 1) * qw].astype(BF16), wc_ref[q])
        ye = oe[:, :yw] + pltpu.roll(oe[:, yw:], rows - half, 0)
        yo = pltpu.roll(oo[:, :yw], half, 0) + oo[:, yw:]
        pieces.append(jnp.where(low_y, ye, yo) + d_ref[:, q * uw:(q + 1) * uw] * utb_ref[q])
    y = jax.nn.gelu(jnp.concatenate(pieces, axis=1)).astype(BF16)
    z = _dot(y, wglu_ref[...])
    res = z[:, :d_model] * _sigmoid(z[:, d_model:])
    n_slab = d_model // uw
    for j in range(n_slab):
        ytb_ref[j] = res[:, j * uw:(j + 1) * uw]
    for j in range(n_slab):
        for b in range(bsz):
            y_ref[b, :, j * uw:(j + 1) * uw] = ytb_ref[j, pl.ds(b, tc, stride=bsz), :]


def _s5(u3, wb, a, bsw, wc, d, wglu, l, d_model):
    bsz, seq, _ = u3.shape
    tc = S5_TIME
    rows = tc * bsz
    lanes = SSM_WIDTH // SSM_QUARTERS
    return pl.pallas_call(
        _s5_kernel,
        grid=(seq // tc,),
        in_specs=[pl.BlockSpec((bsz, tc, SSM_WIDTH), lambda i: (0, i, 0))]
        + [_layer_spec(w_.shape, l) for w_ in (wb, a, bsw, wc, d, wglu)],
        out_specs=pl.BlockSpec((bsz, tc, d_model), lambda i: (0, i, 0)),
        out_shape=jax.ShapeDtypeStruct((bsz, seq, d_model), F32),
        scratch_shapes=[pltpu.VMEM((rows, SSM_CH), F32), pltpu.VMEM((rows, SSM_CH), F32),
                        pltpu.VMEM((8, SSM_CH), F32), pltpu.VMEM((SSM_QUARTERS, rows, lanes), F32),
                        pltpu.VMEM((d_model // lanes, rows, lanes), F32)],
        compiler_params=_params(("arbitrary",)),
        name="s5",
    )(u3, wb, a, bsw, wc, d, wglu)


def _mla_kernel(qi_ref, kj_ref, q_ref, k_ref, vt_ref, o_ref, m_ref, acc_ref):
    p = pl.program_id(1)
    qi = qi_ref[p]
    kj = kj_ref[p]
    blk = q_ref.shape[2]

    @pl.when(kj == 0)
    def _():
        m_ref[...] = jnp.full_like(m_ref, -1e30)
        acc_ref[...] = jnp.zeros_like(acc_ref)

    def step(masked, sub):
        if masked:
            key = lax.broadcasted_iota(jnp.int32, (blk, blk), 0)
            qry = lax.broadcasted_iota(jnp.int32, (blk, blk), 1)
            causal = key <= qry
        sc, mx = {}, {}
        for t in range(MLA_HEADS + 2):
            if t < MLA_HEADS:
                s = _dot_nt(k_ref[0, t, sub * blk:(sub + 1) * blk, :], q_ref[0, t])
                sc[t] = jnp.where(causal, s, -1e30) if masked else s
            if 1 <= t <= MLA_HEADS:
                hh = t - 1
                m_prev = m_ref[hh]
                m_new = jnp.maximum(m_prev, jnp.max(sc[hh], axis=0, keepdims=True))
                mx[hh] = (m_new, jnp.exp2(m_prev - m_new))
                m_ref[hh] = m_new
            if t >= 2:
                hh = t - 2
                m_new, alpha = mx.pop(hh)
                pr = jnp.exp2(sc.pop(hh) - m_new).astype(BF16)
                acc_ref[hh] = alpha * acc_ref[hh] + _dot(vt_ref[0, sub, hh], pr)

    for sub in range(MLA_KEY_BLOCKS):
        kb = kj * MLA_KEY_BLOCKS + sub

        @pl.when(kb < qi)
        def _(sub=sub):
            step(False, sub)

        @pl.when(kb == qi)
        def _(sub=sub):
            step(True, sub)
            for hh in range(MLA_HEADS):
                acc = acc_ref[hh]
                o_ref[0, 0, hh] = (acc[:MLA_V, :] / acc[MLA_V:MLA_V + 1, :]).astype(BF16)


def _mla(q, k, vt):
    bsz, nh, seq, hd = q.shape
    blk = MLA_BLOCK
    nb = seq // blk
    pairs = [(i, j) for i in range(nb) for j in range(i // MLA_KEY_BLOCKS + 1)]
    qi = jnp.asarray([p[0] for p in pairs], jnp.int32)
    kj = jnp.asarray([p[1] for p in pairs], jnp.int32)
    qspec = pl.BlockSpec((1, nh, blk, hd), lambda b, p, qi, kj: (b, 0, qi[p], 0))
    kspec = pl.BlockSpec((1, nh, blk * MLA_KEY_BLOCKS, hd), lambda b, p, qi, kj: (b, 0, kj[p], 0))
    vspec = pl.BlockSpec((1, MLA_KEY_BLOCKS, nh, MLA_VT_ROWS, blk), lambda b, p, qi, kj: (b, kj[p], 0, 0, 0))
    ospec = pl.BlockSpec((1, 1, nh, MLA_V, blk), lambda b, p, qi, kj: (b, qi[p], 0, 0, 0))
    return pl.pallas_call(
        _mla_kernel,
        grid_spec=pltpu.PrefetchScalarGridSpec(
            num_scalar_prefetch=2,
            grid=(bsz, len(pairs)),
            in_specs=[qspec, kspec, vspec],
            out_specs=ospec,
            scratch_shapes=[pltpu.VMEM((nh, 1, blk), F32), pltpu.VMEM((nh, MLA_VT_ROWS, blk), F32)],
        ),
        out_shape=jax.ShapeDtypeStruct((bsz, nb, nh, MLA_V, blk), BF16),
        compiler_params=_params(("parallel", "arbitrary")),
        name="mla",
    )(qi, kj, q, k, vt)


def _hgrn_kernel(q_ref, f_ref, v_ref, g_ref, gn_ref, o_ref, st_ref, ks_ref):
    c = HG_CHUNK
    sub = HG_SUB
    n_sub = c // sub

    @pl.when(pl.program_id(1) == 0)
    def _():
        st_ref[...] = jnp.zeros_like(st_ref)
        ks_ref[...] = jnp.zeros_like(ks_ref)

    chains = [(hh, ci) for hh in range(HG_HEADS) for ci in range(HG_STEP_CHUNKS)]

    def blk(ref, ch):
        hh, ci = ch
        return ref[ci * c:(ci + 1) * c, hh * HG_D:(hh + 1) * HG_D]

    row = lax.broadcasted_iota(jnp.int32, (c, c), 0)
    col = lax.broadcasted_iota(jnp.int32, (c, c), 1)
    tri = (col <= row).astype(BF16)
    lane = lax.broadcasted_iota(jnp.int32, (sub, HG_D), 1)
    trow = lax.broadcasted_iota(jnp.int32, (sub, HG_D), 0)

    q, k, b, bk = {}, {}, {}, {}
    for ch in chains:
        f = blk(f_ref, ch)
        k[ch] = 1.0 - f
        lf = jnp.log(f) * LOG2E
        l1 = lf.astype(BF16)
        r1 = lf - l1.astype(F32)
        l2 = r1.astype(BF16)
        l3 = (r1 - l2.astype(F32)).astype(BF16)
        b[ch] = _dot(tri, l1) + _dot(tri, l2) + _dot(tri, l3)
    for ch in chains:
        q[ch] = blk(q_ref, ch).astype(F32)
        bk[ch] = b[ch] - jnp.log(k[ch]) * LOG2E

    qb, kdec, ebl, o_inter = {}, {}, {}, {}
    for ch in chains:
        bl = b[ch][c - 1:c, :]
        qb[ch] = (q[ch] * jnp.exp2(b[ch])).astype(BF16)
        kdec[ch] = (k[ch] * jnp.exp2(bl - b[ch])).astype(BF16)
        ebl[ch] = jnp.exp2(bl)
    for hh in range(HG_HEADS):
        st = st_ref[hh]
        for ci in range(HG_STEP_CHUNKS):
            ch = (hh, ci)
            o_inter[ch] = _dot_nt(qb[ch], st.astype(BF16))
            st = st * ebl[ch] + _dot_tn(blk(v_ref, ch), kdec[ch])
        st_ref[hh] = st

    a_rows = {ch: [] for ch in chains}
    for i in range(n_sub):
        rs = slice(i * sub, (i + 1) * sub)
        tiles = {ch: jnp.zeros((sub, HG_D), F32) for ch in chains}
        for ss in range(sub):
            s = i * sub + ss
            for ch in chains:
                kdec_s = jnp.exp2(jnp.minimum(b[ch][rs] - bk[ch][s:s + 1, :], 0.0))
                a_s = jnp.sum(q[ch][rs] * kdec_s, axis=-1, keepdims=True)
                tiles[ch] = jnp.where(lane == s, a_s, tiles[ch])
        for ch in chains:
            a_rows[ch].append(jnp.where(trow + i * sub >= lane, tiles[ch], 0.0)[:, :c])
    for i in range(1, n_sub):
        rs = slice(i * sub, (i + 1) * sub)
        for n, ch in enumerate(chains):
            r = b[ch][i * sub - 1:i * sub, :]
            ks_ref[n, i - 1, :i * sub, :] = k[ch][:i * sub] * jnp.exp2(r - b[ch][:i * sub])
            qs = (q[ch][rs] * jnp.exp2(b[ch][rs] - r)).astype(BF16)
            a_rows[ch][i] = a_rows[ch][i] + _dot_nt(qs, ks_ref[n, i - 1].astype(BF16))
    o = {}
    for ch in chains:
        a = jnp.concatenate(a_rows[ch], axis=0).astype(BF16)
        o[ch] = o_inter[ch] + _dot(a, blk(v_ref, ch))

    gn = gn_ref[...]
    for ch in chains:
        hh, ci = ch
        on = o[ch] * lax.rsqrt(jnp.mean(o[ch] * o[ch], axis=-1, keepdims=True) + RMS_EPS) * gn
        o_ref[ci * c:(ci + 1) * c, hh * HG_D:(hh + 1) * HG_D] = (on * blk(g_ref, ch).astype(F32)).astype(BF16)


def _hgrn(hq, hf, hi, hg, gn, l, bsz, seq):
    n_tok = hq.shape[0]
    c = HG_CHUNK
    rows = c * HG_STEP_CHUNKS
    nc = seq // rows
    spec = pl.BlockSpec((rows, HG_WIDTH), lambda b, i: (b * nc + i, 0))
    return pl.pallas_call(
        _hgrn_kernel,
        grid=(bsz, nc),
        in_specs=[spec, spec, spec, spec, _layer_spec(gn.shape, l)],
        out_specs=spec,
        out_shape=jax.ShapeDtypeStruct((n_tok, HG_WIDTH), BF16),
        scratch_shapes=[pltpu.VMEM((HG_HEADS, HG_D, HG_D), F32),
                        pltpu.VMEM((HG_HEADS * HG_STEP_CHUNKS, c // HG_SUB - 1, c, HG_D), F32)],
        compiler_params=_params(("parallel", "arbitrary")),
        name="hgrn",
    )(hq, hf, hi, hg, gn)


def _memkv_kernel(mem_ref, g_ref, w_ref, kv_ref):
    kv_ref[...] = _dot(_rms(mem_ref[...], g_ref[...]).astype(BF16), w_ref[...]).astype(BF16)


def _memkv(mem2, g, w, l, n_mem):
    n_rows, d_model = mem2.shape
    return pl.pallas_call(
        _memkv_kernel,
        grid=(n_rows // n_mem,),
        in_specs=[pl.BlockSpec((n_mem, d_model), lambda i: (i, 0)), _layer_spec(g.shape, l), _layer_spec(w.shape, l)],
        out_specs=pl.BlockSpec((n_mem, 2 * X_WIDTH), lambda i: (i, 0)),
        out_shape=jax.ShapeDtypeStruct((n_rows, 2 * X_WIDTH), BF16),
        compiler_params=_params(("parallel",)),
        name="memkv",
    )(mem2, g, w)


def _mid_kernel(x_ref, ys_ref, om_ref, oh_ref, gate_ref, kv_ref, wmo_ref, who_ref, wout_ref, gx_ref, wq_ref,
                wxo_ref, o_ref, *, x_scale):
    d_model = x_ref.shape[1]
    om_t = om_ref[0, 0].reshape(MLA_HEADS * MLA_V, x_ref.shape[0])
    y_mla = _dot_tn(om_t, wmo_ref[...])
    y_hg = _dot(oh_ref[...], who_ref[...])
    gate = lambda br: gate_ref[:, br * d_model:(br + 1) * d_model].astype(F32)
    merged = gate(0) * ys_ref[...] + gate(1) * y_mla + gate(2) * y_hg
    x = x_ref[...] + _dot(merged.astype(BF16), wout_ref[...])

    q = (_dot(_rms(x, gx_ref[...]).astype(BF16), wq_ref[...]) * x_scale).astype(BF16)
    outs = []
    for hh in range(X_HEADS):
        sl = slice(hh * X_HEAD_DIM, (hh + 1) * X_HEAD_DIM)
        s = _dot_nt(q[:, sl], kv_ref[:, sl])
        pr = jnp.exp2(s - jnp.max(s, axis=-1, keepdims=True))
        den = jnp.sum(pr, axis=-1, keepdims=True)
        vh = kv_ref[:, X_WIDTH + hh * X_HEAD_DIM:X_WIDTH + (hh + 1) * X_HEAD_DIM]
        outs.append((_dot(pr.astype(BF16), vh) / den).astype(BF16))
    o_ref[...] = x + _dot(jnp.concatenate(outs, axis=1), wxo_ref[...])


def _mid(x2, ys, om, oh, gates, kv, wmo, who, wout, gx, wq, wxo, l, seq, n_mem):
    n_tok, d_model = x2.shape
    t = MID_TILE
    nt = seq // t
    tok = lambda w_: pl.BlockSpec((t, w_), lambda i: (i, 0))
    return pl.pallas_call(
        functools.partial(_mid_kernel, x_scale=LOG2E / math.sqrt(X_HEAD_DIM)),
        grid=(n_tok // t,),
        in_specs=[tok(d_model), tok(d_model),
                  pl.BlockSpec((1, 1, MLA_HEADS, MLA_V, t), lambda i: (i // nt, i % nt, 0, 0, 0)),
                  tok(HG_WIDTH), tok(N_BRANCH * d_model),
                  pl.BlockSpec((n_mem, 2 * X_WIDTH), lambda i: (i // nt, 0)),
                  ] + [_layer_spec(w_.shape, l) for w_ in (wmo, who, wout, gx, wq, wxo)],
        out_specs=tok(d_model),
        out_shape=jax.ShapeDtypeStruct((n_tok, d_model), F32),
        compiler_params=_params(("parallel",)),
        name="mid",
    )(x2, ys, om, oh, gates, kv, wmo, who, wout, gx, wq, wxo)


def _ffn_kernel(x_ref, g_ref, wgu_ref, wd_ref, gf_ref, o_ref, *, final_norm):
    d_ff = wd_ref.shape[0]
    x = x_ref[...]
    h = _rms(x, g_ref[...]).astype(BF16)
    y = x
    for c0 in range(0, d_ff, FFN_CHUNK):
        gt = _dot(h, wgu_ref[:, c0:c0 + FFN_CHUNK])
        up = _dot(h, wgu_ref[:, d_ff + c0:d_ff + c0 + FFN_CHUNK])
        act = (gt * _sigmoid(gt) * up).astype(BF16)
        y = y + _dot(act, wd_ref[c0:c0 + FFN_CHUNK, :])
    if final_norm:
        y = _rms(y, gf_ref[...])
    o_ref[...] = y


def _ffn(x2, g, wgu, wd, gf, l, final_norm):
    n_tok, d_model = x2.shape
    t = FFN_TILE
    tok = pl.BlockSpec((t, d_model), lambda i: (i, 0))
    return pl.pallas_call(
        functools.partial(_ffn_kernel, final_norm=final_norm),
        grid=(n_tok // t,),
        in_specs=[tok, _layer_spec(g.shape, l), _layer_spec(wgu.shape, l), _layer_spec(wd.shape, l),
                  _const_spec(gf.shape)],
        out_specs=tok,
        out_shape=jax.ShapeDtypeStruct((n_tok, d_model), F32),
        compiler_params=_params(("parallel",)),
        name="ffn",
    )(x2, g, wgu, wd, gf)


def _in_proj_weights(w_in):
    kpe = jnp.pad(w_in[:, :, C_KPE:C_KPE + MLA_ROPE], ((0, 0), (0, 0), (MLA_NOPE, HEAD_PAD - MLA_NOPE - MLA_ROPE)))
    cols = [w_in[:, :, :C_KPE], kpe, w_in[:, :, C_KPE + MLA_ROPE:]]
    return jnp.concatenate(cols, axis=2).astype(BF16)


def _mla_weights(w_uq, w_ukv):
    depth, rank, _ = w_uq.shape
    dq = MLA_NOPE + MLA_ROPE
    wq = w_uq.reshape(depth, rank, MLA_HEADS, dq)
    wuq = jnp.pad(wq, ((0, 0), (0, 0), (0, 0), (0, HEAD_PAD - dq))).reshape(depth, rank, MLA_HEADS * HEAD_PAD)
    kvr = w_ukv.shape[1]
    wkv = w_ukv.reshape(depth, kvr, MLA_HEADS, MLA_NOPE + MLA_V)
    wuk = wkv[..., :MLA_NOPE].reshape(depth, kvr, -1)
    wuvt = wkv[..., MLA_NOPE:].reshape(depth, kvr, -1).transpose(0, 2, 1)
    return wuq.astype(BF16), wuk.astype(BF16), wuvt.astype(BF16)


def _s5_weights(lam_re, lam_im, b_re, b_im, c_re, c_im, log_step):
    depth = lam_re.shape[0]
    step = jnp.exp(log_step)[..., None]
    mag = jnp.exp(lam_re * step)
    lbr = mag * jnp.cos(lam_im * step)
    lbi = mag * jnp.sin(lam_im * step)
    den = lam_re * lam_re + lam_im * lam_im
    cr = ((lbr - 1.0) * lam_re + lbi * lam_im) / den
    ci = (lbi * lam_re - (lbr - 1.0) * lam_im) / den
    bbr = cr[..., None] * b_re - ci[..., None] * b_im
    bbi = cr[..., None] * b_im + ci[..., None] * b_re
    gq = SSM_GROUPS // SSM_QUARTERS
    eye = jnp.eye(gq, dtype=F32)

    def blockdiag(m):
        m = m.reshape(depth, SSM_QUARTERS, gq, m.shape[2], m.shape[3]).transpose(0, 1, 2, 4, 3)
        out = m[:, :, :, :, None, :] * eye[None, None, :, None, :, None]
        return out.reshape(depth, SSM_QUARTERS, gq * m.shape[3], gq * m.shape[4])

    wb = jnp.concatenate([blockdiag(bbr), blockdiag(bbi)], axis=3).astype(BF16)
    wc = jnp.concatenate([blockdiag(c_re), -blockdiag(c_im)], axis=3).astype(BF16)
    lr = lbr.reshape(depth, 1, SSM_CH)
    li = lbi.reshape(depth, 1, SSM_CH)
    a = jnp.broadcast_to(lr, (depth, 8, SSM_CH))
    bsw = jnp.concatenate([jnp.broadcast_to(-li, (depth, 4, SSM_CH)), jnp.broadcast_to(li, (depth, 4, SSM_CH))],
                          axis=1)
    return wb, a, bsw, wc


def kernel(x, mem, positions, norm_mix, w_in, ssm_lam_re, ssm_lam_im, ssm_b_re, ssm_b_im, ssm_c_re, ssm_c_im, ssm_d, ssm_log_step, ssm_w_glu, mla_q_norm, mla_kv_norm, mla_w_uq, mla_w_ukv, mla_w_o, hg_lb, hg_g_norm, hg_w_o, w_out, norm_cross, norm_mem, x_w_q, x_w_kv, x_w_o, norm_ffn, ffn_w_gate_up, ffn_w_down, norm_final):
    bsz, seq, d_model = x.shape
    depth = w_in.shape[0]
    n_mem = mem.shape[1]
    assert bsz == 4, "the s5 scan packs 4 batches x (re, im) on the 8 sublanes"
    n_tok = bsz * seq

    half = MLA_ROPE // 2
    invf = (ROPE_THETA ** (-jnp.arange(half, dtype=F32) / half)).reshape(half, 1)
    lb_p = jax.nn.softmax(hg_lb.astype(F32), axis=0)
    lower_bounds = jnp.cumsum(lb_p, axis=0) - lb_p[0:1]

    rows = lambda a: a.reshape(depth, 1, -1)
    bf = lambda a: a.astype(BF16)
    w_inp = _in_proj_weights(w_in)
    wuq, wuk, wuvt = _mla_weights(mla_w_uq, mla_w_ukv)
    wb, a, bsw, wc = _s5_weights(ssm_lam_re, ssm_lam_im, ssm_b_re, ssm_b_im, ssm_c_re, ssm_c_im, ssm_log_step)
    wglu, wmo, who, wout = bf(ssm_w_glu), bf(mla_w_o), bf(hg_w_o), bf(w_out)
    wxq, wxkv, wxo, wgu, wdn = bf(x_w_q), bf(x_w_kv), bf(x_w_o), bf(ffn_w_gate_up), bf(ffn_w_down)
    g_mix, g_q, g_kv, g_hg = rows(norm_mix), rows(mla_q_norm), rows(mla_kv_norm), rows(hg_g_norm)
    g_cross, g_mem, g_ffn = rows(norm_cross), rows(norm_mem), rows(norm_ffn)
    lbs, ssm_dr = rows(lower_bounds), rows(ssm_d)

    x2 = x.reshape(n_tok, d_model)
    mem2 = mem.reshape(bsz * n_mem, d_model)
    pos2 = positions.reshape(n_tok // IN_TILE, 1, IN_TILE)

    for l in range(depth):
        u, q, k, v, hq, hf, hi, hg, gates = _in_proj(x2, pos2, g_mix, w_inp, invf, g_q, wuq, g_kv, wuk, wuvt,
                                                     lbs, l, bsz, seq)
        ys = _s5(u.reshape(bsz, seq, SSM_WIDTH), wb, a, bsw, wc, ssm_dr, wglu, l, d_model).reshape(n_tok, d_model)
        om = _mla(q, k, v)
        oh = _hgrn(hq, hf, hi, hg, g_hg, l, bsz, seq)
        kv = _memkv(mem2, g_mem, wxkv, l, n_mem)
        x2 = _mid(x2, ys, om, oh, gates, kv, wmo, who, wout, g_cross, wxq, wxo, l, seq, n_mem)
        x2 = _ffn(x2, g_ffn, wgu, wdn, norm_final.reshape(1, -1), l, final_norm=(l == depth - 1))
    return x2.reshape(bsz, seq, d_model)
```

```python
import functools
import math

import jax
import jax.numpy as jnp
from jax import lax
from jax.experimental import pallas as pl
from jax.experimental.pallas import tpu as pltpu

F32 = jnp.float32
BF16 = jnp.bfloat16

SUBLANES = 8
LANES = 128

RMS_EPS = 1e-6
LOG2E = 1.4426950408889634

SSM_GROUPS = 32
SSM_GROUP_CH = 16
SSM_STATE = 64
SSM_WIDTH = SSM_GROUPS * SSM_GROUP_CH
SSM_CH = SSM_GROUPS * SSM_STATE
SSM_QUARTERS = 4

MLA_HEADS = 8
MLA_Q_RANK = 512
MLA_KV_RANK = 256
MLA_NOPE = 64
MLA_ROPE = 32
MLA_V = 64
ROPE_THETA = 10000.0
HEAD_PAD = LANES
MLA_VT_ROWS = 80

HG_HEADS = 4
HG_D = 128
HG_WIDTH = HG_HEADS * HG_D
HG_CHUNK = 64
HG_SUB = SUBLANES
HG_STEP_CHUNKS = 8

X_HEADS = 4
X_HEAD_DIM = 128
X_WIDTH = X_HEADS * X_HEAD_DIM

N_BRANCH = 3

VMEM_LIMIT = 56 * 1024 * 1024

IN_TILE = 512
S5_TIME = 128
MLA_BLOCK = 512
MID_TILE = 512
FFN_TILE = 512
FFN_CHUNK = 2816


def _dot(a, b):
    return jnp.dot(a, b, preferred_element_type=F32)


def _dot_nt(a, b):
    return lax.dot_general(a, b, (((1,), (1,)), ((), ())), preferred_element_type=F32)


def _dot_tn(a, b):
    return lax.dot_general(a, b, (((0,), (0,)), ((), ())), preferred_element_type=F32)


def _rms(x, g):
    return x * lax.rsqrt(jnp.mean(x * x, axis=-1, keepdims=True) + RMS_EPS) * g


def _sigmoid(x):
    return 1.0 / (1.0 + jnp.exp(-x))


def _const_spec(shape):
    nd = len(shape)
    return pl.BlockSpec(shape, lambda *_: (0,) * nd, pipeline_mode=pl.Buffered(1))


def _layer_spec(shape, l):
    nd = len(shape)
    return pl.BlockSpec((None,) + tuple(shape[1:]), lambda *_: (l,) + (0,) * (nd - 1),
                        pipeline_mode=pl.Buffered(1))


def _params(sem):
    return pltpu.CompilerParams(dimension_semantics=sem, vmem_limit_bytes=VMEM_LIMIT)


C_SSM = 0
C_QLAT = C_SSM + SSM_WIDTH
C_KVLAT = C_QLAT + MLA_Q_RANK
C_KPE = C_KVLAT + MLA_KV_RANK
C_HQ = C_KPE + HEAD_PAD
C_HF = C_HQ + HG_WIDTH
C_HI = C_HF + HG_WIDTH
C_HG = C_HI + HG_WIDTH
C_GATE = C_HG + HG_WIDTH


def _in_proj_kernel(x_ref, pos_ref, g_ref, w_ref, invf_ref, qn_ref, wuq_ref, kvn_ref, wuk_ref, wuvt_ref, lb_ref,
                    u_ref, q_ref, k_ref, v_ref, hq_ref, hf_ref, hi_ref, hg_ref, gate_ref, *, q_scale):
    d_model = x_ref.shape[1]
    h = _rms(x_ref[...], g_ref[...]).astype(BF16)

    def seg(a, b):
        return _dot(h, w_ref[:, a:b])

    qlat = seg(C_QLAT, C_KVLAT)
    kvlat = seg(C_KVLAT, C_KPE)
    kp = seg(C_KPE, C_HQ)
    qn = _rms(qlat, qn_ref[...]).astype(BF16)
    kvn = _rms(kvlat, kvn_ref[...]).astype(BF16)

    t = x_ref.shape[0]
    ang = invf_ref[...] * pos_ref[0].astype(F32)
    cos_c = jnp.cos(ang)
    sin_c = jnp.sin(ang)
    fill = lambda v, n: jnp.full((n, t), v, F32)
    tail = HEAD_PAD - MLA_NOPE - MLA_ROPE
    cos = jnp.concatenate([fill(1.0, MLA_NOPE), cos_c, cos_c, fill(1.0, tail)], axis=0).T
    sin = jnp.concatenate([fill(0.0, MLA_NOPE), -sin_c, sin_c, fill(0.0, tail)], axis=0).T
    lane = lax.broadcasted_iota(jnp.int32, (1, HEAD_PAD), 1)
    first_half = lane < MLA_NOPE + MLA_ROPE // 2

    def rope(v):
        swapped = jnp.where(first_half, pltpu.roll(v, HEAD_PAD - MLA_ROPE // 2, 1),
                            pltpu.roll(v, MLA_ROPE // 2, 1))
        return v * cos + swapped * sin

    u_ref[...] = seg(C_SSM, C_QLAT)
    hq = seg(C_HQ, C_HF)
    hq_ref[...] = (hq * _sigmoid(hq)).astype(BF16)
    lb = lb_ref[...]
    hf_ref[...] = lb + (1.0 - lb) * _sigmoid(seg(C_HF, C_HI))
    hi_ref[...] = seg(C_HI, C_HG).astype(BF16)
    hg = seg(C_HG, C_GATE)
    hg_ref[...] = (hg * _sigmoid(hg)).astype(BF16)
    for br in range(N_BRANCH):
        gate_ref[:, br * d_model:(br + 1) * d_model] = _sigmoid(
            seg(C_GATE + br * d_model, C_GATE + (br + 1) * d_model)).astype(BF16)

    qa = _dot(qn, wuq_ref[...])
    for hh in range(MLA_HEADS):
        q_ref[0, hh] = (rope(qa[:, hh * HEAD_PAD:(hh + 1) * HEAD_PAD]) * q_scale).astype(BF16)

    kn = _dot(kvn, wuk_ref[...])
    vt = _dot_nt(wuvt_ref[...], kvn)
    kpe = rope(kp)
    nope_lanes = lane < MLA_NOPE
    ones_rows = (lax.broadcasted_iota(jnp.int32, (MLA_VT_ROWS - MLA_V, t), 0) == 0).astype(BF16)
    for hh in range(MLA_HEADS):
        piece = kn[:, (hh // 2) * HEAD_PAD:(hh // 2 + 1) * HEAD_PAD]
        if hh % 2:
            piece = pltpu.roll(piece, MLA_NOPE, 1)
        k_ref[0, hh] = jnp.where(nope_lanes, piece, kpe).astype(BF16)
        v_ref[0, hh, :MLA_V, :] = vt[hh * MLA_V:(hh + 1) * MLA_V].astype(BF16)
        v_ref[0, hh, MLA_V:, :] = ones_rows


def _in_proj(x2, pos2, g, w, invf, qn, wuq, kvn, wuk, wuvt, lb, l, bsz, seq):
    n_tok, d_model = x2.shape
    t = IN_TILE
    nt = seq // t
    tok = lambda w_: pl.BlockSpec((t, w_), lambda i: (i, 0))
    lay = lambda a: _layer_spec(a.shape, l)
    head = pl.BlockSpec((1, MLA_HEADS, t, HEAD_PAD), lambda i: (i // nt, 0, i % nt, 0))
    head_shape = jax.ShapeDtypeStruct((bsz, MLA_HEADS, seq, HEAD_PAD), BF16)
    head_t = pl.BlockSpec((1, MLA_HEADS, MLA_VT_ROWS, t), lambda i: (i // nt, 0, 0, i % nt))
    head_t_shape = jax.ShapeDtypeStruct((bsz, MLA_HEADS, MLA_VT_ROWS, seq), BF16)
    q_scale = LOG2E / math.sqrt(MLA_NOPE + MLA_ROPE)
    return pl.pallas_call(
        functools.partial(_in_proj_kernel, q_scale=q_scale),
        grid=(n_tok // t,),
        in_specs=[tok(d_model), pl.BlockSpec((1, 1, t), lambda i: (i, 0, 0)), lay(g), lay(w),
                  _const_spec(invf.shape), lay(qn), lay(wuq), lay(kvn), lay(wuk), lay(wuvt), lay(lb)],
        out_specs=[tok(SSM_WIDTH), head, head, head_t, tok(HG_WIDTH), tok(HG_WIDTH), tok(HG_WIDTH), tok(HG_WIDTH),
                   tok(N_BRANCH * d_model)],
        out_shape=[jax.ShapeDtypeStruct((n_tok, SSM_WIDTH), F32), head_shape, head_shape, head_t_shape,
                   jax.ShapeDtypeStruct((n_tok, HG_WIDTH), BF16), jax.ShapeDtypeStruct((n_tok, HG_WIDTH), F32),
                   jax.ShapeDtypeStruct((n_tok, HG_WIDTH), BF16), jax.ShapeDtypeStruct((n_tok, HG_WIDTH), BF16),
                   jax.ShapeDtypeStruct((n_tok, N_BRANCH * d_model), BF16)],
        compiler_params=_params(("parallel",)),
        name="in_proj",
    )(x2, pos2, g, w, invf, qn, wuq, kvn, wuk, wuvt, lb)


def _s5_kernel(u_ref, wb_ref, a_ref, bsw_ref, wc_ref, d_ref, wglu_ref, y_ref, se_ref, so_ref, st_ref, utb_ref,
               ytb_ref):
    bsz, tc, _ = u_ref.shape
    rows = bsz * tc
    d_model = y_ref.shape[2]
    qw = SSM_CH // SSM_QUARTERS
    uw = SSM_WIDTH // SSM_QUARTERS
    half = SUBLANES // 2

    @pl.when(pl.program_id(0) == 0)
    def _():
        st_ref[...] = jnp.zeros_like(st_ref)

    for q in range(SSM_QUARTERS):
        for b in range(bsz):
            utb_ref[q, pl.ds(b, tc, stride=bsz), :] = u_ref[b, :, q * uw:(q + 1) * uw]

    low = (lax.broadcasted_iota(jnp.int32, (rows, qw), 0) % SUBLANES) < half
    for q in range(SSM_QUARTERS):
        o = _dot(utb_ref[q].astype(BF16), wb_ref[q])
        re = o[:, :qw]
        im = o[:, qw:]
        se_ref[:, q * qw:(q + 1) * qw] = jnp.where(low, re, pltpu.roll(im, half, 0))
        so_ref[:, q * qw:(q + 1) * qw] = jnp.where(low, pltpu.roll(re, rows - half, 0), im)

    n_blk = 2
    cw = SSM_CH // n_blk
    for cb in range(n_blk):
        cs = slice(cb * cw, (cb + 1) * cw)
        a = a_ref[:, cs]
        bs = bsw_ref[:, cs]

        def body(v, s, cs=cs, a=a, bs=bs):
            r = pl.multiple_of(v * SUBLANES, SUBLANES)
            s = a * s + bs * pltpu.roll(s, half, 0) + se_ref[pl.ds(r, SUBLANES), cs]
            se_ref[pl.ds(r, SUBLANES), cs] = s
            s = a * s + bs * pltpu.roll(s, half, 0) + so_ref[pl.ds(r, SUBLANES), cs]
            so_ref[pl.ds(r, SUBLANES), cs] = s
            return s

        st_ref[:, cs] = lax.fori_loop(0, rows // SUBLANES, body, st_ref[:, cs], unroll=4)

    yw = uw
    low_y = low[:, :yw]
    pieces = []
    for q in range(SSM_QUARTERS):
        oe = _dot(se_ref[:, q * qw:(q + 1) * qw].astype(BF16), wc_ref[q])
        oo = _dot(so_ref[:, q * qw:(q + 1) * qw].astype(BF16), wc_ref[q])
        ye = oe[:, :yw] + pltpu.roll(oe[:, yw:], rows - half, 0)
        yo = pltpu.roll(oo[:, :yw], half, 0) + oo[:, yw:]
        pieces.append(jnp.where(low_y, ye, yo) + d_ref[:, q * uw:(q + 1) * uw] * utb_ref[q])
    y = jax.nn.gelu(jnp.concatenate(pieces, axis=1)).astype(BF16)
    z = _dot(y, wglu_ref[...])
    res = z[:, :d_model] * _sigmoid(z[:, d_model:])
    n_slab = d_model // uw
    for j in range(n_slab):
        ytb_ref[j] = res[:, j * uw:(j + 1) * uw]
    for j in range(n_slab):
        for b in range(bsz):
            y_ref[b, :, j * uw:(j + 1) * uw] = ytb_ref[j, pl.ds(b, tc, stride=bsz), :]


def _s5(u3, wb, a, bsw, wc, d, wglu, l, d_model):
    bsz, seq, _ = u3.shape
    tc = S5_TIME
    rows = tc * bsz
    lanes = SSM_WIDTH // SSM_QUARTERS
    return pl.pallas_call(
        _s5_kernel,
        grid=(seq // tc,),
        in_specs=[pl.BlockSpec((bsz, tc, SSM_WIDTH), lambda i: (0, i, 0))]
        + [_layer_spec(w_.shape, l) for w_ in (wb, a, bsw, wc, d, wglu)],
        out_specs=pl.BlockSpec((bsz, tc, d_model), lambda i: (0, i, 0)),
        out_shape=jax.ShapeDtypeStruct((bsz, seq, d_model), F32),
        scratch_shapes=[pltpu.VMEM((rows, SSM_CH), F32), pltpu.VMEM((rows, SSM_CH), F32),
                        pltpu.VMEM((SUBLANES, SSM_CH), F32), pltpu.VMEM((SSM_QUARTERS, rows, lanes), F32),
                        pltpu.VMEM((d_model // lanes, rows, lanes), F32)],
        compiler_params=_params(("arbitrary",)),
        name="s5",
    )(u3, wb, a, bsw, wc, d, wglu)


def _mla_kernel(qi_ref, kj_ref, q_ref, k_ref, vt_ref, o_ref, m_ref, acc_ref):
    p = pl.program_id(1)
    qi = qi_ref[p]
    kj = kj_ref[p]
    blk = q_ref.shape[2]

    @pl.when(kj == 0)
    def _():
        m_ref[...] = jnp.full_like(m_ref, -1e30)
        acc_ref[...] = jnp.zeros_like(acc_ref)

    def step(masked):
        if masked:
            key = lax.broadcasted_iota(jnp.int32, (blk, blk), 0)
            qry = lax.broadcasted_iota(jnp.int32, (blk, blk), 1)
            causal = key <= qry
        sc, mx = {}, {}
        for t in range(MLA_HEADS + 2):
            if t < MLA_HEADS:
                s = _dot_nt(k_ref[0, t], q_ref[0, t])
                sc[t] = jnp.where(causal, s, -1e30) if masked else s
            if 1 <= t <= MLA_HEADS:
                hh = t - 1
                m_prev = m_ref[hh]
                m_new = jnp.maximum(m_prev, jnp.max(sc[hh], axis=0, keepdims=True))
                mx[hh] = (m_new, jnp.exp2(m_prev - m_new))
                m_ref[hh] = m_new
            if t >= 2:
                hh = t - 2
                m_new, alpha = mx.pop(hh)
                pr = jnp.exp2(sc.pop(hh) - m_new).astype(BF16)
                acc_ref[hh] = alpha * acc_ref[hh] + _dot(vt_ref[0, hh], pr)

    @pl.when(kj < qi)
    def _():
        step(False)

    @pl.when(kj == qi)
    def _():
        step(True)
        for hh in range(MLA_HEADS):
            acc = acc_ref[hh]
            o_ref[0, hh] = (acc[:MLA_V, :] / acc[MLA_V:MLA_V + 1, :]).astype(BF16)


def _mla(q, k, vt):
    bsz, nh, seq, hd = q.shape
    blk = MLA_BLOCK
    nb = seq // blk
    pairs = [(i, j) for i in range(nb) for j in range(i + 1)]
    qi = jnp.asarray([p[0] for p in pairs], jnp.int32)
    kj = jnp.asarray([p[1] for p in pairs], jnp.int32)
    qspec = pl.BlockSpec((1, nh, blk, hd), lambda b, p, qi, kj: (b, 0, qi[p], 0))
    kspec = pl.BlockSpec((1, nh, blk, hd), lambda b, p, qi, kj: (b, 0, kj[p], 0))
    vspec = pl.BlockSpec((1, nh, MLA_VT_ROWS, blk), lambda b, p, qi, kj: (b, 0, 0, kj[p]))
    ospec = pl.BlockSpec((1, nh, MLA_V, blk), lambda b, p, qi, kj: (b, 0, 0, qi[p]))
    return pl.pallas_call(
        _mla_kernel,
        grid_spec=pltpu.PrefetchScalarGridSpec(
            num_scalar_prefetch=2,
            grid=(bsz, len(pairs)),
            in_specs=[qspec, kspec, vspec],
            out_specs=ospec,
            scratch_shapes=[pltpu.VMEM((nh, 1, blk), F32), pltpu.VMEM((nh, MLA_VT_ROWS, blk), F32)],
        ),
        out_shape=jax.ShapeDtypeStruct((bsz, nh, MLA_V, seq), BF16),
        compiler_params=_params(("parallel", "arbitrary")),
        name="mla",
    )(qi, kj, q, k, vt)


def _hgrn_kernel(q_ref, f_ref, v_ref, g_ref, gn_ref, o_ref, st_ref, ks_ref):
    c = HG_CHUNK
    sub = HG_SUB
    n_sub = c // sub

    @pl.when(pl.program_id(1) == 0)
    def _():
        st_ref[...] = jnp.zeros_like(st_ref)
        ks_ref[...] = jnp.zeros_like(ks_ref)

    chains = [(hh, ci) for hh in range(HG_HEADS) for ci in range(HG_STEP_CHUNKS)]

    def blk(ref, ch):
        hh, ci = ch
        return ref[ci * c:(ci + 1) * c, hh * HG_D:(hh + 1) * HG_D]

    row = lax.broadcasted_iota(jnp.int32, (c, c), 0)
    col = lax.broadcasted_iota(jnp.int32, (c, c), 1)
    tri = (col <= row).astype(BF16)
    lane = lax.broadcasted_iota(jnp.int32, (sub, HG_D), 1)
    trow = lax.broadcasted_iota(jnp.int32, (sub, HG_D), 0)

    q, k, b, bk = {}, {}, {}, {}
    for ch in chains:
        f = blk(f_ref, ch)
        k[ch] = 1.0 - f
        lf = jnp.log(f) * LOG2E
        l1 = lf.astype(BF16)
        r1 = lf - l1.astype(F32)
        l2 = r1.astype(BF16)
        l3 = (r1 - l2.astype(F32)).astype(BF16)
        b[ch] = _dot(tri, l1) + _dot(tri, l2) + _dot(tri, l3)
    for ch in chains:
        q[ch] = blk(q_ref, ch).astype(F32)
        bk[ch] = b[ch] - jnp.log(k[ch]) * LOG2E

    qb, kdec, ebl, o_inter = {}, {}, {}, {}
    for ch in chains:
        bl = b[ch][c - 1:c, :]
        qb[ch] = (q[ch] * jnp.exp2(b[ch])).astype(BF16)
        kdec[ch] = (k[ch] * jnp.exp2(bl - b[ch])).astype(BF16)
        ebl[ch] = jnp.exp2(bl)
    for hh in range(HG_HEADS):
        st = st_ref[hh]
        for ci in range(HG_STEP_CHUNKS):
            ch = (hh, ci)
            o_inter[ch] = _dot_nt(qb[ch], st.astype(BF16))
            st = st * ebl[ch] + _dot_tn(blk(v_ref, ch), kdec[ch])
        st_ref[hh] = st

    a_rows = {ch: [] for ch in chains}
    for i in range(n_sub):
        rs = slice(i * sub, (i + 1) * sub)
        tiles = {ch: jnp.zeros((sub, HG_D), F32) for ch in chains}
        for ss in range(sub):
            s = i * sub + ss
            for ch in chains:
                kdec_s = jnp.exp2(jnp.minimum(b[ch][rs] - bk[ch][s:s + 1, :], 0.0))
                a_s = jnp.sum(q[ch][rs] * kdec_s, axis=-1, keepdims=True)
                tiles[ch] = jnp.where(lane == s, a_s, tiles[ch])
        for ch in chains:
            a_rows[ch].append(jnp.where(trow + i * sub >= lane, tiles[ch], 0.0)[:, :c])
    for i in range(1, n_sub):
        rs = slice(i * sub, (i + 1) * sub)
        for n, ch in enumerate(chains):
            r = b[ch][i * sub - 1:i * sub, :]
            ks_ref[n, i - 1, :i * sub, :] = k[ch][:i * sub] * jnp.exp2(r - b[ch][:i * sub])
            qs = (q[ch][rs] * jnp.exp2(b[ch][rs] - r)).astype(BF16)
            a_rows[ch][i] = a_rows[ch][i] + _dot_nt(qs, ks_ref[n, i - 1].astype(BF16))
    o = {}
    for ch in chains:
        a = jnp.concatenate(a_rows[ch], axis=0).astype(BF16)
        o[ch] = o_inter[ch] + _dot(a, blk(v_ref, ch))

    gn = gn_ref[...]
    for ch in chains:
        hh, ci = ch
        on = o[ch] * lax.rsqrt(jnp.mean(o[ch] * o[ch], axis=-1, keepdims=True) + RMS_EPS) * gn
        o_ref[ci * c:(ci + 1) * c, hh * HG_D:(hh + 1) * HG_D] = (on * blk(g_ref, ch).astype(F32)).astype(BF16)


def _hgrn(hq, hf, hi, hg, gn, l, bsz, seq):
    n_tok = hq.shape[0]
    c = HG_CHUNK
    rows = c * HG_STEP_CHUNKS
    nc = seq // rows
    spec = pl.BlockSpec((rows, HG_WIDTH), lambda b, i: (b * nc + i, 0))
    return pl.pallas_call(
        _hgrn_kernel,
        grid=(bsz, nc),
        in_specs=[spec, spec, spec, spec, _layer_spec(gn.shape, l)],
        out_specs=spec,
        out_shape=jax.ShapeDtypeStruct((n_tok, HG_WIDTH), BF16),
        scratch_shapes=[pltpu.VMEM((HG_HEADS, HG_D, HG_D), F32),
                        pltpu.VMEM((HG_HEADS * HG_STEP_CHUNKS, c // HG_SUB - 1, c, HG_D), F32)],
        compiler_params=_params(("parallel", "arbitrary")),
        name="hgrn",
    )(hq, hf, hi, hg, gn)


def _memkv_kernel(mem_ref, g_ref, w_ref, kv_ref):
    kv_ref[...] = _dot(_rms(mem_ref[...], g_ref[...]).astype(BF16), w_ref[...]).astype(BF16)


def _memkv(mem2, g, w, l, n_mem):
    n_rows, d_model = mem2.shape
    return pl.pallas_call(
        _memkv_kernel,
        grid=(n_rows // n_mem,),
        in_specs=[pl.BlockSpec((n_mem, d_model), lambda i: (i, 0)), _layer_spec(g.shape, l), _layer_spec(w.shape, l)],
        out_specs=pl.BlockSpec((n_mem, 2 * X_WIDTH), lambda i: (i, 0)),
        out_shape=jax.ShapeDtypeStruct((n_rows, 2 * X_WIDTH), BF16),
        compiler_params=_params(("parallel",)),
        name="memkv",
    )(mem2, g, w)


def _mid_kernel(x_ref, ys_ref, om_ref, oh_ref, gate_ref, kv_ref, wmo_ref, who_ref, wout_ref, gx_ref, wq_ref,
                wxo_ref, o_ref, *, x_scale):
    d_model = x_ref.shape[1]
    om_t = om_ref[0].reshape(MLA_HEADS * MLA_V, x_ref.shape[0])
    y_mla = _dot_tn(om_t, wmo_ref[...])
    y_hg = _dot(oh_ref[...], who_ref[...])
    gate = lambda br: gate_ref[:, br * d_model:(br + 1) * d_model].astype(F32)
    merged = gate(0) * ys_ref[...] + gate(1) * y_mla + gate(2) * y_hg
    x = x_ref[...] + _dot(merged.astype(BF16), wout_ref[...])

    q = (_dot(_rms(x, gx_ref[...]).astype(BF16), wq_ref[...]) * x_scale).astype(BF16)
    outs = []
    for hh in range(X_HEADS):
        sl = slice(hh * X_HEAD_DIM, (hh + 1) * X_HEAD_DIM)
        s = _dot_nt(q[:, sl], kv_ref[:, sl])
        pr = jnp.exp2(s - jnp.max(s, axis=-1, keepdims=True))
        den = jnp.sum(pr, axis=-1, keepdims=True)
        vh = kv_ref[:, X_WIDTH + hh * X_HEAD_DIM:X_WIDTH + (hh + 1) * X_HEAD_DIM]
        outs.append((_dot(pr.astype(BF16), vh) / den).astype(BF16))
    o_ref[...] = x + _dot(jnp.concatenate(outs, axis=1), wxo_ref[...])


def _mid(x2, ys, om, oh, gates, kv, wmo, who, wout, gx, wq, wxo, l, seq, n_mem):
    n_tok, d_model = x2.shape
    t = MID_TILE
    nt = seq // t
    tok = lambda w_: pl.BlockSpec((t, w_), lambda i: (i, 0))
    return pl.pallas_call(
        functools.partial(_mid_kernel, x_scale=LOG2E / math.sqrt(X_HEAD_DIM)),
        grid=(n_tok // t,),
        in_specs=[tok(d_model), tok(d_model),
                  pl.BlockSpec((1, MLA_HEADS, MLA_V, t), lambda i: (i // nt, 0, 0, i % nt)),
                  tok(HG_WIDTH), tok(N_BRANCH * d_model),
                  pl.BlockSpec((n_mem, 2 * X_WIDTH), lambda i: (i // nt, 0)),
                  ] + [_layer_spec(w_.shape, l) for w_ in (wmo, who, wout, gx, wq, wxo)],
        out_specs=tok(d_model),
        out_shape=jax.ShapeDtypeStruct((n_tok, d_model), F32),
        compiler_params=_params(("parallel",)),
        name="mid",
    )(x2, ys, om, oh, gates, kv, wmo, who, wout, gx, wq, wxo)


def _ffn_kernel(x_ref, g_ref, wgu_ref, wd_ref, gf_ref, o_ref, *, final_norm):
    d_ff = wd_ref.shape[0]
    x = x_ref[...]
    h = _rms(x, g_ref[...]).astype(BF16)
    y = x
    for c0 in range(0, d_ff, FFN_CHUNK):
        gt = _dot(h, wgu_ref[:, c0:c0 + FFN_CHUNK])
        up = _dot(h, wgu_ref[:, d_ff + c0:d_ff + c0 + FFN_CHUNK])
        act = (gt * _sigmoid(gt) * up).astype(BF16)
        y = y + _dot(act, wd_ref[c0:c0 + FFN_CHUNK, :])
    if final_norm:
        y = _rms(y, gf_ref[...])
    o_ref[...] = y


def _ffn(x2, g, wgu, wd, gf, l, final_norm):
    n_tok, d_model = x2.shape
    t = FFN_TILE
    tok = pl.BlockSpec((t, d_model), lambda i: (i, 0))
    return pl.pallas_call(
        functools.partial(_ffn_kernel, final_norm=final_norm),
        grid=(n_tok // t,),
        in_specs=[tok, _layer_spec(g.shape, l), _layer_spec(wgu.shape, l), _layer_spec(wd.shape, l),
                  _const_spec(gf.shape)],
        out_specs=tok,
        out_shape=jax.ShapeDtypeStruct((n_tok, d_model), F32),
        compiler_params=_params(("parallel",)),
        name="ffn",
    )(x2, g, wgu, wd, gf)


def _in_proj_weights(w_in):
    kpe = jnp.pad(w_in[:, :, C_KPE:C_KPE + MLA_ROPE], ((0, 0), (0, 0), (MLA_NOPE, HEAD_PAD - MLA_NOPE - MLA_ROPE)))
    cols = [w_in[:, :, :C_KPE], kpe, w_in[:, :, C_KPE + MLA_ROPE:]]
    return jnp.concatenate(cols, axis=2).astype(BF16)


def _mla_weights(w_uq, w_ukv):
    depth, rank, _ = w_uq.shape
    dq = MLA_NOPE + MLA_ROPE
    wq = w_uq.reshape(depth, rank, MLA_HEADS, dq)
    wuq = jnp.pad(wq, ((0, 0), (0, 0), (0, 0), (0, HEAD_PAD - dq))).reshape(depth, rank, MLA_HEADS * HEAD_PAD)
    kvr = w_ukv.shape[1]
    wkv = w_ukv.reshape(depth, kvr, MLA_HEADS, MLA_NOPE + MLA_V)
    wuk = wkv[..., :MLA_NOPE].reshape(depth, kvr, -1)
    wuvt = wkv[..., MLA_NOPE:].reshape(depth, kvr, -1).transpose(0, 2, 1)
    return wuq.astype(BF16), wuk.astype(BF16), wuvt.astype(BF16)


def _s5_weights(lam_re, lam_im, b_re, b_im, c_re, c_im, log_step):
    depth = lam_re.shape[0]
    step = jnp.exp(log_step)[..., None]
    mag = jnp.exp(lam_re * step)
    lbr = mag * jnp.cos(lam_im * step)
    lbi = mag * jnp.sin(lam_im * step)
    den = lam_re * lam_re + lam_im * lam_im
    cr = ((lbr - 1.0) * lam_re + lbi * lam_im) / den
    ci = (lbi * lam_re - (lbr - 1.0) * lam_im) / den
    bbr = cr[..., None] * b_re - ci[..., None] * b_im
    bbi = cr[..., None] * b_im + ci[..., None] * b_re
    gq = SSM_GROUPS // SSM_QUARTERS
    eye = jnp.eye(gq, dtype=F32)

    def blockdiag(m):
        m = m.reshape(depth, SSM_QUARTERS, gq, m.shape[2], m.shape[3]).transpose(0, 1, 2, 4, 3)
        out = m[:, :, :, :, None, :] * eye[None, None, :, None, :, None]
        return out.reshape(depth, SSM_QUARTERS, gq * m.shape[3], gq * m.shape[4])

    wb = jnp.concatenate([blockdiag(bbr), blockdiag(bbi)], axis=3).astype(BF16)
    wc = jnp.concatenate([blockdiag(c_re), -blockdiag(c_im)], axis=3).astype(BF16)
    lr = lbr.reshape(depth, 1, SSM_CH)
    li = lbi.reshape(depth, 1, SSM_CH)
    half = SUBLANES // 2
    a = jnp.broadcast_to(lr, (depth, SUBLANES, SSM_CH))
    bsw = jnp.concatenate([jnp.broadcast_to(-li, (depth, half, SSM_CH)), jnp.broadcast_to(li, (depth, half, SSM_CH))],
                          axis=1)
    return wb, a, bsw, wc


def kernel(x, mem, positions, norm_mix, w_in, ssm_lam_re, ssm_lam_im, ssm_b_re, ssm_b_im, ssm_c_re, ssm_c_im, ssm_d, ssm_log_step, ssm_w_glu, mla_q_norm, mla_kv_norm, mla_w_uq, mla_w_ukv, mla_w_o, hg_lb, hg_g_norm, hg_w_o, w_out, norm_cross, norm_mem, x_w_q, x_w_kv, x_w_o, norm_ffn, ffn_w_gate_up, ffn_w_down, norm_final):
    bsz, seq, d_model = x.shape
    depth = w_in.shape[0]
    n_mem = mem.shape[1]
    assert 2 * bsz == SUBLANES, "the s5 scan packs (re, im) x batch on the sublanes of one tile"
    n_tok = bsz * seq

    half = MLA_ROPE // 2
    invf = (ROPE_THETA ** (-jnp.arange(half, dtype=F32) / half)).reshape(half, 1)
    lb_p = jax.nn.softmax(hg_lb.astype(F32), axis=0)
    lower_bounds = jnp.cumsum(lb_p, axis=0) - lb_p[0:1]

    rows = lambda a: a.reshape(depth, 1, -1)
    bf = lambda a: a.astype(BF16)
    w_inp = _in_proj_weights(w_in)
    wuq, wuk, wuvt = _mla_weights(mla_w_uq, mla_w_ukv)
    wb, a, bsw, wc = _s5_weights(ssm_lam_re, ssm_lam_im, ssm_b_re, ssm_b_im, ssm_c_re, ssm_c_im, ssm_log_step)
    wglu, wmo, who, wout = bf(ssm_w_glu), bf(mla_w_o), bf(hg_w_o), bf(w_out)
    wxq, wxkv, wxo, wgu, wdn = bf(x_w_q), bf(x_w_kv), bf(x_w_o), bf(ffn_w_gate_up), bf(ffn_w_down)
    g_mix, g_q, g_kv, g_hg = rows(norm_mix), rows(mla_q_norm), rows(mla_kv_norm), rows(hg_g_norm)
    g_cross, g_mem, g_ffn = rows(norm_cross), rows(norm_mem), rows(norm_ffn)
    lbs, ssm_dr = rows(lower_bounds), rows(ssm_d)

    x2 = x.reshape(n_tok, d_model)
    mem2 = mem.reshape(bsz * n_mem, d_model)
    pos2 = positions.reshape(n_tok // IN_TILE, 1, IN_TILE)

    for l in range(depth):
        u, q, k, v, hq, hf, hi, hg, gates = _in_proj(x2, pos2, g_mix, w_inp, invf, g_q, wuq, g_kv, wuk, wuvt,
                                                     lbs, l, bsz, seq)
        ys = _s5(u.reshape(bsz, seq, SSM_WIDTH), wb, a, bsw, wc, ssm_dr, wglu, l, d_model).reshape(n_tok, d_model)
        om = _mla(q, k, v)
        oh = _hgrn(hq, hf, hi, hg, g_hg, l, bsz, seq)
        kv = _memkv(mem2, g_mem, wxkv, l, n_mem)
        x2 = _mid(x2, ys, om, oh, gates, kv, wmo, who, wout, g_cross, wxq, wxo, l, seq, n_mem)
        x2 = _ffn(x2, g_ffn, wgu, wdn, norm_final.reshape(1, -1), l, final_norm=(l == depth - 1))
    return x2.reshape(bsz, seq, d_model)
```

```python
import functools
import math

import jax
import jax.numpy as jnp
from jax import lax
from jax.experimental import pallas as pl
from jax.experimental.pallas import tpu as pltpu

F32 = jnp.float32
BF16 = jnp.bfloat16

SUBLANES = 8
LANES = 128

RMS_EPS = 1e-6
LOG2E = 1.4426950408889634

SSM_GROUPS = 32
SSM_GROUP_CH = 16
SSM_STATE = 64
SSM_WIDTH = SSM_GROUPS * SSM_GROUP_CH
SSM_CH = SSM_GROUPS * SSM_STATE
SSM_QUARTERS = 4

MLA_HEADS = 8
MLA_Q_RANK = 512
MLA_KV_RANK = 256
MLA_NOPE = 64
MLA_ROPE = 32
MLA_V = 64
ROPE_THETA = 10000.0
HEAD_PAD = LANES
MLA_VT_ROWS = 80

HG_HEADS = 4
HG_D = 128
HG_WIDTH = HG_HEADS * HG_D
HG_CHUNK = 64
HG_SUB = SUBLANES
HG_STEP_CHUNKS = 8

X_HEADS = 4
X_HEAD_DIM = 128
X_WIDTH = X_HEADS * X_HEAD_DIM

N_BRANCH = 3

VMEM_LIMIT = 56 * 1024 * 1024

IN_TILE = 512
S5_TIME = 128
MLA_BLOCK = 512
MID_TILE = 512
FFN_TILE = 512
FFN_CHUNK = 2816


def _dot(a, b):
    return jnp.dot(a, b, preferred_element_type=F32)


def _dot_nt(a, b):
    return lax.dot_general(a, b, (((1,), (1,)), ((), ())), preferred_element_type=F32)


def _dot_tn(a, b):
    return lax.dot_general(a, b, (((0,), (0,)), ((), ())), preferred_element_type=F32)


def _rms(x, g):
    return x * lax.rsqrt(jnp.mean(x * x, axis=-1, keepdims=True) + RMS_EPS) * g


def _sigmoid(x):
    return 1.0 / (1.0 + jnp.exp(-x))


def _const_spec(shape):
    nd = len(shape)
    return pl.BlockSpec(shape, lambda *_: (0,) * nd, pipeline_mode=pl.Buffered(1))


def _layer_spec(shape, l):
    nd = len(shape)
    return pl.BlockSpec((None,) + tuple(shape[1:]), lambda *_: (l,) + (0,) * (nd - 1),
                        pipeline_mode=pl.Buffered(1))


def _params(sem):
    return pltpu.CompilerParams(dimension_semantics=sem, vmem_limit_bytes=VMEM_LIMIT)


C_SSM = 0
C_QLAT = C_SSM + SSM_WIDTH
C_KVLAT = C_QLAT + MLA_Q_RANK
C_KPE = C_KVLAT + MLA_KV_RANK
C_HQ = C_KPE + HEAD_PAD
C_HF = C_HQ + HG_WIDTH
C_HI = C_HF + HG_WIDTH
C_HG = C_HI + HG_WIDTH
C_GATE = C_HG + HG_WIDTH


def _in_proj_kernel(x_ref, pos_ref, g_ref, w_ref, invf_ref, qn_ref, wuq_ref, kvn_ref, wuk_ref, wuvt_ref, lb_ref,
                    u_ref, q_ref, k_ref, v_ref, hq_ref, hf_ref, hi_ref, hg_ref, gate_ref, *, q_scale):
    d_model = x_ref.shape[1]
    h = _rms(x_ref[...], g_ref[...]).astype(BF16)

    def seg(a, b):
        return _dot(h, w_ref[:, a:b])

    qlat = seg(C_QLAT, C_KVLAT)
    kvlat = seg(C_KVLAT, C_KPE)
    kp = seg(C_KPE, C_HQ)
    qn = _rms(qlat, qn_ref[...]).astype(BF16)
    kvn = _rms(kvlat, kvn_ref[...]).astype(BF16)

    t = x_ref.shape[0]
    ang = invf_ref[...] * pos_ref[0].astype(F32)
    cos_c = jnp.cos(ang)
    sin_c = jnp.sin(ang)
    fill = lambda v, n: jnp.full((n, t), v, F32)
    tail = HEAD_PAD - MLA_NOPE - MLA_ROPE
    cos = jnp.concatenate([fill(1.0, MLA_NOPE), cos_c, cos_c, fill(1.0, tail)], axis=0).T
    sin = jnp.concatenate([fill(0.0, MLA_NOPE), -sin_c, sin_c, fill(0.0, tail)], axis=0).T
    lane = lax.broadcasted_iota(jnp.int32, (1, HEAD_PAD), 1)
    first_half = lane < MLA_NOPE + MLA_ROPE // 2

    def rope(v):
        swapped = jnp.where(first_half, pltpu.roll(v, HEAD_PAD - MLA_ROPE // 2, 1),
                            pltpu.roll(v, MLA_ROPE // 2, 1))
        return v * cos + swapped * sin

    u_ref[...] = seg(C_SSM, C_QLAT)
    hq = seg(C_HQ, C_HF)
    hq_ref[...] = (hq * _sigmoid(hq)).astype(BF16)
    lb = lb_ref[...]
    hf_ref[...] = lb + (1.0 - lb) * _sigmoid(seg(C_HF, C_HI))
    hi_ref[...] = seg(C_HI, C_HG).astype(BF16)
    hg = seg(C_HG, C_GATE)
    hg_ref[...] = (hg * _sigmoid(hg)).astype(BF16)

    qa = _dot(qn, wuq_ref[...])
    for hh in range(MLA_HEADS):
        q_ref[0, hh] = (rope(qa[:, hh * HEAD_PAD:(hh + 1) * HEAD_PAD]) * q_scale).astype(BF16)

    kn = _dot(kvn, wuk_ref[...])
    vt = _dot_nt(wuvt_ref[...], kvn)
    kpe = rope(kp)
    nope_lanes = lane < MLA_NOPE
    ones_rows = (lax.broadcasted_iota(jnp.int32, (MLA_VT_ROWS - MLA_V, t), 0) == 0).astype(BF16)
    for hh in range(MLA_HEADS):
        piece = kn[:, (hh // 2) * HEAD_PAD:(hh // 2 + 1) * HEAD_PAD]
        if hh % 2:
            piece = pltpu.roll(piece, MLA_NOPE, 1)
        k_ref[0, hh] = jnp.where(nope_lanes, piece, kpe).astype(BF16)
        v_ref[0, hh, :MLA_V, :] = vt[hh * MLA_V:(hh + 1) * MLA_V].astype(BF16)
        v_ref[0, hh, MLA_V:, :] = ones_rows

    for br in range(N_BRANCH):
        gate_ref[:, br * d_model:(br + 1) * d_model] = _sigmoid(
            seg(C_GATE + br * d_model, C_GATE + (br + 1) * d_model)).astype(BF16)


def _in_proj(x2, pos2, g, w, invf, qn, wuq, kvn, wuk, wuvt, lb, l, bsz, seq):
    n_tok, d_model = x2.shape
    t = IN_TILE
    nt = seq // t
    tok = lambda w_: pl.BlockSpec((t, w_), lambda i: (i, 0))
    lay = lambda a: _layer_spec(a.shape, l)
    head = pl.BlockSpec((1, MLA_HEADS, t, HEAD_PAD), lambda i: (i // nt, 0, i % nt, 0))
    head_shape = jax.ShapeDtypeStruct((bsz, MLA_HEADS, seq, HEAD_PAD), BF16)
    head_t = pl.BlockSpec((1, MLA_HEADS, MLA_VT_ROWS, t), lambda i: (i // nt, 0, 0, i % nt))
    head_t_shape = jax.ShapeDtypeStruct((bsz, MLA_HEADS, MLA_VT_ROWS, seq), BF16)
    q_scale = LOG2E / math.sqrt(MLA_NOPE + MLA_ROPE)
    return pl.pallas_call(
        functools.partial(_in_proj_kernel, q_scale=q_scale),
        grid=(n_tok // t,),
        in_specs=[tok(d_model), pl.BlockSpec((1, 1, t), lambda i: (i, 0, 0)), lay(g), lay(w),
                  _const_spec(invf.shape), lay(qn), lay(wuq), lay(kvn), lay(wuk), lay(wuvt), lay(lb)],
        out_specs=[tok(SSM_WIDTH), head, head, head_t, tok(HG_WIDTH), tok(HG_WIDTH), tok(HG_WIDTH), tok(HG_WIDTH),
                   tok(N_BRANCH * d_model)],
        out_shape=[jax.ShapeDtypeStruct((n_tok, SSM_WIDTH), F32), head_shape, head_shape, head_t_shape,
                   jax.ShapeDtypeStruct((n_tok, HG_WIDTH), BF16), jax.ShapeDtypeStruct((n_tok, HG_WIDTH), F32),
                   jax.ShapeDtypeStruct((n_tok, HG_WIDTH), BF16), jax.ShapeDtypeStruct((n_tok, HG_WIDTH), BF16),
                   jax.ShapeDtypeStruct((n_tok, N_BRANCH * d_model), BF16)],
        compiler_params=_params(("parallel",)),
        name="in_proj",
    )(x2, pos2, g, w, invf, qn, wuq, kvn, wuk, wuvt, lb)


def _s5_kernel(u_ref, wb_ref, a_ref, bsw_ref, wc_ref, d_ref, wglu_ref, y_ref, se_ref, so_ref, st_ref, utb_ref,
               ytb_ref):
    bsz, tc, _ = u_ref.shape
    rows = bsz * tc
    d_model = y_ref.shape[2]
    qw = SSM_CH // SSM_QUARTERS
    uw = SSM_WIDTH // SSM_QUARTERS
    half = SUBLANES // 2

    @pl.when(pl.program_id(0) == 0)
    def _():
        st_ref[...] = jnp.zeros_like(st_ref)

    for q in range(SSM_QUARTERS):
        for b in range(bsz):
            utb_ref[q, pl.ds(b, tc, stride=bsz), :] = u_ref[b, :, q * uw:(q + 1) * uw]

    low = (lax.broadcasted_iota(jnp.int32, (rows, qw), 0) % SUBLANES) < half
    for q in range(SSM_QUARTERS):
        o = _dot(utb_ref[q].astype(BF16), wb_ref[q])
        re = o[:, :qw]
        im = o[:, qw:]
        se_ref[:, q * qw:(q + 1) * qw] = jnp.where(low, re, pltpu.roll(im, half, 0))
        so_ref[:, q * qw:(q + 1) * qw] = jnp.where(low, pltpu.roll(re, rows - half, 0), im)

    n_blk = 2
    cw = SSM_CH // n_blk
    for cb in range(n_blk):
        cs = slice(cb * cw, (cb + 1) * cw)
        a = a_ref[:, cs]
        bs = bsw_ref[:, cs]

        def body(v, s, cs=cs, a=a, bs=bs):
            r = pl.multiple_of(v * SUBLANES, SUBLANES)
            s = a * s + bs * pltpu.roll(s, half, 0) + se_ref[pl.ds(r, SUBLANES), cs]
            se_ref[pl.ds(r, SUBLANES), cs] = s
            s = a * s + bs * pltpu.roll(s, half, 0) + so_ref[pl.ds(r, SUBLANES), cs]
            so_ref[pl.ds(r, SUBLANES), cs] = s
            return s

        st_ref[:, cs] = lax.fori_loop(0, rows // SUBLANES, body, st_ref[:, cs], unroll=4)

    yw = uw
    low_y = low[:, :yw]
    pieces = []
    for q in range(SSM_QUARTERS):
        oe = _dot(se_ref[:, q * qw:(q + 1) * qw].astype(BF16), wc_ref[q])
        oo = _dot(so_ref[:, q * qw:(q + 1) * qw].astype(BF16), wc_ref[q])
        ye = oe[:, :yw] + pltpu.roll(oe[:, yw:], rows - half, 0)
        yo = pltpu.roll(oo[:, :yw], half, 0) + oo[:, yw:]
        pieces.append(jnp.where(low_y, ye, yo) + d_ref[:, q * uw:(q + 1) * uw] * utb_ref[q])
    y = jax.nn.gelu(jnp.concatenate(pieces, axis=1)).astype(BF16)
    z = _dot(y, wglu_ref[...])
    res = z[:, :d_model] * _sigmoid(z[:, d_model:])
    n_slab = d_model // uw
    for j in range(n_slab):
        ytb_ref[j] = res[:, j * uw:(j + 1) * uw]
    for j in range(n_slab):
        for b in range(bsz):
            y_ref[b, :, j * uw:(j + 1) * uw] = ytb_ref[j, pl.ds(b, tc, stride=bsz), :]


def _s5(u3, wb, a, bsw, wc, d, wglu, l, d_model):
    bsz, seq, _ = u3.shape
    tc = S5_TIME
    rows = tc * bsz
    lanes = SSM_WIDTH // SSM_QUARTERS
    return pl.pallas_call(
        _s5_kernel,
        grid=(seq // tc,),
        in_specs=[pl.BlockSpec((bsz, tc, SSM_WIDTH), lambda i: (0, i, 0))]
        + [_layer_spec(w_.shape, l) for w_ in (wb, a, bsw, wc, d, wglu)],
        out_specs=pl.BlockSpec((bsz, tc, d_model), lambda i: (0, i, 0)),
        out_shape=jax.ShapeDtypeStruct((bsz, seq, d_model), F32),
        scratch_shapes=[pltpu.VMEM((rows, SSM_CH), F32), pltpu.VMEM((rows, SSM_CH), F32),
                        pltpu.VMEM((SUBLANES, SSM_CH), F32), pltpu.VMEM((SSM_QUARTERS, rows, lanes), F32),
                        pltpu.VMEM((d_model // lanes, rows, lanes), F32)],
        compiler_params=_params(("arbitrary",)),
        name="s5",
    )(u3, wb, a, bsw, wc, d, wglu)


def _mla_kernel(qi_ref, kj_ref, q_ref, k_ref, vt_ref, o_ref, m_ref, acc_ref):
    p = pl.program_id(1)
    qi = qi_ref[p]
    kj = kj_ref[p]
    blk = q_ref.shape[2]

    @pl.when(kj == 0)
    def _():
        m_ref[...] = jnp.full_like(m_ref, -1e30)
        acc_ref[...] = jnp.zeros_like(acc_ref)

    def step(masked):
        if masked:
            key = lax.broadcasted_iota(jnp.int32, (blk, blk), 0)
            qry = lax.broadcasted_iota(jnp.int32, (blk, blk), 1)
            causal = key <= qry
        sc, mx = {}, {}
        for t in range(MLA_HEADS + 2):
            if t < MLA_HEADS:
                s = _dot_nt(k_ref[0, t], q_ref[0, t])
                sc[t] = jnp.where(causal, s, -1e30) if masked else s
            if 1 <= t <= MLA_HEADS:
                hh = t - 1
                m_prev = m_ref[hh]
                m_new = jnp.maximum(m_prev, jnp.max(sc[hh], axis=0, keepdims=True))
                mx[hh] = (m_new, jnp.exp2(m_prev - m_new))
                m_ref[hh] = m_new
            if t >= 2:
                hh = t - 2
                m_new, alpha = mx.pop(hh)
                pr = jnp.exp2(sc.pop(hh) - m_new).astype(BF16)
                acc_ref[hh] = alpha * acc_ref[hh] + _dot(vt_ref[0, hh], pr)

    @pl.when(kj < qi)
    def _():
        step(False)

    @pl.when(kj == qi)
    def _():
        step(True)
        for hh in range(MLA_HEADS):
            acc = acc_ref[hh]
            o_ref[0, hh] = (acc[:MLA_V, :] / acc[MLA_V:MLA_V + 1, :]).astype(BF16)


def _mla(q, k, vt):
    bsz, nh, seq, hd = q.shape
    blk = MLA_BLOCK
    nb = seq // blk
    pairs = [(i, j) for i in range(nb) for j in range(i + 1)]
    qi = jnp.asarray([p[0] for p in pairs], jnp.int32)
    kj = jnp.asarray([p[1] for p in pairs], jnp.int32)
    qspec = pl.BlockSpec((1, nh, blk, hd), lambda b, p, qi, kj: (b, 0, qi[p], 0))
    kspec = pl.BlockSpec((1, nh, blk, hd), lambda b, p, qi, kj: (b, 0, kj[p], 0))
    vspec = pl.BlockSpec((1, nh, MLA_VT_ROWS, blk), lambda b, p, qi, kj: (b, 0, 0, kj[p]))
    ospec = pl.BlockSpec((1, nh, MLA_V, blk), lambda b, p, qi, kj: (b, 0, 0, qi[p]))
    return pl.pallas_call(
        _mla_kernel,
        grid_spec=pltpu.PrefetchScalarGridSpec(
            num_scalar_prefetch=2,
            grid=(bsz, len(pairs)),
            in_specs=[qspec, kspec, vspec],
            out_specs=ospec,
            scratch_shapes=[pltpu.VMEM((nh, 1, blk), F32), pltpu.VMEM((nh, MLA_VT_ROWS, blk), F32)],
        ),
        out_shape=jax.ShapeDtypeStruct((bsz, nh, MLA_V, seq), BF16),
        compiler_params=_params(("parallel", "arbitrary")),
        name="mla",
    )(qi, kj, q, k, vt)


def _hgrn_kernel(q_ref, f_ref, v_ref, g_ref, gn_ref, o_ref, st_ref, ks_ref):
    c = HG_CHUNK
    sub = HG_SUB
    n_sub = c // sub

    @pl.when(pl.program_id(1) == 0)
    def _():
        st_ref[...] = jnp.zeros_like(st_ref)
        ks_ref[...] = jnp.zeros_like(ks_ref)

    chains = [(hh, ci) for hh in range(HG_HEADS) for ci in range(HG_STEP_CHUNKS)]

    def blk(ref, ch):
        hh, ci = ch
        return ref[ci * c:(ci + 1) * c, hh * HG_D:(hh + 1) * HG_D]

    row = lax.broadcasted_iota(jnp.int32, (c, c), 0)
    col = lax.broadcasted_iota(jnp.int32, (c, c), 1)
    tri = (col <= row).astype(BF16)
    lane = lax.broadcasted_iota(jnp.int32, (sub, HG_D), 1)
    trow = lax.broadcasted_iota(jnp.int32, (sub, HG_D), 0)

    q, k, b, bk = {}, {}, {}, {}
    for ch in chains:
        f = blk(f_ref, ch)
        k[ch] = 1.0 - f
        lf = jnp.log(f) * LOG2E
        l1 = lf.astype(BF16)
        r1 = lf - l1.astype(F32)
        l2 = r1.astype(BF16)
        l3 = (r1 - l2.astype(F32)).astype(BF16)
        b[ch] = _dot(tri, l1) + _dot(tri, l2) + _dot(tri, l3)
    for ch in chains:
        q[ch] = blk(q_ref, ch).astype(F32)
        bk[ch] = b[ch] - jnp.log(k[ch]) * LOG2E

    qb, kdec, ebl, o_inter = {}, {}, {}, {}
    for ch in chains:
        bl = b[ch][c - 1:c, :]
        qb[ch] = (q[ch] * jnp.exp2(b[ch])).astype(BF16)
        kdec[ch] = (k[ch] * jnp.exp2(bl - b[ch])).astype(BF16)
        ebl[ch] = jnp.exp2(bl)
    for hh in range(HG_HEADS):
        st = st_ref[hh]
        for ci in range(HG_STEP_CHUNKS):
            ch = (hh, ci)
            o_inter[ch] = _dot_nt(qb[ch], st.astype(BF16))
            st = st * ebl[ch] + _dot_tn(blk(v_ref, ch), kdec[ch])
        st_ref[hh] = st

    a_rows = {ch: [] for ch in chains}
    for i in range(n_sub):
        rs = slice(i * sub, (i + 1) * sub)
        tiles = {ch: jnp.zeros((sub, HG_D), F32) for ch in chains}
        for ss in range(sub):
            s = i * sub + ss
            for ch in chains:
                kdec_s = jnp.exp2(jnp.minimum(b[ch][rs] - bk[ch][s:s + 1, :], 0.0))
                a_s = jnp.sum(q[ch][rs] * kdec_s, axis=-1, keepdims=True)
                tiles[ch] = jnp.where(lane == s, a_s, tiles[ch])
        for ch in chains:
            a_rows[ch].append(jnp.where(trow + i * sub >= lane, tiles[ch], 0.0)[:, :c])
    for i in range(1, n_sub):
        rs = slice(i * sub, (i + 1) * sub)
        for n, ch in enumerate(chains):
            r = b[ch][i * sub - 1:i * sub, :]
            ks_ref[n, i - 1, :i * sub, :] = k[ch][:i * sub] * jnp.exp2(r - b[ch][:i * sub])
            qs = (q[ch][rs] * jnp.exp2(b[ch][rs] - r)).astype(BF16)
            a_rows[ch][i] = a_rows[ch][i] + _dot_nt(qs, ks_ref[n, i - 1].astype(BF16))
    o = {}
    for ch in chains:
        a = jnp.concatenate(a_rows[ch], axis=0).astype(BF16)
        o[ch] = o_inter[ch] + _dot(a, blk(v_ref, ch))

    gn = gn_ref[...]
    for ch in chains:
        hh, ci = ch
        on = o[ch] * lax.rsqrt(jnp.mean(o[ch] * o[ch], axis=-1, keepdims=True) + RMS_EPS) * gn
        o_ref[ci * c:(ci + 1) * c, hh * HG_D:(hh + 1) * HG_D] = (on * blk(g_ref, ch).astype(F32)).astype(BF16)


def _hgrn(hq, hf, hi, hg, gn, l, bsz, seq):
    n_tok = hq.shape[0]
    c = HG_CHUNK
    rows = c * HG_STEP_CHUNKS
    nc = seq // rows
    spec = pl.BlockSpec((rows, HG_WIDTH), lambda b, i: (b * nc + i, 0))
    return pl.pallas_call(
        _hgrn_kernel,
        grid=(bsz, nc),
        in_specs=[spec, spec, spec, spec, _layer_spec(gn.shape, l)],
        out_specs=spec,
        out_shape=jax.ShapeDtypeStruct((n_tok, HG_WIDTH), BF16),
        scratch_shapes=[pltpu.VMEM((HG_HEADS, HG_D, HG_D), F32),
                        pltpu.VMEM((HG_HEADS * HG_STEP_CHUNKS, c // HG_SUB - 1, c, HG_D), F32)],
        compiler_params=_params(("parallel", "arbitrary")),
        name="hgrn",
    )(hq, hf, hi, hg, gn)


def _memkv_kernel(mem_ref, g_ref, w_ref, kv_ref):
    kv_ref[...] = _dot(_rms(mem_ref[...], g_ref[...]).astype(BF16), w_ref[...]).astype(BF16)


def _memkv(mem2, g, w, l, n_mem):
    n_rows, d_model = mem2.shape
    return pl.pallas_call(
        _memkv_kernel,
        grid=(n_rows // n_mem,),
        in_specs=[pl.BlockSpec((n_mem, d_model), lambda i: (i, 0)), _layer_spec(g.shape, l), _layer_spec(w.shape, l)],
        out_specs=pl.BlockSpec((n_mem, 2 * X_WIDTH), lambda i: (i, 0)),
        out_shape=jax.ShapeDtypeStruct((n_rows, 2 * X_WIDTH), BF16),
        compiler_params=_params(("parallel",)),
        name="memkv",
    )(mem2, g, w)


def _mid_kernel(x_ref, ys_ref, om_ref, oh_ref, gate_ref, kv_ref, wmo_ref, who_ref, wout_ref, gx_ref, wq_ref,
                wxo_ref, o_ref, *, x_scale):
    d_model = x_ref.shape[1]
    om_t = om_ref[0].reshape(MLA_HEADS * MLA_V, x_ref.shape[0])
    y_mla = _dot_tn(om_t, wmo_ref[...])
    y_hg = _dot(oh_ref[...], who_ref[...])
    gate = lambda br: gate_ref[:, br * d_model:(br + 1) * d_model].astype(F32)
    merged = gate(0) * ys_ref[...] + gate(1) * y_mla + gate(2) * y_hg
    x = x_ref[...] + _dot(merged.astype(BF16), wout_ref[...])

    q = (_dot(_rms(x, gx_ref[...]).astype(BF16), wq_ref[...]) * x_scale).astype(BF16)
    outs = []
    for hh in range(X_HEADS):
        sl = slice(hh * X_HEAD_DIM, (hh + 1) * X_HEAD_DIM)
        s = _dot_nt(q[:, sl], kv_ref[:, sl])
        pr = jnp.exp2(s - jnp.max(s, axis=-1, keepdims=True))
        den = jnp.sum(pr, axis=-1, keepdims=True)
        vh = kv_ref[:, X_WIDTH + hh * X_HEAD_DIM:X_WIDTH + (hh + 1) * X_HEAD_DIM]
        outs.append((_dot(pr.astype(BF16), vh) / den).astype(BF16))
    o_ref[...] = x + _dot(jnp.concatenate(outs, axis=1), wxo_ref[...])


def _mid(x2, ys, om, oh, gates, kv, wmo, who, wout, gx, wq, wxo, l, seq, n_mem):
    n_tok, d_model = x2.shape
    t = MID_TILE
    nt = seq // t
    tok = lambda w_: pl.BlockSpec((t, w_), lambda i: (i, 0))
    return pl.pallas_call(
        functools.partial(_mid_kernel, x_scale=LOG2E / math.sqrt(X_HEAD_DIM)),
        grid=(n_tok // t,),
        in_specs=[tok(d_model), tok(d_model),
                  pl.BlockSpec((1, MLA_HEADS, MLA_V, t), lambda i: (i // nt, 0, 0, i % nt)),
                  tok(HG_WIDTH), tok(N_BRANCH * d_model),
                  pl.BlockSpec((n_mem, 2 * X_WIDTH), lambda i: (i // nt, 0)),
                  ] + [_layer_spec(w_.shape, l) for w_ in (wmo, who, wout, gx, wq, wxo)],
        out_specs=tok(d_model),
        out_shape=jax.ShapeDtypeStruct((n_tok, d_model), F32),
        compiler_params=_params(("parallel",)),
        name="mid",
    )(x2, ys, om, oh, gates, kv, wmo, who, wout, gx, wq, wxo)


def _ffn_kernel(x_ref, g_ref, wgu_ref, wd_ref, gf_ref, o_ref, *, final_norm):
    d_ff = wd_ref.shape[0]
    x = x_ref[...]
    h = _rms(x, g_ref[...]).astype(BF16)
    y = x
    for c0 in range(0, d_ff, FFN_CHUNK):
        gt = _dot(h, wgu_ref[:, c0:c0 + FFN_CHUNK])
        up = _dot(h, wgu_ref[:, d_ff + c0:d_ff + c0 + FFN_CHUNK])
        act = (gt * _sigmoid(gt) * up).astype(BF16)
        y = y + _dot(act, wd_ref[c0:c0 + FFN_CHUNK, :])
    if final_norm:
        y = _rms(y, gf_ref[...])
    o_ref[...] = y


def _ffn(x2, g, wgu, wd, gf, l, final_norm):
    n_tok, d_model = x2.shape
    t = FFN_TILE
    tok = pl.BlockSpec((t, d_model), lambda i: (i, 0))
    return pl.pallas_call(
        functools.partial(_ffn_kernel, final_norm=final_norm),
        grid=(n_tok // t,),
        in_specs=[tok, _layer_spec(g.shape, l), _layer_spec(wgu.shape, l), _layer_spec(wd.shape, l),
                  _const_spec(gf.shape)],
        out_specs=tok,
        out_shape=jax.ShapeDtypeStruct((n_tok, d_model), F32),
        compiler_params=_params(("parallel",)),
        name="ffn",
    )(x2, g, wgu, wd, gf)


def _in_proj_weights(w_in):
    kpe = jnp.pad(w_in[:, :, C_KPE:C_KPE + MLA_ROPE], ((0, 0), (0, 0), (MLA_NOPE, HEAD_PAD - MLA_NOPE - MLA_ROPE)))
    cols = [w_in[:, :, :C_KPE], kpe, w_in[:, :, C_KPE + MLA_ROPE:]]
    return jnp.concatenate(cols, axis=2).astype(BF16)


def _mla_weights(w_uq, w_ukv):
    depth, rank, _ = w_uq.shape
    dq = MLA_NOPE + MLA_ROPE
    wq = w_uq.reshape(depth, rank, MLA_HEADS, dq)
    wuq = jnp.pad(wq, ((0, 0), (0, 0), (0, 0), (0, HEAD_PAD - dq))).reshape(depth, rank, MLA_HEADS * HEAD_PAD)
    kvr = w_ukv.shape[1]
    wkv = w_ukv.reshape(depth, kvr, MLA_HEADS, MLA_NOPE + MLA_V)
    wuk = wkv[..., :MLA_NOPE].reshape(depth, kvr, -1)
    wuvt = wkv[..., MLA_NOPE:].reshape(depth, kvr, -1).transpose(0, 2, 1)
    return wuq.astype(BF16), wuk.astype(BF16), wuvt.astype(BF16)


def _s5_weights(lam_re, lam_im, b_re, b_im, c_re, c_im, log_step):
    depth = lam_re.shape[0]
    step = jnp.exp(log_step)[..., None]
    mag = jnp.exp(lam_re * step)
    lbr = mag * jnp.cos(lam_im * step)
    lbi = mag * jnp.sin(lam_im * step)
    den = lam_re * lam_re + lam_im * lam_im
    cr = ((lbr - 1.0) * lam_re + lbi * lam_im) / den
    ci = (lbi * lam_re - (lbr - 1.0) * lam_im) / den
    bbr = cr[..., None] * b_re - ci[..., None] * b_im
    bbi = cr[..., None] * b_im + ci[..., None] * b_re
    gq = SSM_GROUPS // SSM_QUARTERS
    eye = jnp.eye(gq, dtype=F32)

    def blockdiag(m):
        m = m.reshape(depth, SSM_QUARTERS, gq, m.shape[2], m.shape[3]).transpose(0, 1, 2, 4, 3)
        out = m[:, :, :, :, None, :] * eye[None, None, :, None, :, None]
        return out.reshape(depth, SSM_QUARTERS, gq * m.shape[3], gq * m.shape[4])

    wb = jnp.concatenate([blockdiag(bbr), blockdiag(bbi)], axis=3).astype(BF16)
    wc = jnp.concatenate([blockdiag(c_re), -blockdiag(c_im)], axis=3).astype(BF16)
    lr = lbr.reshape(depth, 1, SSM_CH)
    li = lbi.reshape(depth, 1, SSM_CH)
    half = SUBLANES // 2
    a = jnp.broadcast_to(lr, (depth, SUBLANES, SSM_CH))
    bsw = jnp.concatenate([jnp.broadcast_to(-li, (depth, half, SSM_CH)), jnp.broadcast_to(li, (depth, half, SSM_CH))],
                          axis=1)
    return wb, a, bsw, wc


def kernel(x, mem, positions, norm_mix, w_in, ssm_lam_re, ssm_lam_im, ssm_b_re, ssm_b_im, ssm_c_re, ssm_c_im, ssm_d, ssm_log_step, ssm_w_glu, mla_q_norm, mla_kv_norm, mla_w_uq, mla_w_ukv, mla_w_o, hg_lb, hg_g_norm, hg_w_o, w_out, norm_cross, norm_mem, x_w_q, x_w_kv, x_w_o, norm_ffn, ffn_w_gate_up, ffn_w_down, norm_final):
    bsz, seq, d_model = x.shape
    depth = w_in.shape[0]
    n_mem = mem.shape[1]
    assert 2 * bsz == SUBLANES, "the s5 scan packs (re, im) x batch on the sublanes of one tile"
    n_tok = bsz * seq

    half = MLA_ROPE // 2
    invf = (ROPE_THETA ** (-jnp.arange(half, dtype=F32) / half)).reshape(half, 1)
    lb_p = jax.nn.softmax(hg_lb.astype(F32), axis=0)
    lower_bounds = jnp.cumsum(lb_p, axis=0) - lb_p[0:1]

    rows = lambda a: a.reshape(depth, 1, -1)
    bf = lambda a: a.astype(BF16)
    w_inp = _in_proj_weights(w_in)
    wuq, wuk, wuvt = _mla_weights(mla_w_uq, mla_w_ukv)
    wb, a, bsw, wc = _s5_weights(ssm_lam_re, ssm_lam_im, ssm_b_re, ssm_b_im, ssm_c_re, ssm_c_im, ssm_log_step)
    wglu, wmo, who, wout = bf(ssm_w_glu), bf(mla_w_o), bf(hg_w_o), bf(w_out)
    wxq, wxkv, wxo, wgu, wdn = bf(x_w_q), bf(x_w_kv), bf(x_w_o), bf(ffn_w_gate_up), bf(ffn_w_down)
    g_mix, g_q, g_kv, g_hg = rows(norm_mix), rows(mla_q_norm), rows(mla_kv_norm), rows(hg_g_norm)
    g_cross, g_mem, g_ffn = rows(norm_cross), rows(norm_mem), rows(norm_ffn)
    lbs, ssm_dr = rows(lower_bounds), rows(ssm_d)

    x2 = x.reshape(n_tok, d_model)
    mem2 = mem.reshape(bsz * n_mem, d_model)
    pos2 = positions.reshape(n_tok // IN_TILE, 1, IN_TILE)

    for l in range(depth):
        u, q, k, v, hq, hf, hi, hg, gates = _in_proj(x2, pos2, g_mix, w_inp, invf, g_q, wuq, g_kv, wuk, wuvt,
                                                     lbs, l, bsz, seq)
        ys = _s5(u.reshape(bsz, seq, SSM_WIDTH), wb, a, bsw, wc, ssm_dr, wglu, l, d_model).reshape(n_tok, d_model)
        om = _mla(q, k, v)
        oh = _hgrn(hq, hf, hi, hg, g_hg, l, bsz, seq)
        kv = _memkv(mem2, g_mem, wxkv, l, n_mem)
        x2 = _mid(x2, ys, om, oh, gates, kv, wmo, who, wout, g_cross, wxq, wxo, l, seq, n_mem)
        x2 = _ffn(x2, g_ffn, wgu, wdn, norm_final.reshape(1, -1), l, final_norm=(l == depth - 1))
    return x2.reshape(bsz, seq, d_model)
```

```python
import functools
import math

import jax
import jax.numpy as jnp
from jax import lax
from jax.experimental import pallas as pl
from jax.experimental.pallas import tpu as pltpu

F32 = jnp.float32
BF16 = jnp.bfloat16

SUBLANES = 8
LANES = 128

RMS_EPS = 1e-6
LOG2E = 1.4426950408889634

SSM_GROUPS = 32
SSM_GROUP_CH = 16
SSM_STATE = 64
SSM_WIDTH = SSM_GROUPS * SSM_GROUP_CH
SSM_CH = SSM_GROUPS * SSM_STATE
SSM_QUARTERS = 4

MLA_HEADS = 8
MLA_Q_RANK = 512
MLA_KV_RANK = 256
MLA_NOPE = 64
MLA_ROPE = 32
MLA_V = 64
ROPE_THETA = 10000.0
HEAD_PAD = LANES
MLA_VT_ROWS = 80

HG_HEADS = 4
HG_D = 128
HG_WIDTH = HG_HEADS * HG_D
HG_CHUNK = 64
HG_SUB = SUBLANES
HG_STEP_CHUNKS = 8

X_HEADS = 4
X_HEAD_DIM = 128
X_WIDTH = X_HEADS * X_HEAD_DIM

N_BRANCH = 3

VMEM_LIMIT = 56 * 1024 * 1024

IN_TILE = 512
S5_TIME = 128
MLA_BLOCK = 512
MLA_KEY_BLOCKS = 2
MID_TILE = 512
FFN_TILE = 512
FFN_CHUNK = 2816


def _dot(a, b):
    return jnp.dot(a, b, preferred_element_type=F32)


def _dot_nt(a, b):
    return lax.dot_general(a, b, (((1,), (1,)), ((), ())), preferred_element_type=F32)


def _dot_tn(a, b):
    return lax.dot_general(a, b, (((0,), (0,)), ((), ())), preferred_element_type=F32)


def _rms(x, g):
    return x * lax.rsqrt(jnp.mean(x * x, axis=-1, keepdims=True) + RMS_EPS) * g


def _sigmoid(x):
    return 1.0 / (1.0 + jnp.exp(-x))


def _const_spec(shape):
    nd = len(shape)
    return pl.BlockSpec(shape, lambda *_: (0,) * nd, pipeline_mode=pl.Buffered(1))


def _layer_spec(shape, l):
    nd = len(shape)
    return pl.BlockSpec((None,) + tuple(shape[1:]), lambda *_: (l,) + (0,) * (nd - 1),
                        pipeline_mode=pl.Buffered(1))


def _params(sem):
    return pltpu.CompilerParams(dimension_semantics=sem, vmem_limit_bytes=VMEM_LIMIT)


C_SSM = 0
C_QLAT = C_SSM + SSM_WIDTH
C_KVLAT = C_QLAT + MLA_Q_RANK
C_KPE = C_KVLAT + MLA_KV_RANK
C_HQ = C_KPE + HEAD_PAD
C_HF = C_HQ + HG_WIDTH
C_HI = C_HF + HG_WIDTH
C_HG = C_HI + HG_WIDTH
C_GATE = C_HG + HG_WIDTH


def _in_proj_kernel(x_ref, pos_ref, g_ref, w_ref, invf_ref, qn_ref, wuq_ref, kvn_ref, wuk_ref, wuvt_ref, lb_ref,
                    u_ref, q_ref, k_ref, v_ref, hq_ref, hf_ref, hi_ref, hg_ref, gate_ref, *, q_scale):
    d_model = x_ref.shape[1]
    h = _rms(x_ref[...], g_ref[...]).astype(BF16)

    def seg(a, b):
        return _dot(h, w_ref[:, a:b])

    qlat = seg(C_QLAT, C_KVLAT)
    kvlat = seg(C_KVLAT, C_KPE)
    kp = seg(C_KPE, C_HQ)
    qn = _rms(qlat, qn_ref[...]).astype(BF16)
    kvn = _rms(kvlat, kvn_ref[...]).astype(BF16)

    t = x_ref.shape[0]
    ang = invf_ref[...] * pos_ref[0].astype(F32)
    cos_c = jnp.cos(ang)
    sin_c = jnp.sin(ang)
    fill = lambda v, n: jnp.full((n, t), v, F32)
    tail = HEAD_PAD - MLA_NOPE - MLA_ROPE
    cos = jnp.concatenate([fill(1.0, MLA_NOPE), cos_c, cos_c, fill(1.0, tail)], axis=0).T
    sin = jnp.concatenate([fill(0.0, MLA_NOPE), -sin_c, sin_c, fill(0.0, tail)], axis=0).T
    lane = lax.broadcasted_iota(jnp.int32, (1, HEAD_PAD), 1)
    first_half = lane < MLA_NOPE + MLA_ROPE // 2

    def rope(v):
        swapped = jnp.where(first_half, pltpu.roll(v, HEAD_PAD - MLA_ROPE // 2, 1),
                            pltpu.roll(v, MLA_ROPE // 2, 1))
        return v * cos + swapped * sin

    u_ref[...] = seg(C_SSM, C_QLAT)
    hq = seg(C_HQ, C_HF)
    hq_ref[...] = (hq * _sigmoid(hq)).astype(BF16)
    lb = lb_ref[...]
    hf_ref[...] = lb + (1.0 - lb) * _sigmoid(seg(C_HF, C_HI))
    hi_ref[...] = seg(C_HI, C_HG).astype(BF16)
    hg = seg(C_HG, C_GATE)
    hg_ref[...] = (hg * _sigmoid(hg)).astype(BF16)

    qa = _dot(qn, wuq_ref[...])
    for hh in range(MLA_HEADS):
        q_ref[0, hh] = (rope(qa[:, hh * HEAD_PAD:(hh + 1) * HEAD_PAD]) * q_scale).astype(BF16)

    kn = _dot(kvn, wuk_ref[...])
    vt = _dot_nt(wuvt_ref[...], kvn)
    kpe = rope(kp)
    nope_lanes = lane < MLA_NOPE
    ones_rows = (lax.broadcasted_iota(jnp.int32, (MLA_VT_ROWS - MLA_V, t), 0) == 0).astype(BF16)
    for hh in range(MLA_HEADS):
        piece = kn[:, (hh // 2) * HEAD_PAD:(hh // 2 + 1) * HEAD_PAD]
        if hh % 2:
            piece = pltpu.roll(piece, MLA_NOPE, 1)
        k_ref[0, hh] = jnp.where(nope_lanes, piece, kpe).astype(BF16)
        v_ref[0, hh, :MLA_V, :] = vt[hh * MLA_V:(hh + 1) * MLA_V].astype(BF16)
        v_ref[0, hh, MLA_V:, :] = ones_rows

    for br in range(N_BRANCH):
        gate_ref[:, br * d_model:(br + 1) * d_model] = _sigmoid(
            seg(C_GATE + br * d_model, C_GATE + (br + 1) * d_model)).astype(BF16)


def _in_proj(x2, pos2, g, w, invf, qn, wuq, kvn, wuk, wuvt, lb, l, bsz, seq):
    n_tok, d_model = x2.shape
    t = IN_TILE
    nt = seq // t
    tok = lambda w_: pl.BlockSpec((t, w_), lambda i: (i, 0))
    lay = lambda a: _layer_spec(a.shape, l)
    head = pl.BlockSpec((1, MLA_HEADS, t, HEAD_PAD), lambda i: (i // nt, 0, i % nt, 0))
    head_shape = jax.ShapeDtypeStruct((bsz, MLA_HEADS, seq, HEAD_PAD), BF16)
    head_t = pl.BlockSpec((1, MLA_HEADS, MLA_VT_ROWS, t), lambda i: (i // nt, 0, 0, i % nt))
    head_t_shape = jax.ShapeDtypeStruct((bsz, MLA_HEADS, MLA_VT_ROWS, seq), BF16)
    q_scale = LOG2E / math.sqrt(MLA_NOPE + MLA_ROPE)
    return pl.pallas_call(
        functools.partial(_in_proj_kernel, q_scale=q_scale),
        grid=(n_tok // t,),
        in_specs=[tok(d_model), pl.BlockSpec((1, 1, t), lambda i: (i, 0, 0)), lay(g), lay(w),
                  _const_spec(invf.shape), lay(qn), lay(wuq), lay(kvn), lay(wuk), lay(wuvt), lay(lb)],
        out_specs=[tok(SSM_WIDTH), head, head, head_t, tok(HG_WIDTH), tok(HG_WIDTH), tok(HG_WIDTH), tok(HG_WIDTH),
                   tok(N_BRANCH * d_model)],
        out_shape=[jax.ShapeDtypeStruct((n_tok, SSM_WIDTH), F32), head_shape, head_shape, head_t_shape,
                   jax.ShapeDtypeStruct((n_tok, HG_WIDTH), BF16), jax.ShapeDtypeStruct((n_tok, HG_WIDTH), F32),
                   jax.ShapeDtypeStruct((n_tok, HG_WIDTH), BF16), jax.ShapeDtypeStruct((n_tok, HG_WIDTH), BF16),
                   jax.ShapeDtypeStruct((n_tok, N_BRANCH * d_model), BF16)],
        compiler_params=_params(("parallel",)),
        name="in_proj",
    )(x2, pos2, g, w, invf, qn, wuq, kvn, wuk, wuvt, lb)


def _s5_kernel(u_ref, wb_ref, a_ref, bsw_ref, wc_ref, d_ref, wglu_ref, y_ref, se_ref, so_ref, st_ref, utb_ref,
               ytb_ref):
    bsz, tc, _ = u_ref.shape
    rows = bsz * tc
    d_model = y_ref.shape[2]
    qw = SSM_CH // SSM_QUARTERS
    uw = SSM_WIDTH // SSM_QUARTERS
    half = SUBLANES // 2

    @pl.when(pl.program_id(0) == 0)
    def _():
        st_ref[...] = jnp.zeros_like(st_ref)

    for q in range(SSM_QUARTERS):
        for b in range(bsz):
            utb_ref[q, pl.ds(b, tc, stride=bsz), :] = u_ref[b, :, q * uw:(q + 1) * uw]

    low = (lax.broadcasted_iota(jnp.int32, (rows, qw), 0) % SUBLANES) < half
    for q in range(SSM_QUARTERS):
        o = _dot(utb_ref[q].astype(BF16), wb_ref[q])
        re = o[:, :qw]
        im = o[:, qw:]
        se_ref[:, q * qw:(q + 1) * qw] = jnp.where(low, re, pltpu.roll(im, half, 0))
        so_ref[:, q * qw:(q + 1) * qw] = jnp.where(low, pltpu.roll(re, rows - half, 0), im)

    n_blk = 2
    cw = SSM_CH // n_blk
    for cb in range(n_blk):
        cs = slice(cb * cw, (cb + 1) * cw)
        a = a_ref[:, cs]
        bs = bsw_ref[:, cs]

        def body(v, s, cs=cs, a=a, bs=bs):
            r = pl.multiple_of(v * SUBLANES, SUBLANES)
            s = a * s + bs * pltpu.roll(s, half, 0) + se_ref[pl.ds(r, SUBLANES), cs]
            se_ref[pl.ds(r, SUBLANES), cs] = s
            s = a * s + bs * pltpu.roll(s, half, 0) + so_ref[pl.ds(r, SUBLANES), cs]
            so_ref[pl.ds(r, SUBLANES), cs] = s
            return s

        st_ref[:, cs] = lax.fori_loop(0, rows // SUBLANES, body, st_ref[:, cs], unroll=4)

    yw = uw
    low_y = low[:, :yw]
    pieces = []
    for q in range(SSM_QUARTERS):
        oe = _dot(se_ref[:, q * qw:(q + 1) * qw].astype(BF16), wc_ref[q])
        oo = _dot(so_ref[:, q * qw:(q + 1) * qw].astype(BF16), wc_ref[q])
        ye = oe[:, :yw] + pltpu.roll(oe[:, yw:], rows - half, 0)
        yo = pltpu.roll(oo[:, :yw], half, 0) + oo[:, yw:]
        pieces.append(jnp.where(low_y, ye, yo) + d_ref[:, q * uw:(q + 1) * uw] * utb_ref[q])
    y = jax.nn.gelu(jnp.concatenate(pieces, axis=1)).astype(BF16)
    z = _dot(y, wglu_ref[...])
    res = z[:, :d_model] * _sigmoid(z[:, d_model:])
    n_slab = d_model // uw
    for j in range(n_slab):
        ytb_ref[j] = res[:, j * uw:(j + 1) * uw]
    for j in range(n_slab):
        for b in range(bsz):
            y_ref[b, :, j * uw:(j + 1) * uw] = ytb_ref[j, pl.ds(b, tc, stride=bsz), :]


def _s5(u3, wb, a, bsw, wc, d, wglu, l, d_model):
    bsz, seq, _ = u3.shape
    tc = S5_TIME
    rows = tc * bsz
    lanes = SSM_WIDTH // SSM_QUARTERS
    return pl.pallas_call(
        _s5_kernel,
        grid=(seq // tc,),
        in_specs=[pl.BlockSpec((bsz, tc, SSM_WIDTH), lambda i: (0, i, 0))]
        + [_layer_spec(w_.shape, l) for w_ in (wb, a, bsw, wc, d, wglu)],
        out_specs=pl.BlockSpec((bsz, tc, d_model), lambda i: (0, i, 0)),
        out_shape=jax.ShapeDtypeStruct((bsz, seq, d_model), F32),
        scratch_shapes=[pltpu.VMEM((rows, SSM_CH), F32), pltpu.VMEM((rows, SSM_CH), F32),
                        pltpu.VMEM((SUBLANES, SSM_CH), F32), pltpu.VMEM((SSM_QUARTERS, rows, lanes), F32),
                        pltpu.VMEM((d_model // lanes, rows, lanes), F32)],
        compiler_params=_params(("arbitrary",)),
        name="s5",
    )(u3, wb, a, bsw, wc, d, wglu)


def _mla_kernel(qi_ref, kj_ref, q_ref, k_ref, vt_ref, o_ref, m_ref, acc_ref):
    p = pl.program_id(1)
    qi = qi_ref[p]
    kj = kj_ref[p]
    blk = q_ref.shape[2]

    @pl.when(kj == 0)
    def _():
        m_ref[...] = jnp.full_like(m_ref, -1e30)
        acc_ref[...] = jnp.zeros_like(acc_ref)

    def step(plan):
        if any(plan):
            key = lax.broadcasted_iota(jnp.int32, (blk, blk), 0)
            qry = lax.broadcasted_iota(jnp.int32, (blk, blk), 1)
            causal = key <= qry
        items = [(sub, hh, masked) for sub, masked in enumerate(plan) for hh in range(MLA_HEADS)]
        sc, mx = {}, {}
        for t in range(len(items) + 2):
            if t < len(items):
                sub, hh, masked = items[t]
                s = _dot_nt(k_ref[0, hh, sub * blk:(sub + 1) * blk, :], q_ref[0, hh])
                sc[t] = jnp.where(causal, s, -1e30) if masked else s
            if 1 <= t <= len(items):
                hh = items[t - 1][1]
                m_prev = m_ref[hh]
                m_new = jnp.maximum(m_prev, jnp.max(sc[t - 1], axis=0, keepdims=True))
                mx[t - 1] = (m_new, jnp.exp2(m_prev - m_new))
                m_ref[hh] = m_new
            if t >= 2:
                sub, hh, _ = items[t - 2]
                m_new, alpha = mx.pop(t - 2)
                pr = jnp.exp2(sc.pop(t - 2) - m_new).astype(BF16)
                acc_ref[hh] = alpha * acc_ref[hh] + _dot(vt_ref[0, hh, :, sub * blk:(sub + 1) * blk], pr)

    def finalize():
        for hh in range(MLA_HEADS):
            acc = acc_ref[hh]
            o_ref[0, hh] = (acc[:MLA_V, :] / acc[MLA_V:MLA_V + 1, :]).astype(BF16)

    first = kj * MLA_KEY_BLOCKS

    @pl.when(first + 1 < qi)
    def _():
        step([False, False])

    @pl.when(first + 1 == qi)
    def _():
        step([False, True])
        finalize()

    @pl.when(first == qi)
    def _():
        step([True])
        finalize()


def _mla(q, k, vt):
    assert MLA_KEY_BLOCKS == 2, "the kernel enumerates the three window shapes of a two-block window"
    bsz, nh, seq, hd = q.shape
    blk = MLA_BLOCK
    nb = seq // blk
    pairs = [(i, j) for i in range(nb) for j in range(i // MLA_KEY_BLOCKS + 1)]
    qi = jnp.asarray([p[0] for p in pairs], jnp.int32)
    kj = jnp.asarray([p[1] for p in pairs], jnp.int32)
    qspec = pl.BlockSpec((1, nh, blk, hd), lambda b, p, qi, kj: (b, 0, qi[p], 0))
    kspec = pl.BlockSpec((1, nh, blk * MLA_KEY_BLOCKS, hd), lambda b, p, qi, kj: (b, 0, kj[p], 0))
    vspec = pl.BlockSpec((1, nh, MLA_VT_ROWS, blk * MLA_KEY_BLOCKS), lambda b, p, qi, kj: (b, 0, 0, kj[p]))
    ospec = pl.BlockSpec((1, nh, MLA_V, blk), lambda b, p, qi, kj: (b, 0, 0, qi[p]))
    return pl.pallas_call(
        _mla_kernel,
        grid_spec=pltpu.PrefetchScalarGridSpec(
            num_scalar_prefetch=2,
            grid=(bsz, len(pairs)),
            in_specs=[qspec, kspec, vspec],
            out_specs=ospec,
            scratch_shapes=[pltpu.VMEM((nh, 1, blk), F32), pltpu.VMEM((nh, MLA_VT_ROWS, blk), F32)],
        ),
        out_shape=jax.ShapeDtypeStruct((bsz, nh, MLA_V, seq), BF16),
        compiler_params=_params(("parallel", "arbitrary")),
        name="mla",
    )(qi, kj, q, k, vt)


def _hgrn_kernel(q_ref, f_ref, v_ref, g_ref, gn_ref, o_ref, st_ref, ks_ref):
    c = HG_CHUNK
    sub = HG_SUB
    n_sub = c // sub

    @pl.when(pl.program_id(1) == 0)
    def _():
        st_ref[...] = jnp.zeros_like(st_ref)
        ks_ref[...] = jnp.zeros_like(ks_ref)

    chains = [(hh, ci) for hh in range(HG_HEADS) for ci in range(HG_STEP_CHUNKS)]

    def blk(ref, ch):
        hh, ci = ch
        return ref[ci * c:(ci + 1) * c, hh * HG_D:(hh + 1) * HG_D]

    row = lax.broadcasted_iota(jnp.int32, (c, c), 0)
    col = lax.broadcasted_iota(jnp.int32, (c, c), 1)
    tri = (col <= row).astype(BF16)
    lane = lax.broadcasted_iota(jnp.int32, (sub, HG_D), 1)
    trow = lax.broadcasted_iota(jnp.int32, (sub, HG_D), 0)

    q, k, b, bk = {}, {}, {}, {}
    for ch in chains:
        f = blk(f_ref, ch)
        k[ch] = 1.0 - f
        lf = jnp.log(f) * LOG2E
        l1 = lf.astype(BF16)
        r1 = lf - l1.astype(F32)
        l2 = r1.astype(BF16)
        l3 = (r1 - l2.astype(F32)).astype(BF16)
        b[ch] = _dot(tri, l1) + _dot(tri, l2) + _dot(tri, l3)
    for ch in chains:
        q[ch] = blk(q_ref, ch).astype(F32)
        bk[ch] = b[ch] - jnp.log(k[ch]) * LOG2E

    qb, kdec, ebl, o_inter = {}, {}, {}, {}
    for ch in chains:
        bl = b[ch][c - 1:c, :]
        qb[ch] = (q[ch] * jnp.exp2(b[ch])).astype(BF16)
        kdec[ch] = (k[ch] * jnp.exp2(bl - b[ch])).astype(BF16)
        ebl[ch] = jnp.exp2(bl)
    for hh in range(HG_HEADS):
        st = st_ref[hh]
        for ci in range(HG_STEP_CHUNKS):
            ch = (hh, ci)
            o_inter[ch] = _dot_nt(qb[ch], st.astype(BF16))
            st = st * ebl[ch] + _dot_tn(blk(v_ref, ch), kdec[ch])
        st_ref[hh] = st

    a_rows = {ch: [] for ch in chains}
    for i in range(n_sub):
        rs = slice(i * sub, (i + 1) * sub)
        tiles = {ch: jnp.zeros((sub, HG_D), F32) for ch in chains}
        for ss in range(sub):
            s = i * sub + ss
            for ch in chains:
                kdec_s = jnp.exp2(jnp.minimum(b[ch][rs] - bk[ch][s:s + 1, :], 0.0))
                a_s = jnp.sum(q[ch][rs] * kdec_s, axis=-1, keepdims=True)
                tiles[ch] = jnp.where(lane == s, a_s, tiles[ch])
        for ch in chains:
            a_rows[ch].append(jnp.where(trow + i * sub >= lane, tiles[ch], 0.0)[:, :c])
    for i in range(1, n_sub):
        rs = slice(i * sub, (i + 1) * sub)
        for n, ch in enumerate(chains):
            r = b[ch][i * sub - 1:i * sub, :]
            ks_ref[n, i - 1, :i * sub, :] = k[ch][:i * sub] * jnp.exp2(r - b[ch][:i * sub])
            qs = (q[ch][rs] * jnp.exp2(b[ch][rs] - r)).astype(BF16)
            a_rows[ch][i] = a_rows[ch][i] + _dot_nt(qs, ks_ref[n, i - 1].astype(BF16))
    o = {}
    for ch in chains:
        a = jnp.concatenate(a_rows[ch], axis=0).astype(BF16)
        o[ch] = o_inter[ch] + _dot(a, blk(v_ref, ch))

    gn = gn_ref[...]
    for ch in chains:
        hh, ci = ch
        on = o[ch] * lax.rsqrt(jnp.mean(o[ch] * o[ch], axis=-1, keepdims=True) + RMS_EPS) * gn
        o_ref[ci * c:(ci + 1) * c, hh * HG_D:(hh + 1) * HG_D] = (on * blk(g_ref, ch).astype(F32)).astype(BF16)


def _hgrn(hq, hf, hi, hg, gn, l, bsz, seq):
    n_tok = hq.shape[0]
    c = HG_CHUNK
    rows = c * HG_STEP_CHUNKS
    nc = seq // rows
    spec = pl.BlockSpec((rows, HG_WIDTH), lambda b, i: (b * nc + i, 0))
    return pl.pallas_call(
        _hgrn_kernel,
        grid=(bsz, nc),
        in_specs=[spec, spec, spec, spec, _layer_spec(gn.shape, l)],
        out_specs=spec,
        out_shape=jax.ShapeDtypeStruct((n_tok, HG_WIDTH), BF16),
        scratch_shapes=[pltpu.VMEM((HG_HEADS, HG_D, HG_D), F32),
                        pltpu.VMEM((HG_HEADS * HG_STEP_CHUNKS, c // HG_SUB - 1, c, HG_D), F32)],
        compiler_params=_params(("parallel", "arbitrary")),
        name="hgrn",
    )(hq, hf, hi, hg, gn)


def _memkv_kernel(mem_ref, g_ref, w_ref, kv_ref):
    kv_ref[...] = _dot(_rms(mem_ref[...], g_ref[...]).astype(BF16), w_ref[...]).astype(BF16)


def _memkv(mem2, g, w, l, n_mem):
    n_rows, d_model = mem2.shape
    return pl.pallas_call(
        _memkv_kernel,
        grid=(n_rows // n_mem,),
        in_specs=[pl.BlockSpec((n_mem, d_model), lambda i: (i, 0)), _layer_spec(g.shape, l), _layer_spec(w.shape, l)],
        out_specs=pl.BlockSpec((n_mem, 2 * X_WIDTH), lambda i: (i, 0)),
        out_shape=jax.ShapeDtypeStruct((n_rows, 2 * X_WIDTH), BF16),
        compiler_params=_params(("parallel",)),
        name="memkv",
    )(mem2, g, w)


def _mid_kernel(x_ref, ys_ref, om_ref, oh_ref, gate_ref, kv_ref, wmo_ref, who_ref, wout_ref, gx_ref, wq_ref,
                wxo_ref, o_ref, *, x_scale):
    d_model = x_ref.shape[1]
    om_t = om_ref[0].reshape(MLA_HEADS * MLA_V, x_ref.shape[0])
    y_mla = _dot_tn(om_t, wmo_ref[...])
    y_hg = _dot(oh_ref[...], who_ref[...])
    gate = lambda br: gate_ref[:, br * d_model:(br + 1) * d_model].astype(F32)
    merged = gate(0) * ys_ref[...] + gate(1) * y_mla + gate(2) * y_hg
    x = x_ref[...] + _dot(merged.astype(BF16), wout_ref[...])

    q = (_dot(_rms(x, gx_ref[...]).astype(BF16), wq_ref[...]) * x_scale).astype(BF16)
    outs = []
    for hh in range(X_HEADS):
        sl = slice(hh * X_HEAD_DIM, (hh + 1) * X_HEAD_DIM)
        s = _dot_nt(q[:, sl], kv_ref[:, sl])
        pr = jnp.exp2(s - jnp.max(s, axis=-1, keepdims=True))
        den = jnp.sum(pr, axis=-1, keepdims=True)
        vh = kv_ref[:, X_WIDTH + hh * X_HEAD_DIM:X_WIDTH + (hh + 1) * X_HEAD_DIM]
        outs.append((_dot(pr.astype(BF16), vh) / den).astype(BF16))
    o_ref[...] = x + _dot(jnp.concatenate(outs, axis=1), wxo_ref[...])


def _mid(x2, ys, om, oh, gates, kv, wmo, who, wout, gx, wq, wxo, l, seq, n_mem):
    n_tok, d_model = x2.shape
    t = MID_TILE
    nt = seq // t
    tok = lambda w_: pl.BlockSpec((t, w_), lambda i: (i, 0))
    return pl.pallas_call(
        functools.partial(_mid_kernel, x_scale=LOG2E / math.sqrt(X_HEAD_DIM)),
        grid=(n_tok // t,),
        in_specs=[tok(d_model), tok(d_model),
                  pl.BlockSpec((1, MLA_HEADS, MLA_V, t), lambda i: (i // nt, 0, 0, i % nt)),
                  tok(HG_WIDTH), tok(N_BRANCH * d_model),
                  pl.BlockSpec((n_mem, 2 * X_WIDTH), lambda i: (i // nt, 0)),
                  ] + [_layer_spec(w_.shape, l) for w_ in (wmo, who, wout, gx, wq, wxo)],
        out_specs=tok(d_model),
        out_shape=jax.ShapeDtypeStruct((n_tok, d_model), F32),
        compiler_params=_params(("parallel",)),
        name="mid",
    )(x2, ys, om, oh, gates, kv, wmo, who, wout, gx, wq, wxo)


def _ffn_kernel(x_ref, g_ref, wgu_ref, wd_ref, gf_ref, o_ref, *, final_norm):
    d_ff = wd_ref.shape[0]
    x = x_ref[...]
    h = _rms(x, g_ref[...]).astype(BF16)
    y = x
    for c0 in range(0, d_ff, FFN_CHUNK):
        gt = _dot(h, wgu_ref[:, c0:c0 + FFN_CHUNK])
        up = _dot(h, wgu_ref[:, d_ff + c0:d_ff + c0 + FFN_CHUNK])
        act = (gt * _sigmoid(gt) * up).astype(BF16)
        y = y + _dot(act, wd_ref[c0:c0 + FFN_CHUNK, :])
    if final_norm:
        y = _rms(y, gf_ref[...])
    o_ref[...] = y


def _ffn(x2, g, wgu, wd, gf, l, final_norm):
    n_tok, d_model = x2.shape
    t = FFN_TILE
    tok = pl.BlockSpec((t, d_model), lambda i: (i, 0))
    return pl.pallas_call(
        functools.partial(_ffn_kernel, final_norm=final_norm),
        grid=(n_tok // t,),
        in_specs=[tok, _layer_spec(g.shape, l), _layer_spec(wgu.shape, l), _layer_spec(wd.shape, l),
                  _const_spec(gf.shape)],
        out_specs=tok,
        out_shape=jax.ShapeDtypeStruct((n_tok, d_model), F32),
        compiler_params=_params(("parallel",)),
        name="ffn",
    )(x2, g, wgu, wd, gf)


def _in_proj_weights(w_in):
    kpe = jnp.pad(w_in[:, :, C_KPE:C_KPE + MLA_ROPE], ((0, 0), (0, 0), (MLA_NOPE, HEAD_PAD - MLA_NOPE - MLA_ROPE)))
    cols = [w_in[:, :, :C_KPE], kpe, w_in[:, :, C_KPE + MLA_ROPE:]]
    return jnp.concatenate(cols, axis=2).astype(BF16)


def _mla_weights(w_uq, w_ukv):
    depth, rank, _ = w_uq.shape
    dq = MLA_NOPE + MLA_ROPE
    wq = w_uq.reshape(depth, rank, MLA_HEADS, dq)
    wuq = jnp.pad(wq, ((0, 0), (0, 0), (0, 0), (0, HEAD_PAD - dq))).reshape(depth, rank, MLA_HEADS * HEAD_PAD)
    kvr = w_ukv.shape[1]
    wkv = w_ukv.reshape(depth, kvr, MLA_HEADS, MLA_NOPE + MLA_V)
    wuk = wkv[..., :MLA_NOPE].reshape(depth, kvr, -1)
    wuvt = wkv[..., MLA_NOPE:].reshape(depth, kvr, -1).transpose(0, 2, 1)
    return wuq.astype(BF16), wuk.astype(BF16), wuvt.astype(BF16)


def _s5_weights(lam_re, lam_im, b_re, b_im, c_re, c_im, log_step):
    depth = lam_re.shape[0]
    step = jnp.exp(log_step)[..., None]
    mag = jnp.exp(lam_re * step)
    lbr = mag * jnp.cos(lam_im * step)
    lbi = mag * jnp.sin(lam_im * step)
    den = lam_re * lam_re + lam_im * lam_im
    cr = ((lbr - 1.0) * lam_re + lbi * lam_im) / den
    ci = (lbi * lam_re - (lbr - 1.0) * lam_im) / den
    bbr = cr[..., None] * b_re - ci[..., None] * b_im
    bbi = cr[..., None] * b_im + ci[..., None] * b_re
    gq = SSM_GROUPS // SSM_QUARTERS
    eye = jnp.eye(gq, dtype=F32)

    def blockdiag(m):
        m = m.reshape(depth, SSM_QUARTERS, gq, m.shape[2], m.shape[3]).transpose(0, 1, 2, 4, 3)
        out = m[:, :, :, :, None, :] * eye[None, None, :, None, :, None]
        return out.reshape(depth, SSM_QUARTERS, gq * m.shape[3], gq * m.shape[4])

    wb = jnp.concatenate([blockdiag(bbr), blockdiag(bbi)], axis=3).astype(BF16)
    wc = jnp.concatenate([blockdiag(c_re), -blockdiag(c_im)], axis=3).astype(BF16)
    lr = lbr.reshape(depth, 1, SSM_CH)
    li = lbi.reshape(depth, 1, SSM_CH)
    half = SUBLANES // 2
    a = jnp.broadcast_to(lr, (depth, SUBLANES, SSM_CH))
    bsw = jnp.concatenate([jnp.broadcast_to(-li, (depth, half, SSM_CH)), jnp.broadcast_to(li, (depth, half, SSM_CH))],
                          axis=1)
    return wb, a, bsw, wc


def kernel(x, mem, positions, norm_mix, w_in, ssm_lam_re, ssm_lam_im, ssm_b_re, ssm_b_im, ssm_c_re, ssm_c_im, ssm_d, ssm_log_step, ssm_w_glu, mla_q_norm, mla_kv_norm, mla_w_uq, mla_w_ukv, mla_w_o, hg_lb, hg_g_norm, hg_w_o, w_out, norm_cross, norm_mem, x_w_q, x_w_kv, x_w_o, norm_ffn, ffn_w_gate_up, ffn_w_down, norm_final):
    bsz, seq, d_model = x.shape
    depth = w_in.shape[0]
    n_mem = mem.shape[1]
    assert 2 * bsz == SUBLANES, "the s5 scan packs (re, im) x batch on the sublanes of one tile"
    n_tok = bsz * seq

    half = MLA_ROPE // 2
    invf = (ROPE_THETA ** (-jnp.arange(half, dtype=F32) / half)).reshape(half, 1)
    lb_p = jax.nn.softmax(hg_lb.astype(F32), axis=0)
    lower_bounds = jnp.cumsum(lb_p, axis=0) - lb_p[0:1]

    rows = lambda a: a.reshape(depth, 1, -1)
    bf = lambda a: a.astype(BF16)
    w_inp = _in_proj_weights(w_in)
    wuq, wuk, wuvt = _mla_weights(mla_w_uq, mla_w_ukv)
    wb, a, bsw, wc = _s5_weights(ssm_lam_re, ssm_lam_im, ssm_b_re, ssm_b_im, ssm_c_re, ssm_c_im, ssm_log_step)
    wglu, wmo, who, wout = bf(ssm_w_glu), bf(mla_w_o), bf(hg_w_o), bf(w_out)
    wxq, wxkv, wxo, wgu, wdn = bf(x_w_q), bf(x_w_kv), bf(x_w_o), bf(ffn_w_gate_up), bf(ffn_w_down)
    g_mix, g_q, g_kv, g_hg = rows(norm_mix), rows(mla_q_norm), rows(mla_kv_norm), rows(hg_g_norm)
    g_cross, g_mem, g_ffn = rows(norm_cross), rows(norm_mem), rows(norm_ffn)
    lbs, ssm_dr = rows(lower_bounds), rows(ssm_d)

    x2 = x.reshape(n_tok, d_model)
    mem2 = mem.reshape(bsz * n_mem, d_model)
    pos2 = positions.reshape(n_tok // IN_TILE, 1, IN_TILE)

    for l in range(depth):
        u, q, k, v, hq, hf, hi, hg, gates = _in_proj(x2, pos2, g_mix, w_inp, invf, g_q, wuq, g_kv, wuk, wuvt,
                                                     lbs, l, bsz, seq)
        ys = _s5(u.reshape(bsz, seq, SSM_WIDTH), wb, a, bsw, wc, ssm_dr, wglu, l, d_model).reshape(n_tok, d_model)
        om = _mla(q, k, v)
        oh = _hgrn(hq, hf, hi, hg, g_hg, l, bsz, seq)
        kv = _memkv(mem2, g_mem, wxkv, l, n_mem)
        x2 = _mid(x2, ys, om, oh, gates, kv, wmo, who, wout, g_cross, wxq, wxo, l, seq, n_mem)
        x2 = _ffn(x2, g_ffn, wgu, wdn, norm_final.reshape(1, -1), l, final_norm=(l == depth - 1))
    return x2.reshape(bsz, seq, d_model)
```

```python
import functools
import math

import jax
import jax.numpy as jnp
from jax import lax
from jax.experimental import pallas as pl
from jax.experimental.pallas import tpu as pltpu

F32 = jnp.float32
BF16 = jnp.bfloat16

SUBLANES = 8
LANES = 128

RMS_EPS = 1e-6
LOG2E = 1.4426950408889634

SSM_GROUPS = 32
SSM_GROUP_CH = 16
SSM_STATE = 64
SSM_WIDTH = SSM_GROUPS * SSM_GROUP_CH
SSM_CH = SSM_GROUPS * SSM_STATE
SSM_QUARTERS = 4

MLA_HEADS = 8
MLA_Q_RANK = 512
MLA_KV_RANK = 256
MLA_NOPE = 64
MLA_ROPE = 32
MLA_V = 64
ROPE_THETA = 10000.0
HEAD_PAD = LANES
MLA_VT_ROWS = 80

HG_HEADS = 4
HG_D = 128
HG_WIDTH = HG_HEADS * HG_D
HG_CHUNK = 64
HG_SUB = SUBLANES
HG_STEP_CHUNKS = 8

X_HEADS = 4
X_HEAD_DIM = 128
X_WIDTH = X_HEADS * X_HEAD_DIM

N_BRANCH = 3

VMEM_LIMIT = 56 * 1024 * 1024

IN_TILE = 512
S5_TIME = 128
MLA_BLOCK = 512
MLA_Q_SPLIT = 2
MID_TILE = 512
FFN_TILE = 512
FFN_CHUNK = 2816


def _dot(a, b):
    return jnp.dot(a, b, preferred_element_type=F32)


def _dot_nt(a, b):
    return lax.dot_general(a, b, (((1,), (1,)), ((), ())), preferred_element_type=F32)


def _dot_tn(a, b):
    return lax.dot_general(a, b, (((0,), (0,)), ((), ())), preferred_element_type=F32)


def _rms(x, g):
    return x * lax.rsqrt(jnp.mean(x * x, axis=-1, keepdims=True) + RMS_EPS) * g


def _sigmoid(x):
    return 1.0 / (1.0 + jnp.exp(-x))


def _const_spec(shape):
    nd = len(shape)
    return pl.BlockSpec(shape, lambda *_: (0,) * nd, pipeline_mode=pl.Buffered(1))


def _layer_spec(shape, l):
    nd = len(shape)
    return pl.BlockSpec((None,) + tuple(shape[1:]), lambda *_: (l,) + (0,) * (nd - 1),
                        pipeline_mode=pl.Buffered(1))


def _params(sem):
    return pltpu.CompilerParams(dimension_semantics=sem, vmem_limit_bytes=VMEM_LIMIT)


C_SSM = 0
C_QLAT = C_SSM + SSM_WIDTH
C_KVLAT = C_QLAT + MLA_Q_RANK
C_KPE = C_KVLAT + MLA_KV_RANK
C_HQ = C_KPE + HEAD_PAD
C_HF = C_HQ + HG_WIDTH
C_HI = C_HF + HG_WIDTH
C_HG = C_HI + HG_WIDTH
C_GATE = C_HG + HG_WIDTH


def _in_proj_kernel(x_ref, pos_ref, g_ref, w_ref, invf_ref, qn_ref, wuq_ref, kvn_ref, wuk_ref, wuvt_ref, lb_ref,
                    u_ref, q_ref, k_ref, v_ref, hq_ref, hf_ref, hi_ref, hg_ref, gate_ref, *, q_scale):
    d_model = x_ref.shape[1]
    h = _rms(x_ref[...], g_ref[...]).astype(BF16)

    def seg(a, b):
        return _dot(h, w_ref[:, a:b])

    qlat = seg(C_QLAT, C_KVLAT)
    kvlat = seg(C_KVLAT, C_KPE)
    kp = seg(C_KPE, C_HQ)
    qn = _rms(qlat, qn_ref[...]).astype(BF16)
    kvn = _rms(kvlat, kvn_ref[...]).astype(BF16)

    t = x_ref.shape[0]
    ang = invf_ref[...] * pos_ref[0].astype(F32)
    cos_c = jnp.cos(ang)
    sin_c = jnp.sin(ang)
    fill = lambda v, n: jnp.full((n, t), v, F32)
    tail = HEAD_PAD - MLA_NOPE - MLA_ROPE
    cos = jnp.concatenate([fill(1.0, MLA_NOPE), cos_c, cos_c, fill(1.0, tail)], axis=0).T
    sin = jnp.concatenate([fill(0.0, MLA_NOPE), -sin_c, sin_c, fill(0.0, tail)], axis=0).T
    lane = lax.broadcasted_iota(jnp.int32, (1, HEAD_PAD), 1)
    first_half = lane < MLA_NOPE + MLA_ROPE // 2

    def rope(v):
        swapped = jnp.where(first_half, pltpu.roll(v, HEAD_PAD - MLA_ROPE // 2, 1),
                            pltpu.roll(v, MLA_ROPE // 2, 1))
        return v * cos + swapped * sin

    u_ref[...] = seg(C_SSM, C_QLAT)
    hq = seg(C_HQ, C_HF)
    hq_ref[...] = (hq * _sigmoid(hq)).astype(BF16)
    lb = lb_ref[...]
    hf_ref[...] = lb + (1.0 - lb) * _sigmoid(seg(C_HF, C_HI))
    hi_ref[...] = seg(C_HI, C_HG).astype(BF16)
    hg = seg(C_HG, C_GATE)
    hg_ref[...] = (hg * _sigmoid(hg)).astype(BF16)

    qa = _dot(qn, wuq_ref[...])
    for hh in range(MLA_HEADS):
        q_ref[0, hh] = (rope(qa[:, hh * HEAD_PAD:(hh + 1) * HEAD_PAD]) * q_scale).astype(BF16)

    kn = _dot(kvn, wuk_ref[...])
    vt = _dot_nt(wuvt_ref[...], kvn)
    kpe = rope(kp)
    nope_lanes = lane < MLA_NOPE
    ones_rows = (lax.broadcasted_iota(jnp.int32, (MLA_VT_ROWS - MLA_V, t), 0) == 0).astype(BF16)
    for hh in range(MLA_HEADS):
        piece = kn[:, (hh // 2) * HEAD_PAD:(hh // 2 + 1) * HEAD_PAD]
        if hh % 2:
            piece = pltpu.roll(piece, MLA_NOPE, 1)
        k_ref[0, hh] = jnp.where(nope_lanes, piece, kpe).astype(BF16)
        v_ref[0, hh, :MLA_V, :] = vt[hh * MLA_V:(hh + 1) * MLA_V].astype(BF16)
        v_ref[0, hh, MLA_V:, :] = ones_rows

    for br in range(N_BRANCH):
        gate_ref[:, br * d_model:(br + 1) * d_model] = _sigmoid(
            seg(C_GATE + br * d_model, C_GATE + (br + 1) * d_model)).astype(BF16)


def _in_proj(x2, pos2, g, w, invf, qn, wuq, kvn, wuk, wuvt, lb, l, bsz, seq):
    n_tok, d_model = x2.shape
    t = IN_TILE
    nt = seq // t
    tok = lambda w_: pl.BlockSpec((t, w_), lambda i: (i, 0))
    lay = lambda a: _layer_spec(a.shape, l)
    head = pl.BlockSpec((1, MLA_HEADS, t, HEAD_PAD), lambda i: (i // nt, 0, i % nt, 0))
    head_shape = jax.ShapeDtypeStruct((bsz, MLA_HEADS, seq, HEAD_PAD), BF16)
    head_t = pl.BlockSpec((1, MLA_HEADS, MLA_VT_ROWS, t), lambda i: (i // nt, 0, 0, i % nt))
    head_t_shape = jax.ShapeDtypeStruct((bsz, MLA_HEADS, MLA_VT_ROWS, seq), BF16)
    q_scale = LOG2E / math.sqrt(MLA_NOPE + MLA_ROPE)
    return pl.pallas_call(
        functools.partial(_in_proj_kernel, q_scale=q_scale),
        grid=(n_tok // t,),
        in_specs=[tok(d_model), pl.BlockSpec((1, 1, t), lambda i: (i, 0, 0)), lay(g), lay(w),
                  _const_spec(invf.shape), lay(qn), lay(wuq), lay(kvn), lay(wuk), lay(wuvt), lay(lb)],
        out_specs=[tok(SSM_WIDTH), head, head, head_t, tok(HG_WIDTH), tok(HG_WIDTH), tok(HG_WIDTH), tok(HG_WIDTH),
                   tok(N_BRANCH * d_model)],
        out_shape=[jax.ShapeDtypeStruct((n_tok, SSM_WIDTH), F32), head_shape, head_shape, head_t_shape,
                   jax.ShapeDtypeStruct((n_tok, HG_WIDTH), BF16), jax.ShapeDtypeStruct((n_tok, HG_WIDTH), F32),
                   jax.ShapeDtypeStruct((n_tok, HG_WIDTH), BF16), jax.ShapeDtypeStruct((n_tok, HG_WIDTH), BF16),
                   jax.ShapeDtypeStruct((n_tok, N_BRANCH * d_model), BF16)],
        compiler_params=_params(("parallel",)),
        name="in_proj",
    )(x2, pos2, g, w, invf, qn, wuq, kvn, wuk, wuvt, lb)


def _s5_kernel(u_ref, wb_ref, a_ref, bsw_ref, wc_ref, d_ref, wglu_ref, y_ref, se_ref, so_ref, st_ref, utb_ref,
               ytb_ref):
    bsz, tc, _ = u_ref.shape
    rows = bsz * tc
    d_model = y_ref.shape[2]
    qw = SSM_CH // SSM_QUARTERS
    uw = SSM_WIDTH // SSM_QUARTERS
    half = SUBLANES // 2

    @pl.when(pl.program_id(0) == 0)
    def _():
        st_ref[...] = jnp.zeros_like(st_ref)

    for q in range(SSM_QUARTERS):
        for b in range(bsz):
            utb_ref[q, pl.ds(b, tc, stride=bsz), :] = u_ref[b, :, q * uw:(q + 1) * uw]

    low = (lax.broadcasted_iota(jnp.int32, (rows, qw), 0) % SUBLANES) < half
    for q in range(SSM_QUARTERS):
        o = _dot(utb_ref[q].astype(BF16), wb_ref[q])
        re = o[:, :qw]
        im = o[:, qw:]
        se_ref[:, q * qw:(q + 1) * qw] = jnp.where(low, re, pltpu.roll(im, half, 0))
        so_ref[:, q * qw:(q + 1) * qw] = jnp.where(low, pltpu.roll(re, rows - half, 0), im)

    n_blk = 2
    cw = SSM_CH // n_blk
    for cb in range(n_blk):
        cs = slice(cb * cw, (cb + 1) * cw)
        a = a_ref[:, cs]
        bs = bsw_ref[:, cs]

        def body(v, s, cs=cs, a=a, bs=bs):
            r = pl.multiple_of(v * SUBLANES, SUBLANES)
            s = a * s + bs * pltpu.roll(s, half, 0) + se_ref[pl.ds(r, SUBLANES), cs]
            se_ref[pl.ds(r, SUBLANES), cs] = s
            s = a * s + bs * pltpu.roll(s, half, 0) + so_ref[pl.ds(r, SUBLANES), cs]
            so_ref[pl.ds(r, SUBLANES), cs] = s
            return s

        st_ref[:, cs] = lax.fori_loop(0, rows // SUBLANES, body, st_ref[:, cs], unroll=4)

    yw = uw
    low_y = low[:, :yw]
    pieces = []
    for q in range(SSM_QUARTERS):
        oe = _dot(se_ref[:, q * qw:(q + 1) * qw].astype(BF16), wc_ref[q])
        oo = _dot(so_ref[:, q * qw:(q + 1) * qw].astype(BF16), wc_ref[q])
        ye = oe[:, :yw] + pltpu.roll(oe[:, yw:], rows - half, 0)
        yo = pltpu.roll(oo[:, :yw], half, 0) + oo[:, yw:]
        pieces.append(jnp.where(low_y, ye, yo) + d_ref[:, q * uw:(q + 1) * uw] * utb_ref[q])
    y = jax.nn.gelu(jnp.concatenate(pieces, axis=1)).astype(BF16)
    z = _dot(y, wglu_ref[...])
    res = z[:, :d_model] * _sigmoid(z[:, d_model:])
    n_slab = d_model // uw
    for j in range(n_slab):
        ytb_ref[j] = res[:, j * uw:(j + 1) * uw]
    for j in range(n_slab):
        for b in range(bsz):
            y_ref[b, :, j * uw:(j + 1) * uw] = ytb_ref[j, pl.ds(b, tc, stride=bsz), :]


def _s5(u3, wb, a, bsw, wc, d, wglu, l, d_model):
    bsz, seq, _ = u3.shape
    tc = S5_TIME
    rows = tc * bsz
    lanes = SSM_WIDTH // SSM_QUARTERS
    return pl.pallas_call(
        _s5_kernel,
        grid=(seq // tc,),
        in_specs=[pl.BlockSpec((bsz, tc, SSM_WIDTH), lambda i: (0, i, 0))]
        + [_layer_spec(w_.shape, l) for w_ in (wb, a, bsw, wc, d, wglu)],
        out_specs=pl.BlockSpec((bsz, tc, d_model), lambda i: (0, i, 0)),
        out_shape=jax.ShapeDtypeStruct((bsz, seq, d_model), F32),
        scratch_shapes=[pltpu.VMEM((rows, SSM_CH), F32), pltpu.VMEM((rows, SSM_CH), F32),
                        pltpu.VMEM((SUBLANES, SSM_CH), F32), pltpu.VMEM((SSM_QUARTERS, rows, lanes), F32),
                        pltpu.VMEM((d_model // lanes, rows, lanes), F32)],
        compiler_params=_params(("arbitrary",)),
        name="s5",
    )(u3, wb, a, bsw, wc, d, wglu)


def _mla_kernel(qi_ref, kj_ref, q_ref, k_ref, vt_ref, o_ref, m_ref, acc_ref):
    p = pl.program_id(1)
    qi = qi_ref[p]
    kj = kj_ref[p]
    blk = q_ref.shape[2]

    @pl.when(kj == 0)
    def _():
        m_ref[...] = jnp.full_like(m_ref, -1e30)
        acc_ref[...] = jnp.zeros_like(acc_ref)

    def step(masked):
        w = blk // MLA_Q_SPLIT
        if masked:
            key = lax.broadcasted_iota(jnp.int32, (blk, w), 0)
            qry = lax.broadcasted_iota(jnp.int32, (blk, w), 1)
        items = [(hh, j) for hh in range(MLA_HEADS) for j in range(MLA_Q_SPLIT)]
        sc, mx = {}, {}
        for t in range(len(items) + 2):
            if t < len(items):
                hh, j = items[t]
                s = _dot_nt(k_ref[0, hh], q_ref[0, hh, j * w:(j + 1) * w, :])
                sc[t] = jnp.where(key <= qry + j * w, s, -1e30) if masked else s
            if 1 <= t <= len(items):
                hh, j = items[t - 1]
                m_prev = m_ref[hh, :, j * w:(j + 1) * w]
                m_new = jnp.maximum(m_prev, jnp.max(sc[t - 1], axis=0, keepdims=True))
                mx[t - 1] = (m_new, jnp.exp2(m_prev - m_new))
                m_ref[hh, :, j * w:(j + 1) * w] = m_new
            if t >= 2:
                hh, j = items[t - 2]
                m_new, alpha = mx.pop(t - 2)
                pr = jnp.exp2(sc.pop(t - 2) - m_new).astype(BF16)
                cols = slice(j * w, (j + 1) * w)
                acc_ref[hh, :, cols] = alpha * acc_ref[hh, :, cols] + _dot(vt_ref[0, hh], pr)

    @pl.when(kj < qi)
    def _():
        step(False)

    @pl.when(kj == qi)
    def _():
        step(True)
        for hh in range(MLA_HEADS):
            acc = acc_ref[hh]
            o_ref[0, hh] = (acc[:MLA_V, :] / acc[MLA_V:MLA_V + 1, :]).astype(BF16)


def _mla(q, k, vt):
    bsz, nh, seq, hd = q.shape
    blk = MLA_BLOCK
    nb = seq // blk
    pairs = [(i, j) for i in range(nb) for j in range(i + 1)]
    qi = jnp.asarray([p[0] for p in pairs], jnp.int32)
    kj = jnp.asarray([p[1] for p in pairs], jnp.int32)
    qspec = pl.BlockSpec((1, nh, blk, hd), lambda b, p, qi, kj: (b, 0, qi[p], 0))
    kspec = pl.BlockSpec((1, nh, blk, hd), lambda b, p, qi, kj: (b, 0, kj[p], 0))
    vspec = pl.BlockSpec((1, nh, MLA_VT_ROWS, blk), lambda b, p, qi, kj: (b, 0, 0, kj[p]))
    ospec = pl.BlockSpec((1, nh, MLA_V, blk), lambda b, p, qi, kj: (b, 0, 0, qi[p]))
    return pl.pallas_call(
        _mla_kernel,
        grid_spec=pltpu.PrefetchScalarGridSpec(
            num_scalar_prefetch=2,
            grid=(bsz, len(pairs)),
            in_specs=[qspec, kspec, vspec],
            out_specs=ospec,
            scratch_shapes=[pltpu.VMEM((nh, 1, blk), F32), pltpu.VMEM((nh, MLA_VT_ROWS, blk), F32)],
        ),
        out_shape=jax.ShapeDtypeStruct((bsz, nh, MLA_V, seq), BF16),
        compiler_params=_params(("parallel", "arbitrary")),
        name="mla",
    )(qi, kj, q, k, vt)


def _hgrn_kernel(q_ref, f_ref, v_ref, g_ref, gn_ref, o_ref, st_ref, ks_ref):
    c = HG_CHUNK
    sub = HG_SUB
    n_sub = c // sub

    @pl.when(pl.program_id(1) == 0)
    def _():
        st_ref[...] = jnp.zeros_like(st_ref)
        ks_ref[...] = jnp.zeros_like(ks_ref)

    chains = [(hh, ci) for hh in range(HG_HEADS) for ci in range(HG_STEP_CHUNKS)]

    def blk(ref, ch):
        hh, ci = ch
        return ref[ci * c:(ci + 1) * c, hh * HG_D:(hh + 1) * HG_D]

    row = lax.broadcasted_iota(jnp.int32, (c, c), 0)
    col = lax.broadcasted_iota(jnp.int32, (c, c), 1)
    tri = (col <= row).astype(BF16)
    lane = lax.broadcasted_iota(jnp.int32, (sub, HG_D), 1)
    trow = lax.broadcasted_iota(jnp.int32, (sub, HG_D), 0)

    q, k, b, bk = {}, {}, {}, {}
    for ch in chains:
        f = blk(f_ref, ch)
        k[ch] = 1.0 - f
        lf = jnp.log(f) * LOG2E
        l1 = lf.astype(BF16)
        r1 = lf - l1.astype(F32)
        l2 = r1.astype(BF16)
        l3 = (r1 - l2.astype(F32)).astype(BF16)
        b[ch] = _dot(tri, l1) + _dot(tri, l2) + _dot(tri, l3)
    for ch in chains:
        q[ch] = blk(q_ref, ch).astype(F32)
        bk[ch] = b[ch] - jnp.log(k[ch]) * LOG2E

    qb, kdec, ebl, o_inter = {}, {}, {}, {}
    for ch in chains:
        bl = b[ch][c - 1:c, :]
        qb[ch] = (q[ch] * jnp.exp2(b[ch])).astype(BF16)
        kdec[ch] = (k[ch] * jnp.exp2(bl - b[ch])).astype(BF16)
        ebl[ch] = jnp.exp2(bl)
    for hh in range(HG_HEADS):
        st = st_ref[hh]
        for ci in range(HG_STEP_CHUNKS):
            ch = (hh, ci)
            o_inter[ch] = _dot_nt(qb[ch], st.astype(BF16))
            st = st * ebl[ch] + _dot_tn(blk(v_ref, ch), kdec[ch])
        st_ref[hh] = st

    a_rows = {ch: [] for ch in chains}
    for i in range(n_sub):
        rs = slice(i * sub, (i + 1) * sub)
        tiles = {ch: jnp.zeros((sub, HG_D), F32) for ch in chains}
        for ss in range(sub):
            s = i * sub + ss
            for ch in chains:
                kdec_s = jnp.exp2(jnp.minimum(b[ch][rs] - bk[ch][s:s + 1, :], 0.0))
                a_s = jnp.sum(q[ch][rs] * kdec_s, axis=-1, keepdims=True)
                tiles[ch] = jnp.where(lane == s, a_s, tiles[ch])
        for ch in chains:
            a_rows[ch].append(jnp.where(trow + i * sub >= lane, tiles[ch], 0.0)[:, :c])
    for i in range(1, n_sub):
        rs = slice(i * sub, (i + 1) * sub)
        for n, ch in enumerate(chains):
            r = b[ch][i * sub - 1:i * sub, :]
            ks_ref[n, i - 1, :i * sub, :] = k[ch][:i * sub] * jnp.exp2(r - b[ch][:i * sub])
            qs = (q[ch][rs] * jnp.exp2(b[ch][rs] - r)).astype(BF16)
            a_rows[ch][i] = a_rows[ch][i] + _dot_nt(qs, ks_ref[n, i - 1].astype(BF16))
    o = {}
    for ch in chains:
        a = jnp.concatenate(a_rows[ch], axis=0).astype(BF16)
        o[ch] = o_inter[ch] + _dot(a, blk(v_ref, ch))

    gn = gn_ref[...]
    for ch in chains:
        hh, ci = ch
        on = o[ch] * lax.rsqrt(jnp.mean(o[ch] * o[ch], axis=-1, keepdims=True) + RMS_EPS) * gn
        o_ref[ci * c:(ci + 1) * c, hh * HG_D:(hh + 1) * HG_D] = (on * blk(g_ref, ch).astype(F32)).astype(BF16)


def _hgrn(hq, hf, hi, hg, gn, l, bsz, seq):
    n_tok = hq.shape[0]
    c = HG_CHUNK
    rows = c * HG_STEP_CHUNKS
    nc = seq // rows
    spec = pl.BlockSpec((rows, HG_WIDTH), lambda b, i: (b * nc + i, 0))
    return pl.pallas_call(
        _hgrn_kernel,
        grid=(bsz, nc),
        in_specs=[spec, spec, spec, spec, _layer_spec(gn.shape, l)],
        out_specs=spec,
        out_shape=jax.ShapeDtypeStruct((n_tok, HG_WIDTH), BF16),
        scratch_shapes=[pltpu.VMEM((HG_HEADS, HG_D, HG_D), F32),
                        pltpu.VMEM((HG_HEADS * HG_STEP_CHUNKS, c // HG_SUB - 1, c, HG_D), F32)],
        compiler_params=_params(("parallel", "arbitrary")),
        name="hgrn",
    )(hq, hf, hi, hg, gn)


def _memkv_kernel(mem_ref, g_ref, w_ref, kv_ref):
    kv_ref[...] = _dot(_rms(mem_ref[...], g_ref[...]).astype(BF16), w_ref[...]).astype(BF16)


def _memkv(mem2, g, w, l, n_mem):
    n_rows, d_model = mem2.shape
    return pl.pallas_call(
        _memkv_kernel,
        grid=(n_rows // n_mem,),
        in_specs=[pl.BlockSpec((n_mem, d_model), lambda i: (i, 0)), _layer_spec(g.shape, l), _layer_spec(w.shape, l)],
        out_specs=pl.BlockSpec((n_mem, 2 * X_WIDTH), lambda i: (i, 0)),
        out_shape=jax.ShapeDtypeStruct((n_rows, 2 * X_WIDTH), BF16),
        compiler_params=_params(("parallel",)),
        name="memkv",
    )(mem2, g, w)


def _mid_kernel(x_ref, ys_ref, om_ref, oh_ref, gate_ref, kv_ref, wmo_ref, who_ref, wout_ref, gx_ref, wq_ref,
                wxo_ref, o_ref, *, x_scale):
    d_model = x_ref.shape[1]
    om_t = om_ref[0].reshape(MLA_HEADS * MLA_V, x_ref.shape[0])
    y_mla = _dot_tn(om_t, wmo_ref[...])
    y_hg = _dot(oh_ref[...], who_ref[...])
    gate = lambda br: gate_ref[:, br * d_model:(br + 1) * d_model].astype(F32)
    merged = gate(0) * ys_ref[...] + gate(1) * y_mla + gate(2) * y_hg
    x = x_ref[...] + _dot(merged.astype(BF16), wout_ref[...])

    q = (_dot(_rms(x, gx_ref[...]).astype(BF16), wq_ref[...]) * x_scale).astype(BF16)
    outs = []
    for hh in range(X_HEADS):
        sl = slice(hh * X_HEAD_DIM, (hh + 1) * X_HEAD_DIM)
        s = _dot_nt(q[:, sl], kv_ref[:, sl])
        pr = jnp.exp2(s - jnp.max(s, axis=-1, keepdims=True))
        den = jnp.sum(pr, axis=-1, keepdims=True)
        vh = kv_ref[:, X_WIDTH + hh * X_HEAD_DIM:X_WIDTH + (hh + 1) * X_HEAD_DIM]
        outs.append((_dot(pr.astype(BF16), vh) / den).astype(BF16))
    o_ref[...] = x + _dot(jnp.concatenate(outs, axis=1), wxo_ref[...])


def _mid(x2, ys, om, oh, gates, kv, wmo, who, wout, gx, wq, wxo, l, seq, n_mem):
    n_tok, d_model = x2.shape
    t = MID_TILE
    nt = seq // t
    tok = lambda w_: pl.BlockSpec((t, w_), lambda i: (i, 0))
    return pl.pallas_call(
        functools.partial(_mid_kernel, x_scale=LOG2E / math.sqrt(X_HEAD_DIM)),
        grid=(n_tok // t,),
        in_specs=[tok(d_model), tok(d_model),
                  pl.BlockSpec((1, MLA_HEADS, MLA_V, t), lambda i: (i // nt, 0, 0, i % nt)),
                  tok(HG_WIDTH), tok(N_BRANCH * d_model),
                  pl.BlockSpec((n_mem, 2 * X_WIDTH), lambda i: (i // nt, 0)),
                  ] + [_layer_spec(w_.shape, l) for w_ in (wmo, who, wout, gx, wq, wxo)],
        out_specs=tok(d_model),
        out_shape=jax.ShapeDtypeStruct((n_tok, d_model), F32),
        compiler_params=_params(("parallel",)),
        name="mid",
    )(x2, ys, om, oh, gates, kv, wmo, who, wout, gx, wq, wxo)


def _ffn_kernel(x_ref, g_ref, wgu_ref, wd_ref, gf_ref, o_ref, *, final_norm):
    d_ff = wd_ref.shape[0]
    x = x_ref[...]
    h = _rms(x, g_ref[...]).astype(BF16)
    y = x
    for c0 in range(0, d_ff, FFN_CHUNK):
        gt = _dot(h, wgu_ref[:, c0:c0 + FFN_CHUNK])
        up = _dot(h, wgu_ref[:, d_ff + c0:d_ff + c0 + FFN_CHUNK])
        act = (gt * _sigmoid(gt) * up).astype(BF16)
        y = y + _dot(act, wd_ref[c0:c0 + FFN_CHUNK, :])
    if final_norm:
        y = _rms(y, gf_ref[...])
    o_ref[...] = y


def _ffn(x2, g, wgu, wd, gf, l, final_norm):
    n_tok, d_model = x2.shape
    t = FFN_TILE
    tok = pl.BlockSpec((t, d_model), lambda i: (i, 0))
    return pl.pallas_call(
        functools.partial(_ffn_kernel, final_norm=final_norm),
        grid=(n_tok // t,),
        in_specs=[tok, _layer_spec(g.shape, l), _layer_spec(wgu.shape, l), _layer_spec(wd.shape, l),
                  _const_spec(gf.shape)],
        out_specs=tok,
        out_shape=jax.ShapeDtypeStruct((n_tok, d_model), F32),
        compiler_params=_params(("parallel",)),
        name="ffn",
    )(x2, g, wgu, wd, gf)


def _in_proj_weights(w_in):
    kpe = jnp.pad(w_in[:, :, C_KPE:C_KPE + MLA_ROPE], ((0, 0), (0, 0), (MLA_NOPE, HEAD_PAD - MLA_NOPE - MLA_ROPE)))
    cols = [w_in[:, :, :C_KPE], kpe, w_in[:, :, C_KPE + MLA_ROPE:]]
    return jnp.concatenate(cols, axis=2).astype(BF16)


def _mla_weights(w_uq, w_ukv):
    depth, rank, _ = w_uq.shape
    dq = MLA_NOPE + MLA_ROPE
    wq = w_uq.reshape(depth, rank, MLA_HEADS, dq)
    wuq = jnp.pad(wq, ((0, 0), (0, 0), (0, 0), (0, HEAD_PAD - dq))).reshape(depth, rank, MLA_HEADS * HEAD_PAD)
    kvr = w_ukv.shape[1]
    wkv = w_ukv.reshape(depth, kvr, MLA_HEADS, MLA_NOPE + MLA_V)
    wuk = wkv[..., :MLA_NOPE].reshape(depth, kvr, -1)
    wuvt = wkv[..., MLA_NOPE:].reshape(depth, kvr, -1).transpose(0, 2, 1)
    return wuq.astype(BF16), wuk.astype(BF16), wuvt.astype(BF16)


def _s5_weights(lam_re, lam_im, b_re, b_im, c_re, c_im, log_step):
    depth = lam_re.shape[0]
    step = jnp.exp(log_step)[..., None]
    mag = jnp.exp(lam_re * step)
    lbr = mag * jnp.cos(lam_im * step)
    lbi = mag * jnp.sin(lam_im * step)
    den = lam_re * lam_re + lam_im * lam_im
    cr = ((lbr - 1.0) * lam_re + lbi * lam_im) / den
    ci = (lbi * lam_re - (lbr - 1.0) * lam_im) / den
    bbr = cr[..., None] * b_re - ci[..., None] * b_im
    bbi = cr[..., None] * b_im + ci[..., None] * b_re
    gq = SSM_GROUPS // SSM_QUARTERS
    eye = jnp.eye(gq, dtype=F32)

    def blockdiag(m):
        m = m.reshape(depth, SSM_QUARTERS, gq, m.shape[2], m.shape[3]).transpose(0, 1, 2, 4, 3)
        out = m[:, :, :, :, None, :] * eye[None, None, :, None, :, None]
        return out.reshape(depth, SSM_QUARTERS, gq * m.shape[3], gq * m.shape[4])

    wb = jnp.concatenate([blockdiag(bbr), blockdiag(bbi)], axis=3).astype(BF16)
    wc = jnp.concatenate([blockdiag(c_re), -blockdiag(c_im)], axis=3).astype(BF16)
    lr = lbr.reshape(depth, 1, SSM_CH)
    li = lbi.reshape(depth, 1, SSM_CH)
    half = SUBLANES // 2
    a = jnp.broadcast_to(lr, (depth, SUBLANES, SSM_CH))
    bsw = jnp.concatenate([jnp.broadcast_to(-li, (depth, half, SSM_CH)), jnp.broadcast_to(li, (depth, half, SSM_CH))],
                          axis=1)
    return wb, a, bsw, wc


def kernel(x, mem, positions, norm_mix, w_in, ssm_lam_re, ssm_lam_im, ssm_b_re, ssm_b_im, ssm_c_re, ssm_c_im, ssm_d, ssm_log_step, ssm_w_glu, mla_q_norm, mla_kv_norm, mla_w_uq, mla_w_ukv, mla_w_o, hg_lb, hg_g_norm, hg_w_o, w_out, norm_cross, norm_mem, x_w_q, x_w_kv, x_w_o, norm_ffn, ffn_w_gate_up, ffn_w_down, norm_final):
    bsz, seq, d_model = x.shape
    depth = w_in.shape[0]
    n_mem = mem.shape[1]
    assert 2 * bsz == SUBLANES, "the s5 scan packs (re, im) x batch on the sublanes of one tile"
    n_tok = bsz * seq

    half = MLA_ROPE // 2
    invf = (ROPE_THETA ** (-jnp.arange(half, dtype=F32) / half)).reshape(half, 1)
    lb_p = jax.nn.softmax(hg_lb.astype(F32), axis=0)
    lower_bounds = jnp.cumsum(lb_p, axis=0) - lb_p[0:1]

    rows = lambda a: a.reshape(depth, 1, -1)
    bf = lambda a: a.astype(BF16)
    w_inp = _in_proj_weights(w_in)
    wuq, wuk, wuvt = _mla_weights(mla_w_uq, mla_w_ukv)
    wb, a, bsw, wc = _s5_weights(ssm_lam_re, ssm_lam_im, ssm_b_re, ssm_b_im, ssm_c_re, ssm_c_im, ssm_log_step)
    wglu, wmo, who, wout = bf(ssm_w_glu), bf(mla_w_o), bf(hg_w_o), bf(w_out)
    wxq, wxkv, wxo, wgu, wdn = bf(x_w_q), bf(x_w_kv), bf(x_w_o), bf(ffn_w_gate_up), bf(ffn_w_down)
    g_mix, g_q, g_kv, g_hg = rows(norm_mix), rows(mla_q_norm), rows(mla_kv_norm), rows(hg_g_norm)
    g_cross, g_mem, g_ffn = rows(norm_cross), rows(norm_mem), rows(norm_ffn)
    lbs, ssm_dr = rows(lower_bounds), rows(ssm_d)

    x2 = x.reshape(n_tok, d_model)
    mem2 = mem.reshape(bsz * n_mem, d_model)
    pos2 = positions.reshape(n_tok // IN_TILE, 1, IN_TILE)

    for l in range(depth):
        u, q, k, v, hq, hf, hi, hg, gates = _in_proj(x2, pos2, g_mix, w_inp, invf, g_q, wuq, g_kv, wuk, wuvt,
                                                     lbs, l, bsz, seq)
        ys = _s5(u.reshape(bsz, seq, SSM_WIDTH), wb, a, bsw, wc, ssm_dr, wglu, l, d_model).reshape(n_tok, d_model)
        om = _mla(q, k, v)
        oh = _hgrn(hq, hf, hi, hg, g_hg, l, bsz, seq)
        kv = _memkv(mem2, g_mem, wxkv, l, n_mem)
        x2 = _mid(x2, ys, om, oh, gates, kv, wmo, who, wout, g_cross, wxq, wxo, l, seq, n_mem)
        x2 = _ffn(x2, g_ffn, wgu, wdn, norm_final.reshape(1, -1), l, final_norm=(l == depth - 1))
    return x2.reshape(bsz, seq, d_model)
```

```python
import functools
import math

import jax
import jax.numpy as jnp
from jax import lax
from jax.experimental import pallas as pl
from jax.experimental.pallas import tpu as pltpu

F32 = jnp.float32
BF16 = jnp.bfloat16

SUBLANES = 8
LANES = 128

RMS_EPS = 1e-6
LOG2E = 1.4426950408889634

SSM_GROUPS = 32
SSM_GROUP_CH = 16
SSM_STATE = 64
SSM_WIDTH = SSM_GROUPS * SSM_GROUP_CH
SSM_CH = SSM_GROUPS * SSM_STATE
SSM_QUARTERS = 4

MLA_HEADS = 8
MLA_Q_RANK = 512
MLA_KV_RANK = 256
MLA_NOPE = 64
MLA_ROPE = 32
MLA_V = 64
ROPE_THETA = 10000.0
HEAD_PAD = LANES
MLA_VT_ROWS = 80

HG_HEADS = 4
HG_D = 128
HG_WIDTH = HG_HEADS * HG_D
HG_CHUNK = 64
HG_SUB = SUBLANES
HG_STEP_CHUNKS = 8

X_HEADS = 4
X_HEAD_DIM = 128
X_WIDTH = X_HEADS * X_HEAD_DIM

N_BRANCH = 3

VMEM_LIMIT = 56 * 1024 * 1024

IN_TILE = 512
S5_TIME = 128
MLA_BLOCK = 512
MID_TILE = 512
FFN_TILE = 512
FFN_CHUNK = 2816


def _dot(a, b):
    return jnp.dot(a, b, preferred_element_type=F32)


def _dot_nt(a, b):
    return lax.dot_general(a, b, (((1,), (1,)), ((), ())), preferred_element_type=F32)


def _dot_tn(a, b):
    return lax.dot_general(a, b, (((0,), (0,)), ((), ())), preferred_element_type=F32)


def _rms(x, g):
    return x * lax.rsqrt(jnp.mean(x * x, axis=-1, keepdims=True) + RMS_EPS) * g


def _sigmoid(x):
    return 1.0 / (1.0 + jnp.exp(-x))


def _const_spec(shape):
    nd = len(shape)
    return pl.BlockSpec(shape, lambda *_: (0,) * nd, pipeline_mode=pl.Buffered(1))


def _layer_spec(shape, l):
    nd = len(shape)
    return pl.BlockSpec((None,) + tuple(shape[1:]), lambda *_: (l,) + (0,) * (nd - 1),
                        pipeline_mode=pl.Buffered(1))


def _params(sem):
    return pltpu.CompilerParams(dimension_semantics=sem, vmem_limit_bytes=VMEM_LIMIT)


C_SSM = 0
C_QLAT = C_SSM + SSM_WIDTH
C_KVLAT = C_QLAT + MLA_Q_RANK
C_KPE = C_KVLAT + MLA_KV_RANK
C_HQ = C_KPE + HEAD_PAD
C_HF = C_HQ + HG_WIDTH
C_HI = C_HF + HG_WIDTH
C_HG = C_HI + HG_WIDTH
C_GATE = C_HG + HG_WIDTH


def _in_proj_kernel(x_ref, pos_ref, g_ref, w_ref, invf_ref, qn_ref, wuq_ref, kvn_ref, wuk_ref, wuvt_ref, lb_ref,
                    u_ref, q_ref, k_ref, v_ref, hq_ref, hf_ref, hi_ref, hg_ref, gate_ref, *, q_scale):
    d_model = x_ref.shape[1]
    h = _rms(x_ref[...], g_ref[...]).astype(BF16)

    def seg(a, b):
        return _dot(h, w_ref[:, a:b])

    qlat = seg(C_QLAT, C_KVLAT)
    kvlat = seg(C_KVLAT, C_KPE)
    kp = seg(C_KPE, C_HQ)
    qn = _rms(qlat, qn_ref[...]).astype(BF16)
    kvn = _rms(kvlat, kvn_ref[...]).astype(BF16)

    t = x_ref.shape[0]
    ang = invf_ref[...] * pos_ref[0].astype(F32)
    cos_c = jnp.cos(ang)
    sin_c = jnp.sin(ang)
    fill = lambda v, n: jnp.full((n, t), v, F32)
    tail = HEAD_PAD - MLA_NOPE - MLA_ROPE
    cos = jnp.concatenate([fill(1.0, MLA_NOPE), cos_c, cos_c, fill(1.0, tail)], axis=0).T
    sin = jnp.concatenate([fill(0.0, MLA_NOPE), -sin_c, sin_c, fill(0.0, tail)], axis=0).T
    lane = lax.broadcasted_iota(jnp.int32, (1, HEAD_PAD), 1)
    first_half = lane < MLA_NOPE + MLA_ROPE // 2

    def rope(v):
        swapped = jnp.where(first_half, pltpu.roll(v, HEAD_PAD - MLA_ROPE // 2, 1),
                            pltpu.roll(v, MLA_ROPE // 2, 1))
        return v * cos + swapped * sin

    u_ref[...] = seg(C_SSM, C_QLAT)
    hq = seg(C_HQ, C_HF)
    hq_ref[...] = (hq * _sigmoid(hq)).astype(BF16)
    lb = lb_ref[...]
    hf_ref[...] = lb + (1.0 - lb) * _sigmoid(seg(C_HF, C_HI))
    hi_ref[...] = seg(C_HI, C_HG).astype(BF16)
    hg = seg(C_HG, C_GATE)
    hg_ref[...] = (hg * _sigmoid(hg)).astype(BF16)

    qa = _dot(qn, wuq_ref[...])
    for hh in range(MLA_HEADS):
        q_ref[0, hh] = (rope(qa[:, hh * HEAD_PAD:(hh + 1) * HEAD_PAD]) * q_scale).astype(BF16)

    kn = _dot(kvn, wuk_ref[...])
    vt = _dot_nt(wuvt_ref[...], kvn)
    kpe = rope(kp)
    nope_lanes = lane < MLA_NOPE
    ones_rows = (lax.broadcasted_iota(jnp.int32, (MLA_VT_ROWS - MLA_V, t), 0) == 0).astype(BF16)
    for hh in range(MLA_HEADS):
        piece = kn[:, (hh // 2) * HEAD_PAD:(hh // 2 + 1) * HEAD_PAD]
        if hh % 2:
            piece = pltpu.roll(piece, MLA_NOPE, 1)
        k_ref[0, hh] = jnp.where(nope_lanes, piece, kpe).astype(BF16)
        v_ref[0, hh, :MLA_V, :] = vt[hh * MLA_V:(hh + 1) * MLA_V].astype(BF16)
        v_ref[0, hh, MLA_V:, :] = ones_rows

    for br in range(N_BRANCH):
        gate_ref[:, br * d_model:(br + 1) * d_model] = _sigmoid(
            seg(C_GATE + br * d_model, C_GATE + (br + 1) * d_model)).astype(BF16)


def _in_proj(x2, pos2, g, w, invf, qn, wuq, kvn, wuk, wuvt, lb, l, bsz, seq):
    n_tok, d_model = x2.shape
    t = IN_TILE
    nt = seq // t
    tok = lambda w_: pl.BlockSpec((t, w_), lambda i: (i, 0))
    lay = lambda a: _layer_spec(a.shape, l)
    head = pl.BlockSpec((1, MLA_HEADS, t, HEAD_PAD), lambda i: (i // nt, 0, i % nt, 0))
    head_shape = jax.ShapeDtypeStruct((bsz, MLA_HEADS, seq, HEAD_PAD), BF16)
    head_t = pl.BlockSpec((1, MLA_HEADS, MLA_VT_ROWS, t), lambda i: (i // nt, 0, 0, i % nt))
    head_t_shape = jax.ShapeDtypeStruct((bsz, MLA_HEADS, MLA_VT_ROWS, seq), BF16)
    q_scale = LOG2E / math.sqrt(MLA_NOPE + MLA_ROPE)
    return pl.pallas_call(
        functools.partial(_in_proj_kernel, q_scale=q_scale),
        grid=(n_tok // t,),
        in_specs=[tok(d_model), pl.BlockSpec((1, 1, t), lambda i: (i, 0, 0)), lay(g), lay(w),
                  _const_spec(invf.shape), lay(qn), lay(wuq), lay(kvn), lay(wuk), lay(wuvt), lay(lb)],
        out_specs=[tok(SSM_WIDTH), head, head, head_t, tok(HG_WIDTH), tok(HG_WIDTH), tok(HG_WIDTH), tok(HG_WIDTH),
                   tok(N_BRANCH * d_model)],
        out_shape=[jax.ShapeDtypeStruct((n_tok, SSM_WIDTH), F32), head_shape, head_shape, head_t_shape,
                   jax.ShapeDtypeStruct((n_tok, HG_WIDTH), BF16), jax.ShapeDtypeStruct((n_tok, HG_WIDTH), F32),
                   jax.ShapeDtypeStruct((n_tok, HG_WIDTH), BF16), jax.ShapeDtypeStruct((n_tok, HG_WIDTH), BF16),
                   jax.ShapeDtypeStruct((n_tok, N_BRANCH * d_model), BF16)],
        compiler_params=_params(("parallel",)),
        name="in_proj",
    )(x2, pos2, g, w, invf, qn, wuq, kvn, wuk, wuvt, lb)


def _s5_kernel(u_ref, wb_ref, a_ref, bsw_ref, wc_ref, d_ref, wglu_ref, y_ref, se_ref, so_ref, st_ref, utb_ref,
               ytb_ref):
    bsz, tc, _ = u_ref.shape
    rows = bsz * tc
    d_model = y_ref.shape[2]
    qw = SSM_CH // SSM_QUARTERS
    uw = SSM_WIDTH // SSM_QUARTERS
    half = SUBLANES // 2

    @pl.when(pl.program_id(0) == 0)
    def _():
        st_ref[...] = jnp.zeros_like(st_ref)

    for q in range(SSM_QUARTERS):
        for b in range(bsz):
            utb_ref[q, pl.ds(b, tc, stride=bsz), :] = u_ref[b, :, q * uw:(q + 1) * uw]

    low = (lax.broadcasted_iota(jnp.int32, (rows, qw), 0) % SUBLANES) < half
    for q in range(SSM_QUARTERS):
        o = _dot(utb_ref[q].astype(BF16), wb_ref[q])
        re = o[:, :qw]
        im = o[:, qw:]
        se_ref[:, q * qw:(q + 1) * qw] = jnp.where(low, re, pltpu.roll(im, half, 0))
        so_ref[:, q * qw:(q + 1) * qw] = jnp.where(low, pltpu.roll(re, rows - half, 0), im)

    n_blk = 2
    cw = SSM_CH // n_blk
    for cb in range(n_blk):
        cs = slice(cb * cw, (cb + 1) * cw)
        a = a_ref[:, cs]
        bs = bsw_ref[:, cs]

        def body(v, s, cs=cs, a=a, bs=bs):
            r = pl.multiple_of(v * SUBLANES, SUBLANES)
            s = a * s + bs * pltpu.roll(s, half, 0) + se_ref[pl.ds(r, SUBLANES), cs]
            se_ref[pl.ds(r, SUBLANES), cs] = s
            s = a * s + bs * pltpu.roll(s, half, 0) + so_ref[pl.ds(r, SUBLANES), cs]
            so_ref[pl.ds(r, SUBLANES), cs] = s
            return s

        st_ref[:, cs] = lax.fori_loop(0, rows // SUBLANES, body, st_ref[:, cs], unroll=4)

    yw = uw
    low_y = low[:, :yw]
    pieces = []
    for q in range(SSM_QUARTERS):
        oe = _dot(se_ref[:, q * qw:(q + 1) * qw].astype(BF16), wc_ref[q])
        oo = _dot(so_ref[:, q * qw:(q + 1) * qw].astype(BF16), wc_ref[q])
        ye = oe[:, :yw] + pltpu.roll(oe[:, yw:], rows - half, 0)
        yo = pltpu.roll(oo[:, :yw], half, 0) + oo[:, yw:]
        pieces.append(jnp.where(low_y, ye, yo) + d_ref[:, q * uw:(q + 1) * uw] * utb_ref[q])
    y = jax.nn.gelu(jnp.concatenate(pieces, axis=1)).astype(BF16)
    z = _dot(y, wglu_ref[...])
    res = z[:, :d_model] * _sigmoid(z[:, d_model:])
    n_slab = d_model // uw
    for j in range(n_slab):
        ytb_ref[j] = res[:, j * uw:(j + 1) * uw]
    for j in range(n_slab):
        for b in range(bsz):
            y_ref[b, :, j * uw:(j + 1) * uw] = ytb_ref[j, pl.ds(b, tc, stride=bsz), :]


def _s5(u3, wb, a, bsw, wc, d, wglu, l, d_model):
    bsz, seq, _ = u3.shape
    tc = S5_TIME
    rows = tc * bsz
    lanes = SSM_WIDTH // SSM_QUARTERS
    return pl.pallas_call(
        _s5_kernel,
        grid=(seq // tc,),
        in_specs=[pl.BlockSpec((bsz, tc, SSM_WIDTH), lambda i: (0, i, 0))]
        + [_layer_spec(w_.shape, l) for w_ in (wb, a, bsw, wc, d, wglu)],
        out_specs=pl.BlockSpec((bsz, tc, d_model), lambda i: (0, i, 0)),
        out_shape=jax.ShapeDtypeStruct((bsz, seq, d_model), F32),
        scratch_shapes=[pltpu.VMEM((rows, SSM_CH), F32), pltpu.VMEM((rows, SSM_CH), F32),
                        pltpu.VMEM((SUBLANES, SSM_CH), F32), pltpu.VMEM((SSM_QUARTERS, rows, lanes), F32),
                        pltpu.VMEM((d_model // lanes, rows, lanes), F32)],
        compiler_params=_params(("arbitrary",)),
        name="s5",
    )(u3, wb, a, bsw, wc, d, wglu)


def _mla_kernel(qi_ref, kj_ref, q_ref, k_ref, vt_ref, o_ref, m_ref, acc_ref):
    p = pl.program_id(1)
    qi = qi_ref[p]
    kj = kj_ref[p]
    blk = q_ref.shape[2]

    @pl.when(kj == 0)
    def _():
        m_ref[...] = jnp.full_like(m_ref, -1e30)
        acc_ref[...] = jnp.zeros_like(acc_ref)

    def step(masked):
        if masked:
            key = lax.broadcasted_iota(jnp.int32, (blk, blk), 0)
            qry = lax.broadcasted_iota(jnp.int32, (blk, blk), 1)
            causal = key <= qry
        sc, mx = {}, {}
        for t in range(MLA_HEADS + 2):
            if t < MLA_HEADS:
                s = _dot_nt(k_ref[0, t], q_ref[0, t])
                sc[t] = jnp.where(causal, s, -1e30) if masked else s
            if 1 <= t <= MLA_HEADS:
                hh = t - 1
                m_prev = m_ref[hh]
                m_new = jnp.maximum(m_prev, jnp.max(sc[hh], axis=0, keepdims=True))
                mx[hh] = (m_new, jnp.exp2(m_prev - m_new))
                m_ref[hh] = m_new
            if t >= 2:
                hh = t - 2
                m_new, alpha = mx.pop(hh)
                pr = jnp.exp2(sc.pop(hh) - m_new).astype(BF16)
                acc_ref[hh] = alpha * acc_ref[hh] + _dot(vt_ref[0, hh], pr)

    @pl.when(kj < qi)
    def _():
        step(False)

    @pl.when(kj == qi)
    def _():
        step(True)
        for hh in range(MLA_HEADS):
            acc = acc_ref[hh]
            o_ref[0, hh] = (acc[:MLA_V, :] / acc[MLA_V:MLA_V + 1, :]).astype(BF16)


def _mla(q, k, vt):
    bsz, nh, seq, hd = q.shape
    blk = MLA_BLOCK
    nb = seq // blk
    pairs = [(i, j) for i in range(nb) for j in range(i + 1)]
    qi = jnp.asarray([p[0] for p in pairs], jnp.int32)
    kj = jnp.asarray([p[1] for p in pairs], jnp.int32)
    qspec = pl.BlockSpec((1, nh, blk, hd), lambda b, p, qi, kj: (b, 0, qi[p], 0))
    kspec = pl.BlockSpec((1, nh, blk, hd), lambda b, p, qi, kj: (b, 0, kj[p], 0))
    vspec = pl.BlockSpec((1, nh, MLA_VT_ROWS, blk), lambda b, p, qi, kj: (b, 0, 0, kj[p]))
    ospec = pl.BlockSpec((1, nh, MLA_V, blk), lambda b, p, qi, kj: (b, 0, 0, qi[p]))
    return pl.pallas_call(
        _mla_kernel,
        grid_spec=pltpu.PrefetchScalarGridSpec(
            num_scalar_prefetch=2,
            grid=(bsz, len(pairs)),
            in_specs=[qspec, kspec, vspec],
            out_specs=ospec,
            scratch_shapes=[pltpu.VMEM((nh, 1, blk), F32), pltpu.VMEM((nh, MLA_VT_ROWS, blk), F32)],
        ),
        out_shape=jax.ShapeDtypeStruct((bsz, nh, MLA_V, seq), BF16),
        compiler_params=_params(("parallel", "arbitrary")),
        name="mla",
    )(qi, kj, q, k, vt)


def _hgrn_kernel(q_ref, f_ref, v_ref, g_ref, gn_ref, o_ref, st_ref, ks_ref):
    c = HG_CHUNK
    sub = HG_SUB
    n_sub = c // sub

    @pl.when(pl.program_id(1) == 0)
    def _():
        st_ref[...] = jnp.zeros_like(st_ref)
        ks_ref[...] = jnp.zeros_like(ks_ref)

    chains = [(hh, ci) for hh in range(HG_HEADS) for ci in range(HG_STEP_CHUNKS)]

    def blk(ref, ch):
        hh, ci = ch
        return ref[ci * c:(ci + 1) * c, hh * HG_D:(hh + 1) * HG_D]

    row = lax.broadcasted_iota(jnp.int32, (c, c), 0)
    col = lax.broadcasted_iota(jnp.int32, (c, c), 1)
    tri = (col <= row).astype(BF16)
    lane = lax.broadcasted_iota(jnp.int32, (sub, HG_D), 1)
    trow = lax.broadcasted_iota(jnp.int32, (sub, HG_D), 0)

    q, k, b, bk = {}, {}, {}, {}
    for ch in chains:
        f = blk(f_ref, ch)
        k[ch] = 1.0 - f
        lf = jnp.log(f) * LOG2E
        l1 = lf.astype(BF16)
        r1 = lf - l1.astype(F32)
        l2 = r1.astype(BF16)
        l3 = (r1 - l2.astype(F32)).astype(BF16)
        b[ch] = _dot(tri, l1) + _dot(tri, l2) + _dot(tri, l3)
    for ch in chains:
        q[ch] = blk(q_ref, ch).astype(F32)
        bk[ch] = b[ch] - jnp.log(k[ch]) * LOG2E

    qb, kdec, ebl, o_inter = {}, {}, {}, {}
    for ch in chains:
        bl = b[ch][c - 1:c, :]
        qb[ch] = (q[ch] * jnp.exp2(b[ch])).astype(BF16)
        kdec[ch] = (k[ch] * jnp.exp2(bl - b[ch])).astype(BF16)
        ebl[ch] = jnp.exp2(bl)
    for hh in range(HG_HEADS):
        st = st_ref[hh]
        for ci in range(HG_STEP_CHUNKS):
            ch = (hh, ci)
            o_inter[ch] = _dot_nt(qb[ch], st.astype(BF16))
            st = st * ebl[ch] + _dot_tn(blk(v_ref, ch), kdec[ch])
        st_ref[hh] = st

    a_rows = {ch: [] for ch in chains}
    for i in range(n_sub):
        rs = slice(i * sub, (i + 1) * sub)
        tiles = {ch: jnp.zeros((sub, HG_D), F32) for ch in chains}
        for ss in range(sub):
            s = i * sub + ss
            for ch in chains:
                kdec_s = jnp.exp2(jnp.minimum(b[ch][rs] - bk[ch][s:s + 1, :], 0.0))
                a_s = jnp.sum(q[ch][rs] * kdec_s, axis=-1, keepdims=True)
                tiles[ch] = jnp.where(lane == s, a_s, tiles[ch])
        for ch in chains:
            a_rows[ch].append(jnp.where(trow + i * sub >= lane, tiles[ch], 0.0)[:, :c])
    for i in range(1, n_sub):
        rs = slice(i * sub, (i + 1) * sub)
        for n, ch in enumerate(chains):
            r = b[ch][i * sub - 1:i * sub, :]
            ks_ref[n, i - 1, :i * sub, :] = k[ch][:i * sub] * jnp.exp2(r - b[ch][:i * sub])
            qs = (q[ch][rs] * jnp.exp2(b[ch][rs] - r)).astype(BF16)
            a_rows[ch][i] = a_rows[ch][i] + _dot_nt(qs, ks_ref[n, i - 1].astype(BF16))
    o = {}
    for ch in chains:
        a = jnp.concatenate(a_rows[ch], axis=0).astype(BF16)
        o[ch] = o_inter[ch] + _dot(a, blk(v_ref, ch))

    gn = gn_ref[...]
    for ch in chains:
        hh, ci = ch
        on = o[ch] * lax.rsqrt(jnp.mean(o[ch] * o[ch], axis=-1, keepdims=True) + RMS_EPS) * gn
        o_ref[ci * c:(ci + 1) * c, hh * HG_D:(hh + 1) * HG_D] = (on * blk(g_ref, ch).astype(F32)).astype(BF16)


def _hgrn(hq, hf, hi, hg, gn, l, bsz, seq):
    n_tok = hq.shape[0]
    c = HG_CHUNK
    rows = c * HG_STEP_CHUNKS
    nc = seq // rows
    spec = pl.BlockSpec((rows, HG_WIDTH), lambda b, i: (b * nc + i, 0))
    return pl.pallas_call(
        _hgrn_kernel,
        grid=(bsz, nc),
        in_specs=[spec, spec, spec, spec, _layer_spec(gn.shape, l)],
        out_specs=spec,
        out_shape=jax.ShapeDtypeStruct((n_tok, HG_WIDTH), BF16),
        scratch_shapes=[pltpu.VMEM((HG_HEADS, HG_D, HG_D), F32),
                        pltpu.VMEM((HG_HEADS * HG_STEP_CHUNKS, c // HG_SUB - 1, c, HG_D), F32)],
        compiler_params=_params(("parallel", "arbitrary")),
        name="hgrn",
    )(hq, hf, hi, hg, gn)


def _memkv_kernel(mem_ref, g_ref, w_ref, kv_ref):
    kv_ref[...] = _dot(_rms(mem_ref[...], g_ref[...]).astype(BF16), w_ref[...]).astype(BF16)


def _memkv(mem2, g, w, l, n_mem):
    n_rows, d_model = mem2.shape
    return pl.pallas_call(
        _memkv_kernel,
        grid=(n_rows // n_mem,),
        in_specs=[pl.BlockSpec((n_mem, d_model), lambda i: (i, 0)), _layer_spec(g.shape, l), _layer_spec(w.shape, l)],
        out_specs=pl.BlockSpec((n_mem, 2 * X_WIDTH), lambda i: (i, 0)),
        out_shape=jax.ShapeDtypeStruct((n_rows, 2 * X_WIDTH), BF16),
        compiler_params=_params(("parallel",)),
        name="memkv",
    )(mem2, g, w)


def _mid_kernel(x_ref, ys_ref, om_ref, oh_ref, gate_ref, kv_ref, wmo_ref, who_ref, wout_ref, gx_ref, wq_ref,
                wxo_ref, o_ref, *, x_scale):
    d_model = x_ref.shape[1]
    om_t = om_ref[0].reshape(MLA_HEADS * MLA_V, x_ref.shape[0])
    y_mla = _dot_tn(om_t, wmo_ref[...])
    y_hg = _dot(oh_ref[...], who_ref[...])
    gate = lambda br: gate_ref[:, br * d_model:(br + 1) * d_model].astype(F32)
    merged = gate(0) * ys_ref[...] + gate(1) * y_mla + gate(2) * y_hg
    x = x_ref[...] + _dot(merged.astype(BF16), wout_ref[...])

    q = (_dot(_rms(x, gx_ref[...]).astype(BF16), wq_ref[...]) * x_scale).astype(BF16)
    sl = [slice(hh * X_HEAD_DIM, (hh + 1) * X_HEAD_DIM) for hh in range(X_HEADS)]
    outs = []
    s_next = _dot_nt(q[:, sl[0]], kv_ref[:, sl[0]])
    for hh in range(X_HEADS):
        s = s_next
        if hh + 1 < X_HEADS:
            s_next = _dot_nt(q[:, sl[hh + 1]], kv_ref[:, sl[hh + 1]])
        pr = jnp.exp2(s - jnp.max(s, axis=-1, keepdims=True))
        den = jnp.sum(pr, axis=-1, keepdims=True)
        vh = kv_ref[:, X_WIDTH + hh * X_HEAD_DIM:X_WIDTH + (hh + 1) * X_HEAD_DIM]
        outs.append((_dot(pr.astype(BF16), vh) / den).astype(BF16))
    o_ref[...] = x + _dot(jnp.concatenate(outs, axis=1), wxo_ref[...])


def _mid(x2, ys, om, oh, gates, kv, wmo, who, wout, gx, wq, wxo, l, seq, n_mem):
    n_tok, d_model = x2.shape
    t = MID_TILE
    nt = seq // t
    tok = lambda w_: pl.BlockSpec((t, w_), lambda i: (i, 0))
    return pl.pallas_call(
        functools.partial(_mid_kernel, x_scale=LOG2E / math.sqrt(X_HEAD_DIM)),
        grid=(n_tok // t,),
        in_specs=[tok(d_model), tok(d_model),
                  pl.BlockSpec((1, MLA_HEADS, MLA_V, t), lambda i: (i // nt, 0, 0, i % nt)),
                  tok(HG_WIDTH), tok(N_BRANCH * d_model),
                  pl.BlockSpec((n_mem, 2 * X_WIDTH), lambda i: (i // nt, 0)),
                  ] + [_layer_spec(w_.shape, l) for w_ in (wmo, who, wout, gx, wq, wxo)],
        out_specs=tok(d_model),
        out_shape=jax.ShapeDtypeStruct((n_tok, d_model), F32),
        compiler_params=_params(("parallel",)),
        name="mid",
    )(x2, ys, om, oh, gates, kv, wmo, who, wout, gx, wq, wxo)


def _ffn_kernel(x_ref, g_ref, wgu_ref, wd_ref, gf_ref, o_ref, *, final_norm):
    d_ff = wd_ref.shape[0]
    x = x_ref[...]
    h = _rms(x, g_ref[...]).astype(BF16)
    y = x
    for c0 in range(0, d_ff, FFN_CHUNK):
        gt = _dot(h, wgu_ref[:, c0:c0 + FFN_CHUNK])
        up = _dot(h, wgu_ref[:, d_ff + c0:d_ff + c0 + FFN_CHUNK])
        act = (gt * _sigmoid(gt) * up).astype(BF16)
        y = y + _dot(act, wd_ref[c0:c0 + FFN_CHUNK, :])
    if final_norm:
        y = _rms(y, gf_ref[...])
    o_ref[...] = y


def _ffn(x2, g, wgu, wd, gf, l, final_norm):
    n_tok, d_model = x2.shape
    t = FFN_TILE
    tok = pl.BlockSpec((t, d_model), lambda i: (i, 0))
    return pl.pallas_call(
        functools.partial(_ffn_kernel, final_norm=final_norm),
        grid=(n_tok // t,),
        in_specs=[tok, _layer_spec(g.shape, l), _layer_spec(wgu.shape, l), _layer_spec(wd.shape, l),
                  _const_spec(gf.shape)],
        out_specs=tok,
        out_shape=jax.ShapeDtypeStruct((n_tok, d_model), F32),
        compiler_params=_params(("parallel",)),
        name="ffn",
    )(x2, g, wgu, wd, gf)


def _in_proj_weights(w_in):
    kpe = jnp.pad(w_in[:, :, C_KPE:C_KPE + MLA_ROPE], ((0, 0), (0, 0), (MLA_NOPE, HEAD_PAD - MLA_NOPE - MLA_ROPE)))
    cols = [w_in[:, :, :C_KPE], kpe, w_in[:, :, C_KPE + MLA_ROPE:]]
    return jnp.concatenate(cols, axis=2).astype(BF16)


def _mla_weights(w_uq, w_ukv):
    depth, rank, _ = w_uq.shape
    dq = MLA_NOPE + MLA_ROPE
    wq = w_uq.reshape(depth, rank, MLA_HEADS, dq)
    wuq = jnp.pad(wq, ((0, 0), (0, 0), (0, 0), (0, HEAD_PAD - dq))).reshape(depth, rank, MLA_HEADS * HEAD_PAD)
    kvr = w_ukv.shape[1]
    wkv = w_ukv.reshape(depth, kvr, MLA_HEADS, MLA_NOPE + MLA_V)
    wuk = wkv[..., :MLA_NOPE].reshape(depth, kvr, -1)
    wuvt = wkv[..., MLA_NOPE:].reshape(depth, kvr, -1).transpose(0, 2, 1)
    return wuq.astype(BF16), wuk.astype(BF16), wuvt.astype(BF16)


def _s5_weights(lam_re, lam_im, b_re, b_im, c_re, c_im, log_step):
    depth = lam_re.shape[0]
    step = jnp.exp(log_step)[..., None]
    mag = jnp.exp(lam_re * step)
    lbr = mag * jnp.cos(lam_im * step)
    lbi = mag * jnp.sin(lam_im * step)
    den = lam_re * lam_re + lam_im * lam_im
    cr = ((lbr - 1.0) * lam_re + lbi * lam_im) / den
    ci = (lbi * lam_re - (lbr - 1.0) * lam_im) / den
    bbr = cr[..., None] * b_re - ci[..., None] * b_im
    bbi = cr[..., None] * b_im + ci[..., None] * b_re
    gq = SSM_GROUPS // SSM_QUARTERS
    eye = jnp.eye(gq, dtype=F32)

    def blockdiag(m):
        m = m.reshape(depth, SSM_QUARTERS, gq, m.shape[2], m.shape[3]).transpose(0, 1, 2, 4, 3)
        out = m[:, :, :, :, None, :] * eye[None, None, :, None, :, None]
        return out.reshape(depth, SSM_QUARTERS, gq * m.shape[3], gq * m.shape[4])

    wb = jnp.concatenate([blockdiag(bbr), blockdiag(bbi)], axis=3).astype(BF16)
    wc = jnp.concatenate([blockdiag(c_re), -blockdiag(c_im)], axis=3).astype(BF16)
    lr = lbr.reshape(depth, 1, SSM_CH)
    li = lbi.reshape(depth, 1, SSM_CH)
    half = SUBLANES // 2
    a = jnp.broadcast_to(lr, (depth, SUBLANES, SSM_CH))
    bsw = jnp.concatenate([jnp.broadcast_to(-li, (depth, half, SSM_CH)), jnp.broadcast_to(li, (depth, half, SSM_CH))],
                          axis=1)
    return wb, a, bsw, wc


def kernel(x, mem, positions, norm_mix, w_in, ssm_lam_re, ssm_lam_im, ssm_b_re, ssm_b_im, ssm_c_re, ssm_c_im, ssm_d, ssm_log_step, ssm_w_glu, mla_q_norm, mla_kv_norm, mla_w_uq, mla_w_ukv, mla_w_o, hg_lb, hg_g_norm, hg_w_o, w_out, norm_cross, norm_mem, x_w_q, x_w_kv, x_w_o, norm_ffn, ffn_w_gate_up, ffn_w_down, norm_final):
    bsz, seq, d_model = x.shape
    depth = w_in.shape[0]
    n_mem = mem.shape[1]
    assert 2 * bsz == SUBLANES, "the s5 scan packs (re, im) x batch on the sublanes of one tile"
    n_tok = bsz * seq

    half = MLA_ROPE // 2
    invf = (ROPE_THETA ** (-jnp.arange(half, dtype=F32) / half)).reshape(half, 1)
    lb_p = jax.nn.softmax(hg_lb.astype(F32), axis=0)
    lower_bounds = jnp.cumsum(lb_p, axis=0) - lb_p[0:1]

    rows = lambda a: a.reshape(depth, 1, -1)
    bf = lambda a: a.astype(BF16)
    w_inp = _in_proj_weights(w_in)
    wuq, wuk, wuvt = _mla_weights(mla_w_uq, mla_w_ukv)
    wb, a, bsw, wc = _s5_weights(ssm_lam_re, ssm_lam_im, ssm_b_re, ssm_b_im, ssm_c_re, ssm_c_im, ssm_log_step)
    wglu, wmo, who, wout = bf(ssm_w_glu), bf(mla_w_o), bf(hg_w_o), bf(w_out)
    wxq, wxkv, wxo, wgu, wdn = bf(x_w_q), bf(x_w_kv), bf(x_w_o), bf(ffn_w_gate_up), bf(ffn_w_down)
    g_mix, g_q, g_kv, g_hg = rows(norm_mix), rows(mla_q_norm), rows(mla_kv_norm), rows(hg_g_norm)
    g_cross, g_mem, g_ffn = rows(norm_cross), rows(norm_mem), rows(norm_ffn)
    lbs, ssm_dr = rows(lower_bounds), rows(ssm_d)

    x2 = x.reshape(n_tok, d_model)
    mem2 = mem.reshape(bsz * n_mem, d_model)
    pos2 = positions.reshape(n_tok // IN_TILE, 1, IN_TILE)

    for l in range(depth):
        u, q, k, v, hq, hf, hi, hg, gates = _in_proj(x2, pos2, g_mix, w_inp, invf, g_q, wuq, g_kv, wuk, wuvt,
                                                     lbs, l, bsz, seq)
        ys = _s5(u.reshape(bsz, seq, SSM_WIDTH), wb, a, bsw, wc, ssm_dr, wglu, l, d_model).reshape(n_tok, d_model)
        om = _mla(q, k, v)
        oh = _hgrn(hq, hf, hi, hg, g_hg, l, bsz, seq)
        kv = _memkv(mem2, g_mem, wxkv, l, n_mem)
        x2 = _mid(x2, ys, om, oh, gates, kv, wmo, who, wout, g_cross, wxq, wxo, l, seq, n_mem)
        x2 = _ffn(x2, g_ffn, wgu, wdn, norm_final.reshape(1, -1), l, final_norm=(l == depth - 1))
    return x2.reshape(bsz, seq, d_model)
```

```python
import functools
import math

import jax
import jax.numpy as jnp
from jax import lax
from jax.experimental import pallas as pl
from jax.experimental.pallas import tpu as pltpu

F32 = jnp.float32
BF16 = jnp.bfloat16

SUBLANES = 8
LANES = 128

RMS_EPS = 1e-6
LOG2E = 1.4426950408889634

SSM_GROUPS = 32
SSM_GROUP_CH = 16
SSM_STATE = 64
SSM_WIDTH = SSM_GROUPS * SSM_GROUP_CH
SSM_CH = SSM_GROUPS * SSM_STATE
SSM_QUARTERS = 4

MLA_HEADS = 8
MLA_Q_RANK = 512
MLA_KV_RANK = 256
MLA_NOPE = 64
MLA_ROPE = 32
MLA_V = 64
ROPE_THETA = 10000.0
HEAD_PAD = LANES
MLA_VT_ROWS = 80

HG_HEADS = 4
HG_D = 128
HG_WIDTH = HG_HEADS * HG_D
HG_CHUNK = 64
HG_SUB = SUBLANES
HG_STEP_CHUNKS = 8

X_HEADS = 4
X_HEAD_DIM = 128
X_WIDTH = X_HEADS * X_HEAD_DIM

N_BRANCH = 3

VMEM_LIMIT = 56 * 1024 * 1024

IN_TILE = 512
S5_TIME = 128
MLA_BLOCK = 512
MID_TILE = 512
FFN_TILE = 512
FFN_CHUNK = 2816


def _dot(a, b):
    return jnp.dot(a, b, preferred_element_type=F32)


def _dot_nt(a, b):
    return lax.dot_general(a, b, (((1,), (1,)), ((), ())), preferred_element_type=F32)


def _dot_tn(a, b):
    return lax.dot_general(a, b, (((0,), (0,)), ((), ())), preferred_element_type=F32)


def _rms(x, g):
    return x * lax.rsqrt(jnp.mean(x * x, axis=-1, keepdims=True) + RMS_EPS) * g


def _sigmoid(x):
    return 1.0 / (1.0 + jnp.exp(-x))


def _const_spec(shape):
    nd = len(shape)
    return pl.BlockSpec(shape, lambda *_: (0,) * nd, pipeline_mode=pl.Buffered(1))


def _layer_spec(shape, l):
    nd = len(shape)
    return pl.BlockSpec((None,) + tuple(shape[1:]), lambda *_: (l,) + (0,) * (nd - 1),
                        pipeline_mode=pl.Buffered(1))


def _params(sem):
    return pltpu.CompilerParams(dimension_semantics=sem, vmem_limit_bytes=VMEM_LIMIT)


C_SSM = 0
C_QLAT = C_SSM + SSM_WIDTH
C_KVLAT = C_QLAT + MLA_Q_RANK
C_KPE = C_KVLAT + MLA_KV_RANK
C_HQ = C_KPE + HEAD_PAD
C_HF = C_HQ + HG_WIDTH
C_HI = C_HF + HG_WIDTH
C_HG = C_HI + HG_WIDTH
C_GATE = C_HG + HG_WIDTH


def _in_proj_kernel(x_ref, pos_ref, g_ref, w_ref, invf_ref, qn_ref, wuq_ref, kvn_ref, wuk_ref, wuvt_ref, lb_ref,
                    u_ref, q_ref, k_ref, v_ref, hq_ref, hf_ref, hi_ref, hg_ref, gate_ref, *, q_scale):
    d_model = x_ref.shape[1]
    h = _rms(x_ref[...], g_ref[...]).astype(BF16)

    def seg(a, b):
        return _dot(h, w_ref[:, a:b])

    qlat = seg(C_QLAT, C_KVLAT)
    kvlat = seg(C_KVLAT, C_KPE)
    kp = seg(C_KPE, C_HQ)
    qn = _rms(qlat, qn_ref[...]).astype(BF16)
    kvn = _rms(kvlat, kvn_ref[...]).astype(BF16)

    t = x_ref.shape[0]
    ang = invf_ref[...] * pos_ref[0].astype(F32)
    cos_c = jnp.cos(ang)
    sin_c = jnp.sin(ang)
    fill = lambda v, n: jnp.full((n, t), v, F32)
    tail = HEAD_PAD - MLA_NOPE - MLA_ROPE
    cos = jnp.concatenate([fill(1.0, MLA_NOPE), cos_c, cos_c, fill(1.0, tail)], axis=0).T
    sin = jnp.concatenate([fill(0.0, MLA_NOPE), -sin_c, sin_c, fill(0.0, tail)], axis=0).T
    lane = lax.broadcasted_iota(jnp.int32, (1, HEAD_PAD), 1)
    first_half = lane < MLA_NOPE + MLA_ROPE // 2

    def rope(v):
        swapped = jnp.where(first_half, pltpu.roll(v, HEAD_PAD - MLA_ROPE // 2, 1),
                            pltpu.roll(v, MLA_ROPE // 2, 1))
        return v * cos + swapped * sin

    u_ref[...] = seg(C_SSM, C_QLAT)
    hq = seg(C_HQ, C_HF)
    hq_ref[...] = (hq * _sigmoid(hq)).astype(BF16)
    lb = lb_ref[...]
    hf_ref[...] = lb + (1.0 - lb) * _sigmoid(seg(C_HF, C_HI))
    hi_ref[...] = seg(C_HI, C_HG).astype(BF16)
    hg = seg(C_HG, C_GATE)
    hg_ref[...] = (hg * _sigmoid(hg)).astype(BF16)

    qa = _dot(qn, wuq_ref[...])
    for hh in range(MLA_HEADS):
        q_ref[0, hh] = (rope(qa[:, hh * HEAD_PAD:(hh + 1) * HEAD_PAD]) * q_scale).astype(BF16)

    kn = _dot(kvn, wuk_ref[...])
    vt = _dot_nt(wuvt_ref[...], kvn)
    kpe = rope(kp)
    nope_lanes = lane < MLA_NOPE
    ones_rows = (lax.broadcasted_iota(jnp.int32, (MLA_VT_ROWS - MLA_V, t), 0) == 0).astype(BF16)
    for hh in range(MLA_HEADS):
        piece = kn[:, (hh // 2) * HEAD_PAD:(hh // 2 + 1) * HEAD_PAD]
        if hh % 2:
            piece = pltpu.roll(piece, MLA_NOPE, 1)
        k_ref[0, hh] = jnp.where(nope_lanes, piece, kpe).astype(BF16)
        v_ref[0, hh, :MLA_V, :] = vt[hh * MLA_V:(hh + 1) * MLA_V].astype(BF16)
        v_ref[0, hh, MLA_V:, :] = ones_rows

    for br in range(N_BRANCH):
        gate_ref[:, br * d_model:(br + 1) * d_model] = _sigmoid(
            seg(C_GATE + br * d_model, C_GATE + (br + 1) * d_model)).astype(BF16)


def _in_proj(x2, pos2, g, w, invf, qn, wuq, kvn, wuk, wuvt, lb, l, bsz, seq):
    n_tok, d_model = x2.shape
    t = IN_TILE
    nt = seq // t
    tok = lambda w_: pl.BlockSpec((t, w_), lambda i: (i, 0))
    lay = lambda a: _layer_spec(a.shape, l)
    head = pl.BlockSpec((1, MLA_HEADS, t, HEAD_PAD), lambda i: (i // nt, 0, i % nt, 0))
    head_shape = jax.ShapeDtypeStruct((bsz, MLA_HEADS, seq, HEAD_PAD), BF16)
    head_t = pl.BlockSpec((1, MLA_HEADS, MLA_VT_ROWS, t), lambda i: (i // nt, 0, 0, i % nt))
    head_t_shape = jax.ShapeDtypeStruct((bsz, MLA_HEADS, MLA_VT_ROWS, seq), BF16)
    q_scale = LOG2E / math.sqrt(MLA_NOPE + MLA_ROPE)
    return pl.pallas_call(
        functools.partial(_in_proj_kernel, q_scale=q_scale),
        grid=(n_tok // t,),
        in_specs=[tok(d_model), pl.BlockSpec((1, 1, t), lambda i: (i, 0, 0)), lay(g), lay(w),
                  _const_spec(invf.shape), lay(qn), lay(wuq), lay(kvn), lay(wuk), lay(wuvt), lay(lb)],
        out_specs=[tok(SSM_WIDTH), head, head, head_t, tok(HG_WIDTH), tok(HG_WIDTH), tok(HG_WIDTH), tok(HG_WIDTH),
                   tok(N_BRANCH * d_model)],
        out_shape=[jax.ShapeDtypeStruct((n_tok, SSM_WIDTH), F32), head_shape, head_shape, head_t_shape,
                   jax.ShapeDtypeStruct((n_tok, HG_WIDTH), BF16), jax.ShapeDtypeStruct((n_tok, HG_WIDTH), F32),
                   jax.ShapeDtypeStruct((n_tok, HG_WIDTH), BF16), jax.ShapeDtypeStruct((n_tok, HG_WIDTH), BF16),
                   jax.ShapeDtypeStruct((n_tok, N_BRANCH * d_model), BF16)],
        compiler_params=_params(("parallel",)),
        name="in_proj",
    )(x2, pos2, g, w, invf, qn, wuq, kvn, wuk, wuvt, lb)


def _s5_kernel(u_ref, wb_ref, a_ref, bsw_ref, wc_ref, d_ref, wglu_ref, y_ref, se_ref, so_ref, st_ref, utb_ref,
               ytb_ref):
    bsz, tc, _ = u_ref.shape
    rows = bsz * tc
    d_model = y_ref.shape[2]
    qw = SSM_CH // SSM_QUARTERS
    uw = SSM_WIDTH // SSM_QUARTERS
    half = SUBLANES // 2

    @pl.when(pl.program_id(0) == 0)
    def _():
        st_ref[...] = jnp.zeros_like(st_ref)

    for q in range(SSM_QUARTERS):
        for b in range(bsz):
            utb_ref[q, pl.ds(b, tc, stride=bsz), :] = u_ref[b, :, q * uw:(q + 1) * uw]

    low = (lax.broadcasted_iota(jnp.int32, (rows, qw), 0) % SUBLANES) < half
    for q in range(SSM_QUARTERS):
        o = _dot(utb_ref[q].astype(BF16), wb_ref[q])
        re = o[:, :qw]
        im = o[:, qw:]
        se_ref[:, q * qw:(q + 1) * qw] = jnp.where(low, re, pltpu.roll(im, half, 0))
        so_ref[:, q * qw:(q + 1) * qw] = jnp.where(low, pltpu.roll(re, rows - half, 0), im)

    n_blk = 2
    cw = SSM_CH // n_blk
    for cb in range(n_blk):
        cs = slice(cb * cw, (cb + 1) * cw)
        a = a_ref[:, cs]
        bs = bsw_ref[:, cs]

        def body(v, s, cs=cs, a=a, bs=bs):
            r = pl.multiple_of(v * SUBLANES, SUBLANES)
            s = a * s + bs * pltpu.roll(s, half, 0) + se_ref[pl.ds(r, SUBLANES), cs]
            se_ref[pl.ds(r, SUBLANES), cs] = s
            s = a * s + bs * pltpu.roll(s, half, 0) + so_ref[pl.ds(r, SUBLANES), cs]
            so_ref[pl.ds(r, SUBLANES), cs] = s
            return s

        st_ref[:, cs] = lax.fori_loop(0, rows // SUBLANES, body, st_ref[:, cs], unroll=4)

    yw = uw
    low_y = low[:, :yw]
    pieces = []
    for q in range(SSM_QUARTERS):
        oe = _dot(se_ref[:, q * qw:(q + 1) * qw].astype(BF16), wc_ref[q])
        oo = _dot(so_ref[:, q * qw:(q + 1) * qw].astype(BF16), wc_ref[q])
        ye = oe[:, :yw] + pltpu.roll(oe[:, yw:], rows - half, 0)
        yo = pltpu.roll(oo[:, :yw], half, 0) + oo[:, yw:]
        pieces.append(jnp.where(low_y, ye, yo) + d_ref[:, q * uw:(q + 1) * uw] * utb_ref[q])
    y = jax.nn.gelu(jnp.concatenate(pieces, axis=1)).astype(BF16)
    z = _dot(y, wglu_ref[...])
    res = z[:, :d_model] * _sigmoid(z[:, d_model:])
    n_slab = d_model // uw
    for j in range(n_slab):
        ytb_ref[j] = res[:, j * uw:(j + 1) * uw]
    for j in range(n_slab):
        for b in range(bsz):
            y_ref[b, :, j * uw:(j + 1) * uw] = ytb_ref[j, pl.ds(b, tc, stride=bsz), :]


def _s5(u3, wb, a, bsw, wc, d, wglu, l, d_model):
    bsz, seq, _ = u3.shape
    tc = S5_TIME
    rows = tc * bsz
    lanes = SSM_WIDTH // SSM_QUARTERS
    return pl.pallas_call(
        _s5_kernel,
        grid=(seq // tc,),
        in_specs=[pl.BlockSpec((bsz, tc, SSM_WIDTH), lambda i: (0, i, 0))]
        + [_layer_spec(w_.shape, l) for w_ in (wb, a, bsw, wc, d, wglu)],
        out_specs=pl.BlockSpec((bsz, tc, d_model), lambda i: (0, i, 0)),
        out_shape=jax.ShapeDtypeStruct((bsz, seq, d_model), F32),
        scratch_shapes=[pltpu.VMEM((rows, SSM_CH), F32), pltpu.VMEM((rows, SSM_CH), F32),
                        pltpu.VMEM((SUBLANES, SSM_CH), F32), pltpu.VMEM((SSM_QUARTERS, rows, lanes), F32),
                        pltpu.VMEM((d_model // lanes, rows, lanes), F32)],
        compiler_params=_params(("arbitrary",)),
        name="s5",
    )(u3, wb, a, bsw, wc, d, wglu)


def _mla_kernel(qi_ref, kj_ref, q_ref, k_ref, vt_ref, o_ref, m_ref, acc_ref):
    p = pl.program_id(1)
    qi = qi_ref[p]
    kj = kj_ref[p]
    blk = q_ref.shape[2]

    @pl.when(kj == 0)
    def _():
        m_ref[...] = jnp.full_like(m_ref, -1e30)
        acc_ref[...] = jnp.zeros_like(acc_ref)

    def step(masked):
        if masked:
            key = lax.broadcasted_iota(jnp.int32, (blk, blk), 0)
            qry = lax.broadcasted_iota(jnp.int32, (blk, blk), 1)
            causal = key <= qry
        sc, mx = {}, {}
        for t in range(MLA_HEADS + 2):
            if t < MLA_HEADS:
                s = _dot_nt(k_ref[0, t], q_ref[0, t])
                sc[t] = jnp.where(causal, s, -1e30) if masked else s
            if 1 <= t <= MLA_HEADS:
                hh = t - 1
                m_prev = m_ref[hh]
                m_new = jnp.maximum(m_prev, jnp.max(sc[hh], axis=0, keepdims=True))
                mx[hh] = (m_new, jnp.exp2(m_prev - m_new))
                m_ref[hh] = m_new
            if t >= 2:
                hh = t - 2
                m_new, alpha = mx.pop(hh)
                pr = jnp.exp2(sc.pop(hh) - m_new).astype(BF16)
                acc_ref[hh] = alpha * acc_ref[hh] + _dot(vt_ref[0, hh], pr)

    @pl.when(kj < qi)
    def _():
        step(False)

    @pl.when(kj == qi)
    def _():
        step(True)
        for hh in range(MLA_HEADS):
            acc = acc_ref[hh]
            o_ref[0, hh] = (acc[:MLA_V, :] / acc[MLA_V:MLA_V + 1, :]).astype(BF16)


def _mla(q, k, vt):
    bsz, nh, seq, hd = q.shape
    blk = MLA_BLOCK
    nb = seq // blk
    pairs = [(i, j) for i in range(nb) for j in range(i + 1)]
    qi = jnp.asarray([p[0] for p in pairs], jnp.int32)
    kj = jnp.asarray([p[1] for p in pairs], jnp.int32)
    qspec = pl.BlockSpec((1, nh, blk, hd), lambda b, p, qi, kj: (b, 0, qi[p], 0))
    kspec = pl.BlockSpec((1, nh, blk, hd), lambda b, p, qi, kj: (b, 0, kj[p], 0))
    vspec = pl.BlockSpec((1, nh, MLA_VT_ROWS, blk), lambda b, p, qi, kj: (b, 0, 0, kj[p]))
    ospec = pl.BlockSpec((1, nh, MLA_V, blk), lambda b, p, qi, kj: (b, 0, 0, qi[p]))
    return pl.pallas_call(
        _mla_kernel,
        grid_spec=pltpu.PrefetchScalarGridSpec(
            num_scalar_prefetch=2,
            grid=(bsz, len(pairs)),
            in_specs=[qspec, kspec, vspec],
            out_specs=ospec,
            scratch_shapes=[pltpu.VMEM((nh, 1, blk), F32), pltpu.VMEM((nh, MLA_VT_ROWS, blk), F32)],
        ),
        out_shape=jax.ShapeDtypeStruct((bsz, nh, MLA_V, seq), BF16),
        compiler_params=_params(("parallel", "arbitrary")),
        name="mla",
    )(qi, kj, q, k, vt)


def _hgrn_kernel(q_ref, f_ref, v_ref, g_ref, gn_ref, o_ref, st_ref, ks_ref):
    c = HG_CHUNK
    sub = HG_SUB
    n_sub = c // sub

    @pl.when(pl.program_id(1) == 0)
    def _():
        st_ref[...] = jnp.zeros_like(st_ref)
        ks_ref[...] = jnp.zeros_like(ks_ref)

    chains = [(hh, ci) for hh in range(HG_HEADS) for ci in range(HG_STEP_CHUNKS)]

    def blk(ref, ch):
        hh, ci = ch
        return ref[ci * c:(ci + 1) * c, hh * HG_D:(hh + 1) * HG_D]

    row = lax.broadcasted_iota(jnp.int32, (c, c), 0)
    col = lax.broadcasted_iota(jnp.int32, (c, c), 1)
    tri = (col <= row).astype(BF16)
    lane = lax.broadcasted_iota(jnp.int32, (sub, HG_D), 1)
    trow = lax.broadcasted_iota(jnp.int32, (sub, HG_D), 0)

    q, k, b, bk = {}, {}, {}, {}
    for ch in chains:
        f = blk(f_ref, ch)
        k[ch] = 1.0 - f
        lf = jnp.log(f) * LOG2E
        l1 = lf.astype(BF16)
        r1 = lf - l1.astype(F32)
        l2 = r1.astype(BF16)
        l3 = (r1 - l2.astype(F32)).astype(BF16)
        b[ch] = _dot(tri, l1) + _dot(tri, l2) + _dot(tri, l3)
    for ch in chains:
        q[ch] = blk(q_ref, ch).astype(F32)
        bk[ch] = b[ch] - jnp.log(k[ch]) * LOG2E

    qb, kdec, ebl, o_inter = {}, {}, {}, {}
    for ch in chains:
        bl = b[ch][c - 1:c, :]
        qb[ch] = (q[ch] * jnp.exp2(b[ch])).astype(BF16)
        kdec[ch] = (k[ch] * jnp.exp2(bl - b[ch])).astype(BF16)
        ebl[ch] = jnp.exp2(bl)
    for hh in range(HG_HEADS):
        st = st_ref[hh]
        for ci in range(HG_STEP_CHUNKS):
            ch = (hh, ci)
            o_inter[ch] = _dot_nt(qb[ch], st.astype(BF16))
            st = st * ebl[ch] + _dot_tn(blk(v_ref, ch), kdec[ch])
        st_ref[hh] = st

    a_rows = {ch: [] for ch in chains}
    for i in range(n_sub):
        rs = slice(i * sub, (i + 1) * sub)
        for ch in chains:
            b_blk = b[ch][rs]
            q_blk = q[ch][rs]
            tile = jnp.zeros((sub, HG_D), F32)
            for ss in range(sub):
                s = i * sub + ss
                kdec_s = jnp.exp2(b_blk - bk[ch][s:s + 1, :])
                a_s = jnp.sum(q_blk * kdec_s, axis=-1, keepdims=True)
                tile = jnp.where(lane == s, a_s, tile)
            a_rows[ch].append(jnp.where(trow + i * sub >= lane, tile, 0.0)[:, :c])
    for i in range(1, n_sub):
        rs = slice(i * sub, (i + 1) * sub)
        for n, ch in enumerate(chains):
            r = b[ch][i * sub - 1:i * sub, :]
            ks_ref[n, i - 1, :i * sub, :] = k[ch][:i * sub] * jnp.exp2(r - b[ch][:i * sub])
            qs = (q[ch][rs] * jnp.exp2(b[ch][rs] - r)).astype(BF16)
            a_rows[ch][i] = a_rows[ch][i] + _dot_nt(qs, ks_ref[n, i - 1].astype(BF16))
    o = {}
    for ch in chains:
        a = jnp.concatenate(a_rows[ch], axis=0).astype(BF16)
        o[ch] = o_inter[ch] + _dot(a, blk(v_ref, ch))

    gn = gn_ref[...]
    for ch in chains:
        hh, ci = ch
        on = o[ch] * lax.rsqrt(jnp.mean(o[ch] * o[ch], axis=-1, keepdims=True) + RMS_EPS) * gn
        o_ref[ci * c:(ci + 1) * c, hh * HG_D:(hh + 1) * HG_D] = (on * blk(g_ref, ch).astype(F32)).astype(BF16)


def _hgrn(hq, hf, hi, hg, gn, l, bsz, seq):
    n_tok = hq.shape[0]
    c = HG_CHUNK
    rows = c * HG_STEP_CHUNKS
    nc = seq // rows
    spec = pl.BlockSpec((rows, HG_WIDTH), lambda b, i: (b * nc + i, 0))
    return pl.pallas_call(
        _hgrn_kernel,
        grid=(bsz, nc),
        in_specs=[spec, spec, spec, spec, _layer_spec(gn.shape, l)],
        out_specs=spec,
        out_shape=jax.ShapeDtypeStruct((n_tok, HG_WIDTH), BF16),
        scratch_shapes=[pltpu.VMEM((HG_HEADS, HG_D, HG_D), F32),
                        pltpu.VMEM((HG_HEADS * HG_STEP_CHUNKS, c // HG_SUB - 1, c, HG_D), F32)],
        compiler_params=_params(("parallel", "arbitrary")),
        name="hgrn",
    )(hq, hf, hi, hg, gn)


def _memkv_kernel(mem_ref, g_ref, w_ref, kv_ref):
    kv_ref[...] = _dot(_rms(mem_ref[...], g_ref[...]).astype(BF16), w_ref[...]).astype(BF16)


def _memkv(mem2, g, w, l, n_mem):
    n_rows, d_model = mem2.shape
    return pl.pallas_call(
        _memkv_kernel,
        grid=(n_rows // n_mem,),
        in_specs=[pl.BlockSpec((n_mem, d_model), lambda i: (i, 0)), _layer_spec(g.shape, l), _layer_spec(w.shape, l)],
        out_specs=pl.BlockSpec((n_mem, 2 * X_WIDTH), lambda i: (i, 0)),
        out_shape=jax.ShapeDtypeStruct((n_rows, 2 * X_WIDTH), BF16),
        compiler_params=_params(("parallel",)),
        name="memkv",
    )(mem2, g, w)


def _mid_kernel(x_ref, ys_ref, om_ref, oh_ref, gate_ref, kv_ref, wmo_ref, who_ref, wout_ref, gx_ref, wq_ref,
                wxo_ref, o_ref, *, x_scale):
    d_model = x_ref.shape[1]
    om_t = om_ref[0].reshape(MLA_HEADS * MLA_V, x_ref.shape[0])
    y_mla = _dot_tn(om_t, wmo_ref[...])
    y_hg = _dot(oh_ref[...], who_ref[...])
    gate = lambda br: gate_ref[:, br * d_model:(br + 1) * d_model].astype(F32)
    merged = gate(0) * ys_ref[...] + gate(1) * y_mla + gate(2) * y_hg
    x = x_ref[...] + _dot(merged.astype(BF16), wout_ref[...])

    q = (_dot(_rms(x, gx_ref[...]).astype(BF16), wq_ref[...]) * x_scale).astype(BF16)
    sl = [slice(hh * X_HEAD_DIM, (hh + 1) * X_HEAD_DIM) for hh in range(X_HEADS)]
    outs = []
    s_next = _dot_nt(q[:, sl[0]], kv_ref[:, sl[0]])
    for hh in range(X_HEADS):
        s = s_next
        if hh + 1 < X_HEADS:
            s_next = _dot_nt(q[:, sl[hh + 1]], kv_ref[:, sl[hh + 1]])
        pr = jnp.exp2(s - jnp.max(s, axis=-1, keepdims=True))
        den = jnp.sum(pr, axis=-1, keepdims=True)
        vh = kv_ref[:, X_WIDTH + hh * X_HEAD_DIM:X_WIDTH + (hh + 1) * X_HEAD_DIM]
        outs.append((_dot(pr.astype(BF16), vh) / den).astype(BF16))
    o_ref[...] = x + _dot(jnp.concatenate(outs, axis=1), wxo_ref[...])


def _mid(x2, ys, om, oh, gates, kv, wmo, who, wout, gx, wq, wxo, l, seq, n_mem):
    n_tok, d_model = x2.shape
    t = MID_TILE
    nt = seq // t
    tok = lambda w_: pl.BlockSpec((t, w_), lambda i: (i, 0))
    return pl.pallas_call(
        functools.partial(_mid_kernel, x_scale=LOG2E / math.sqrt(X_HEAD_DIM)),
        grid=(n_tok // t,),
        in_specs=[tok(d_model), tok(d_model),
                  pl.BlockSpec((1, MLA_HEADS, MLA_V, t), lambda i: (i // nt, 0, 0, i % nt)),
                  tok(HG_WIDTH), tok(N_BRANCH * d_model),
                  pl.BlockSpec((n_mem, 2 * X_WIDTH), lambda i: (i // nt, 0)),
                  ] + [_layer_spec(w_.shape, l) for w_ in (wmo, who, wout, gx, wq, wxo)],
        out_specs=tok(d_model),
        out_shape=jax.ShapeDtypeStruct((n_tok, d_model), F32),
        compiler_params=_params(("parallel",)),
        name="mid",
    )(x2, ys, om, oh, gates, kv, wmo, who, wout, gx, wq, wxo)


def _ffn_kernel(x_ref, g_ref, wgu_ref, wd_ref, gf_ref, o_ref, *, final_norm):
    d_ff = wd_ref.shape[0]
    x = x_ref[...]
    h = _rms(x, g_ref[...]).astype(BF16)
    y = x
    for c0 in range(0, d_ff, FFN_CHUNK):
        gt = _dot(h, wgu_ref[:, c0:c0 + FFN_CHUNK])
        up = _dot(h, wgu_ref[:, d_ff + c0:d_ff + c0 + FFN_CHUNK])
        act = (gt * _sigmoid(gt) * up).astype(BF16)
        y = y + _dot(act, wd_ref[c0:c0 + FFN_CHUNK, :])
    if final_norm:
        y = _rms(y, gf_ref[...])
    o_ref[...] = y


def _ffn(x2, g, wgu, wd, gf, l, final_norm):
    n_tok, d_model = x2.shape
    t = FFN_TILE
    tok = pl.BlockSpec((t, d_model), lambda i: (i, 0))
    return pl.pallas_call(
        functools.partial(_ffn_kernel, final_norm=final_norm),
        grid=(n_tok // t,),
        in_specs=[tok, _layer_spec(g.shape, l), _layer_spec(wgu.shape, l), _layer_spec(wd.shape, l),
                  _const_spec(gf.shape)],
        out_specs=tok,
        out_shape=jax.ShapeDtypeStruct((n_tok, d_model), F32),
        compiler_params=_params(("parallel",)),
        name="ffn",
    )(x2, g, wgu, wd, gf)


def _in_proj_weights(w_in):
    kpe = jnp.pad(w_in[:, :, C_KPE:C_KPE + MLA_ROPE], ((0, 0), (0, 0), (MLA_NOPE, HEAD_PAD - MLA_NOPE - MLA_ROPE)))
    cols = [w_in[:, :, :C_KPE], kpe, w_in[:, :, C_KPE + MLA_ROPE:]]
    return jnp.concatenate(cols, axis=2).astype(BF16)


def _mla_weights(w_uq, w_ukv):
    depth, rank, _ = w_uq.shape
    dq = MLA_NOPE + MLA_ROPE
    wq = w_uq.reshape(depth, rank, MLA_HEADS, dq)
    wuq = jnp.pad(wq, ((0, 0), (0, 0), (0, 0), (0, HEAD_PAD - dq))).reshape(depth, rank, MLA_HEADS * HEAD_PAD)
    kvr = w_ukv.shape[1]
    wkv = w_ukv.reshape(depth, kvr, MLA_HEADS, MLA_NOPE + MLA_V)
    wuk = wkv[..., :MLA_NOPE].reshape(depth, kvr, -1)
    wuvt = wkv[..., MLA_NOPE:].reshape(depth, kvr, -1).transpose(0, 2, 1)
    return wuq.astype(BF16), wuk.astype(BF16), wuvt.astype(BF16)


def _s5_weights(lam_re, lam_im, b_re, b_im, c_re, c_im, log_step):
    depth = lam_re.shape[0]
    step = jnp.exp(log_step)[..., None]
    mag = jnp.exp(lam_re * step)
    lbr = mag * jnp.cos(lam_im * step)
    lbi = mag * jnp.sin(lam_im * step)
    den = lam_re * lam_re + lam_im * lam_im
    cr = ((lbr - 1.0) * lam_re + lbi * lam_im) / den
    ci = (lbi * lam_re - (lbr - 1.0) * lam_im) / den
    bbr = cr[..., None] * b_re - ci[..., None] * b_im
    bbi = cr[..., None] * b_im + ci[..., None] * b_re
    gq = SSM_GROUPS // SSM_QUARTERS
    eye = jnp.eye(gq, dtype=F32)

    def blockdiag(m):
        m = m.reshape(depth, SSM_QUARTERS, gq, m.shape[2], m.shape[3]).transpose(0, 1, 2, 4, 3)
        out = m[:, :, :, :, None, :] * eye[None, None, :, None, :, None]
        return out.reshape(depth, SSM_QUARTERS, gq * m.shape[3], gq * m.shape[4])

    wb = jnp.concatenate([blockdiag(bbr), blockdiag(bbi)], axis=3).astype(BF16)
    wc = jnp.concatenate([blockdiag(c_re), -blockdiag(c_im)], axis=3).astype(BF16)
    lr = lbr.reshape(depth, 1, SSM_CH)
    li = lbi.reshape(depth, 1, SSM_CH)
    half = SUBLANES // 2
    a = jnp.broadcast_to(lr, (depth, SUBLANES, SSM_CH))
    bsw = jnp.concatenate([jnp.broadcast_to(-li, (depth, half, SSM_CH)), jnp.broadcast_to(li, (depth, half, SSM_CH))],
                          axis=1)
    return wb, a, bsw, wc


def kernel(x, mem, positions, norm_mix, w_in, ssm_lam_re, ssm_lam_im, ssm_b_re, ssm_b_im, ssm_c_re, ssm_c_im, ssm_d, ssm_log_step, ssm_w_glu, mla_q_norm, mla_kv_norm, mla_w_uq, mla_w_ukv, mla_w_o, hg_lb, hg_g_norm, hg_w_o, w_out, norm_cross, norm_mem, x_w_q, x_w_kv, x_w_o, norm_ffn, ffn_w_gate_up, ffn_w_down, norm_final):
    bsz, seq, d_model = x.shape
    depth = w_in.shape[0]
    n_mem = mem.shape[1]
    assert 2 * bsz == SUBLANES, "the s5 scan packs (re, im) x batch on the sublanes of one tile"
    n_tok = bsz * seq

    half = MLA_ROPE // 2
    invf = (ROPE_THETA ** (-jnp.arange(half, dtype=F32) / half)).reshape(half, 1)
    lb_p = jax.nn.softmax(hg_lb.astype(F32), axis=0)
    lower_bounds = jnp.cumsum(lb_p, axis=0) - lb_p[0:1]

    rows = lambda a: a.reshape(depth, 1, -1)
    bf = lambda a: a.astype(BF16)
    w_inp = _in_proj_weights(w_in)
    wuq, wuk, wuvt = _mla_weights(mla_w_uq, mla_w_ukv)
    wb, a, bsw, wc = _s5_weights(ssm_lam_re, ssm_lam_im, ssm_b_re, ssm_b_im, ssm_c_re, ssm_c_im, ssm_log_step)
    wglu, wmo, who, wout = bf(ssm_w_glu), bf(mla_w_o), bf(hg_w_o), bf(w_out)
    wxq, wxkv, wxo, wgu, wdn = bf(x_w_q), bf(x_w_kv), bf(x_w_o), bf(ffn_w_gate_up), bf(ffn_w_down)
    g_mix, g_q, g_kv, g_hg = rows(norm_mix), rows(mla_q_norm), rows(mla_kv_norm), rows(hg_g_norm)
    g_cross, g_mem, g_ffn = rows(norm_cross), rows(norm_mem), rows(norm_ffn)
    lbs, ssm_dr = rows(lower_bounds), rows(ssm_d)

    x2 = x.reshape(n_tok, d_model)
    mem2 = mem.reshape(bsz * n_mem, d_model)
    pos2 = positions.reshape(n_tok // IN_TILE, 1, IN_TILE)

    for l in range(depth):
        u, q, k, v, hq, hf, hi, hg, gates = _in_proj(x2, pos2, g_mix, w_inp, invf, g_q, wuq, g_kv, wuk, wuvt,
                                                     lbs, l, bsz, seq)
        ys = _s5(u.reshape(bsz, seq, SSM_WIDTH), wb, a, bsw, wc, ssm_dr, wglu, l, d_model).reshape(n_tok, d_model)
        om = _mla(q, k, v)
        oh = _hgrn(hq, hf, hi, hg, g_hg, l, bsz, seq)
        kv = _memkv(mem2, g_mem, wxkv, l, n_mem)
        x2 = _mid(x2, ys, om, oh, gates, kv, wmo, who, wout, g_cross, wxq, wxo, l, seq, n_mem)
        x2 = _ffn(x2, g_ffn, wgu, wdn, norm_final.reshape(1, -1), l, final_norm=(l == depth - 1))
    return x2.reshape(bsz, seq, d_model)
```
